```python
import math
import jax, jax.numpy as jnp
from jax import lax
import numpy as np


D_MODEL = 1024
BATCH = 32
SEQ = 2048
DEPTH = 1

MEM_LEN = 256

DA_HEAD_DIM = 64
DA_V_DIM = 2 * DA_HEAD_DIM
DA_HEADS = D_MODEL // DA_V_DIM
DA_W = DA_HEADS * DA_V_DIM
QBLK = 128

ML_HEADS = 4
ML_HEAD_DIM = D_MODEL // ML_HEADS
ML_W = ML_HEADS * ML_HEAD_DIM
ML_CHUNK = 64
ML_CONV = 4

MA_HEADS = 4
MA_HEAD_DIM = D_MODEL // MA_HEADS
MA_W = MA_HEADS * MA_HEAD_DIM

N_BRANCH = 3

REL_BUCKETS = 32
REL_MAX_DIST = 128

N_EXPERTS = 256
TOP_K = 8
N_GROUP = 8
TOPK_GROUP = 4
D_EXPERT = 256
D_SHARED = 256
ROUTED_SCALE = 2.5
EXPERT_BLOCK = 128

DEEPNORM_ALPHA = (2.0 * DEPTH) ** 0.25
DEEPNORM_BETA = (8.0 * DEPTH) ** -0.25

SPLIT_SIZES = (DA_W, DA_W, DA_W, 2 * ML_W, ML_W, ML_W, ML_HEADS, ML_HEADS, MA_W, N_BRANCH * D_MODEL)
SPLIT_OFFSETS = tuple(sum(SPLIT_SIZES[:i]) for i in range(len(SPLIT_SIZES) + 1))
C_IN = SPLIT_OFFSETS[-1]

kernel_name = "hybrid_diffattn_mlstm_memxattn_moe_deepnorm"


def layer_norm(x, g, b, eps=1e-5):
    xf = x.astype(jnp.float32)
    mu = xf.mean(-1, keepdims=True)
    var = jnp.square(xf - mu).mean(-1, keepdims=True)
    return ((xf - mu) * lax.rsqrt(var + eps) * g + b).astype(x.dtype)


def rms_norm(x, g, eps=1e-5):
    xf = x.astype(jnp.float32)
    return (xf * lax.rsqrt(jnp.mean(xf * xf, -1, keepdims=True) + eps) * g).astype(x.dtype)


def head_layer_norm(x, g, eps=1e-5):
    xf = x.astype(jnp.float32)
    mu = xf.mean(-1, keepdims=True)
    var = jnp.square(xf - mu).mean(-1, keepdims=True)
    return ((xf - mu) * lax.rsqrt(var + eps) * g).astype(x.dtype)


def causal_depthwise_conv(u, w):
    k = w.shape[0]
    s = u.shape[1]
    up = jnp.pad(u, ((0, 0), (k - 1, 0), (0, 0)))
    out = up[:, 0:s] * w[0]
    for j in range(1, k):
        out = out + up[:, j:j + s] * w[j]
    return out


def t5_bucket(dist):
    n = jnp.maximum(dist, 0)
    max_exact = REL_BUCKETS // 2
    large = max_exact + (jnp.log(jnp.maximum(n, 1).astype(jnp.float32) / max_exact)
                         / math.log(REL_MAX_DIST / max_exact)
                         * (REL_BUCKETS - max_exact)).astype(jnp.int32)
    large = jnp.minimum(large, REL_BUCKETS - 1)
    return jnp.where(n < max_exact, n, large)


def differential_attention(q, k, v, rel_bias, lam, lam_init, subln_g):
    b, s = q.shape[:2]
    nqb = s // QBLK
    q = (q * DA_HEAD_DIM ** -0.5).transpose(0, 2, 3, 1, 4)
    k = k.transpose(0, 2, 3, 1, 4)
    v = v.transpose(0, 2, 1, 3)
    qb = q.reshape(b, DA_HEADS, 2, nqb, QBLK, DA_HEAD_DIM).transpose(3, 0, 1, 2, 4, 5)
    k_pos = jnp.arange(s)

    def block(args):
        qi, bi = args
        q_pos = bi * QBLK + jnp.arange(QBLK)
        dist = q_pos[:, None] - k_pos[None, :]
        bias = rel_bias[t5_bucket(dist)].astype(jnp.float32).transpose(2, 0, 1)
        logits = jnp.einsum('bhmqd,bhmkd->bhmqk', qi, k).astype(jnp.float32) + bias[None, :, None]
        logits = jnp.where(dist >= 0, logits, -jnp.inf)
        p = jax.nn.softmax(logits, axis=-1)
        a = p[:, :, 0] - lam * p[:, :, 1]
        return jnp.einsum('bhqk,bhkd->bhqd', a.astype(v.dtype), v)

    o = lax.map(block, (qb, jnp.arange(nqb)))
    o = o.transpose(1, 0, 3, 2, 4).reshape(b, s, DA_HEADS, DA_V_DIM)
    o = rms_norm(o, subln_g) * (1.0 - lam_init)
    return o.reshape(b, s, DA_W)


def mlstm_chunkwise(q, k, v, i_pre, f_pre):
    b, s, h, d = q.shape
    nc = s // ML_CHUNK

    def to_chunks(t):
        return t.astype(jnp.float32).reshape(b, nc, ML_CHUNK, h, d).transpose(1, 0, 3, 2, 4)

    def gate_chunks(t):
        return t.astype(jnp.float32).reshape(b, nc, ML_CHUNK, h).transpose(1, 0, 3, 2)

    qc = to_chunks(q)
    kc = to_chunks(k) * d ** -0.5
    vc = to_chunks(v)
    ic = gate_chunks(i_pre)
    fc = jax.nn.log_sigmoid(gate_chunks(f_pre))
    causal = jnp.tril(jnp.ones((ML_CHUNK, ML_CHUNK), bool))

    def step(carry, xs):
        c, n, m = carry
        qi, ki, vi, ii, fi = xs
        bcum = jnp.cumsum(fi, axis=-1)
        dmat = jnp.where(causal, bcum[..., :, None] - bcum[..., None, :] + ii[..., None, :], -jnp.inf)
        inter = bcum + m[..., None]
        m_row = jnp.maximum(inter, dmat.max(-1))
        w = jnp.exp(dmat - m_row[..., None]) * jnp.einsum('bhjd,bhsd->bhjs', qi, ki)
        inter_w = jnp.exp(inter - m_row)
        num = inter_w[..., None] * jnp.einsum('bhvk,bhjk->bhjv', c, qi) + jnp.einsum('bhjs,bhsv->bhjv', w, vi)
        den = inter_w * jnp.einsum('bhk,bhjk->bhj', n, qi) + w.sum(-1)
        hout = num / jnp.maximum(jnp.abs(den), jnp.exp(-m_row))[..., None]
        total = bcum[..., -1]
        g = total[..., None] - bcum + ii
        m_new = jnp.maximum(total + m, g.max(-1))
        decay = jnp.exp(total + m - m_new)
        ws = jnp.exp(g - m_new[..., None])
        c_new = decay[..., None, None] * c + jnp.einsum('bhsv,bhsk->bhvk', ws[..., None] * vi, ki)
        n_new = decay[..., None] * n + jnp.einsum('bhs,bhsk->bhk', ws, ki)
        return (c_new, n_new, m_new), hout

    init = (jnp.zeros((b, h, d, d), jnp.float32), jnp.zeros((b, h, d), jnp.float32),
            jnp.zeros((b, h), jnp.float32))
    _, hs = lax.scan(step, init, (qc, kc, vc, ic, fc))
    return hs.transpose(1, 0, 3, 2, 4).reshape(b, s, h, d)


def memory_attention(q, mem, w_mem_kv):
    b, s = q.shape[:2]
    kv = mem @ w_mem_kv
    k = kv[..., :MA_W].reshape(b, -1, MA_HEADS, MA_HEAD_DIM)
    v = kv[..., MA_W:].reshape(b, -1, MA_HEADS, MA_HEAD_DIM)
    logits = jnp.einsum('bshd,bmhd->bhsm', q * MA_HEAD_DIM ** -0.5, k).astype(jnp.float32)
    p = jax.nn.softmax(logits, axis=-1)
    o = jnp.einsum('bhsm,bmhd->bshd', p.astype(v.dtype), v)
    return o.reshape(b, s, MA_W)


def token_mixer(x, mem, rel_bias, w_in, b_in, conv_w, diff_lambda, subln_g, mlstm_norm_g,
                w_mem_kv, w_branch, w_out, layer_idx):
    b, s, _ = x.shape
    u = x @ w_in + b_in
    da_q, da_k, da_v, ml_qk, ml_v, ml_o, ml_i, ml_f, ma_q, gate_pre = [
        u[..., SPLIT_OFFSETS[i]:SPLIT_OFFSETS[i + 1]] for i in range(len(SPLIT_SIZES))]

    lam_init = 0.8 - 0.6 * math.exp(-0.3 * layer_idx)
    lv = diff_lambda.astype(jnp.float32)
    lam = jnp.exp(jnp.sum(lv[0] * lv[1])) - jnp.exp(jnp.sum(lv[2] * lv[3])) + lam_init
    y_a = differential_attention(da_q.reshape(b, s, DA_HEADS, 2, DA_HEAD_DIM),
                                 da_k.reshape(b, s, DA_HEADS, 2, DA_HEAD_DIM),
                                 da_v.reshape(b, s, DA_HEADS, DA_V_DIM),
                                 rel_bias, lam, lam_init, subln_g)

    ml_qk = jax.nn.silu(causal_depthwise_conv(ml_qk, conv_w))
    h_m = mlstm_chunkwise(ml_qk[..., :ML_W].reshape(b, s, ML_HEADS, ML_HEAD_DIM),
                          ml_qk[..., ML_W:].reshape(b, s, ML_HEADS, ML_HEAD_DIM),
                          ml_v.reshape(b, s, ML_HEADS, ML_HEAD_DIM), ml_i, ml_f)
    h_m = head_layer_norm(h_m, mlstm_norm_g.reshape(ML_HEADS, ML_HEAD_DIM)).astype(x.dtype)
    y_m = jax.nn.sigmoid(ml_o) * h_m.reshape(b, s, ML_W)

    y_c = memory_attention(ma_q.reshape(b, s, MA_HEADS, MA_HEAD_DIM), mem, w_mem_kv)

    ys = jnp.stack([y_a, y_m, y_c], axis=2)
    proj = jnp.einsum('bsnc,ncd->bsnd', ys, w_branch)
    gates = jax.nn.sigmoid(gate_pre.reshape(b, s, N_BRANCH, D_MODEL))
    merged = jnp.sum(gates * proj, axis=2)
    return merged @ w_out


def swiglu(h, wg, wu, wd):
    return (jax.nn.silu(h @ wg) * (h @ wu)) @ wd


def moe(h, w_router, router_bias, w_e_gate, w_e_up, w_e_down, w_s_gate, w_s_up, w_s_down):
    b, s, d = h.shape
    hf = h.reshape(b * s, d)
    t = hf.shape[0]
    scores = jax.nn.sigmoid((hf @ w_router).astype(jnp.float32))
    choice = scores + router_bias.astype(jnp.float32)
    grp = choice.reshape(t, N_GROUP, N_EXPERTS // N_GROUP)
    grp_score = lax.top_k(grp, 2)[0].sum(-1)
    _, top_g = lax.top_k(grp_score, TOPK_GROUP)
    gmask = jax.nn.one_hot(top_g, N_GROUP).sum(1) > 0
    choice = jnp.where(jnp.repeat(gmask, N_EXPERTS // N_GROUP, axis=1), choice, -jnp.inf)
    _, idx = lax.top_k(choice, TOP_K)
    wsel = jnp.take_along_axis(scores, idx, axis=1)
    wsel = wsel / (wsel.sum(-1, keepdims=True) + 1e-20) * ROUTED_SCALE

    n = t * TOP_K
    flat_e = idx.reshape(n)
    flat_w = wsel.reshape(n)
    order = jnp.argsort(flat_e)
    se = flat_e[order]
    stok = (order // TOP_K).astype(jnp.int32)
    sw = flat_w[order]
    counts = jnp.zeros((N_EXPERTS,), jnp.int32).at[flat_e].add(1)
    starts = jnp.cumsum(counts) - counts
    nblk = (counts + EXPERT_BLOCK - 1) // EXPERT_BLOCK
    blk_end = jnp.cumsum(nblk)
    pstart = (blk_end - nblk) * EXPERT_BLOCK
    dest = pstart[se] + jnp.arange(n, dtype=jnp.int32) - starts[se]
    nb = -(-n // EXPERT_BLOCK) + N_EXPERTS
    buf_tok = jnp.zeros((nb * EXPERT_BLOCK,), jnp.int32).at[dest].set(stok)
    buf_w = jnp.zeros((nb * EXPERT_BLOCK,), sw.dtype).at[dest].set(sw)
    block_e = jnp.minimum(jnp.searchsorted(blk_end, jnp.arange(nb), side='right'), N_EXPERTS - 1)

    def step(y, blk):
        tok, w, e = blk
        ob = swiglu(hf[tok], w_e_gate[e], w_e_up[e], w_e_down[e])
        return y.at[tok].add(ob * w[:, None].astype(ob.dtype)), None

    routed, _ = lax.scan(step, jnp.zeros_like(hf),
                         (buf_tok.reshape(nb, EXPERT_BLOCK), buf_w.reshape(nb, EXPERT_BLOCK), block_e))
    shared = swiglu(hf, w_s_gate, w_s_up, w_s_down)
    return (shared + routed).reshape(b, s, d)


def setup_inputs(seed: int = 0) -> dict:
    key = jax.random.key(seed)
    ks = iter(jax.random.split(key, 32))

    def nrm(shape, scale):
        return jax.random.normal(next(ks), shape, jnp.float32) * scale

    col_scale = np.ones((C_IN,), np.float32)
    col_scale[SPLIT_OFFSETS[2]:SPLIT_OFFSETS[3]] = DEEPNORM_BETA
    col_scale[SPLIT_OFFSETS[4]:SPLIT_OFFSETS[5]] = DEEPNORM_BETA
    f_bias = np.zeros((C_IN,), np.float32)
    f_bias[SPLIT_OFFSETS[7]:SPLIT_OFFSETS[8]] = np.linspace(3.0, 6.0, ML_HEADS)
    kv_scale = np.concatenate([np.ones((MA_W,), np.float32), np.full((MA_W,), DEEPNORM_BETA, np.float32)])

    x = nrm((BATCH, SEQ, D_MODEL), 1.0)
    mem = nrm((BATCH, MEM_LEN, D_MODEL), 1.0)
    rel_bias = nrm((REL_BUCKETS, DA_HEADS), 0.5)
    w_in = nrm((DEPTH, D_MODEL, C_IN), D_MODEL ** -0.5) * jnp.asarray(col_scale)
    b_in = nrm((DEPTH, C_IN), 0.02) + jnp.asarray(f_bias)
    conv_w = nrm((DEPTH, ML_CONV, 2 * ML_W), ML_CONV ** -0.5)
    diff_lambda = nrm((DEPTH, 4, DA_HEAD_DIM), 0.1)
    subln_g = 1.0 + nrm((DEPTH, DA_V_DIM), 0.02)
    mlstm_norm_g = 1.0 + nrm((DEPTH, ML_W), 0.02)
    w_mem_kv = nrm((DEPTH, D_MODEL, 2 * MA_W), D_MODEL ** -0.5) * jnp.asarray(kv_scale)
    w_branch = nrm((DEPTH, N_BRANCH, DA_W, D_MODEL), DA_W ** -0.5)
    w_out = nrm((DEPTH, D_MODEL, D_MODEL), D_MODEL ** -0.5 * DEEPNORM_BETA)
    ln1_g = 1.0 + nrm((DEPTH, D_MODEL), 0.02)
    ln1_b = nrm((DEPTH, D_MODEL), 0.02)
    w_router = nrm((DEPTH, D_MODEL, N_EXPERTS), D_MODEL ** -0.5)
    router_bias = nrm((DEPTH, N_EXPERTS), 0.01)
    w_e_gate = nrm((DEPTH, N_EXPERTS, D_MODEL, D_EXPERT), D_MODEL ** -0.5)
    w_e_up = nrm((DEPTH, N_EXPERTS, D_MODEL, D_EXPERT), D_MODEL ** -0.5)
    w_e_down = nrm((DEPTH, N_EXPERTS, D_EXPERT, D_MODEL), D_EXPERT ** -0.5 * DEEPNORM_BETA)
    w_s_gate = nrm((DEPTH, D_MODEL, D_SHARED), D_MODEL ** -0.5)
    w_s_up = nrm((DEPTH, D_MODEL, D_SHARED), D_MODEL ** -0.5)
    w_s_down = nrm((DEPTH, D_SHARED, D_MODEL), D_SHARED ** -0.5 * DEEPNORM_BETA)
    ln2_g = 1.0 + nrm((DEPTH, D_MODEL), 0.02)
    ln2_b = nrm((DEPTH, D_MODEL), 0.02)
    return {"x": x, "mem": mem, "rel_bias": rel_bias, "w_in": w_in, "b_in": b_in,
            "conv_w": conv_w, "diff_lambda": diff_lambda, "subln_g": subln_g,
            "mlstm_norm_g": mlstm_norm_g, "w_mem_kv": w_mem_kv, "w_branch": w_branch,
            "w_out": w_out, "ln1_g": ln1_g, "ln1_b": ln1_b, "w_router": w_router,
            "router_bias": router_bias, "w_e_gate": w_e_gate, "w_e_up": w_e_up,
            "w_e_down": w_e_down, "w_s_gate": w_s_gate, "w_s_up": w_s_up,
            "w_s_down": w_s_down, "ln2_g": ln2_g, "ln2_b": ln2_b}


def reference(x, mem, rel_bias, w_in, b_in, conv_w, diff_lambda, subln_g, mlstm_norm_g,
              w_mem_kv, w_branch, w_out, ln1_g, ln1_b, w_router, router_bias, w_e_gate,
              w_e_up, w_e_down, w_s_gate, w_s_up, w_s_down, ln2_g, ln2_b):
    for l in range(DEPTH):
        mix = token_mixer(x, mem, rel_bias, w_in[l], b_in[l], conv_w[l], diff_lambda[l],
                          subln_g[l], mlstm_norm_g[l], w_mem_kv[l], w_branch[l], w_out[l], l)
        x = layer_norm(DEEPNORM_ALPHA * x + mix, ln1_g[l], ln1_b[l])
        ffn = moe(x, w_router[l], router_bias[l], w_e_gate[l], w_e_up[l], w_e_down[l],
                  w_s_gate[l], w_s_up[l], w_s_down[l])
        x = layer_norm(DEEPNORM_ALPHA * x + ffn, ln2_g[l], ln2_b[l])
    return x
```

```python
import functools
import math

import numpy as np
import jax
import jax.numpy as jnp
from jax import lax
from jax.experimental import pallas as pl
from jax.experimental.pallas import tpu as pltpu

F32 = jnp.float32
BF16 = jnp.bfloat16

D_MODEL = 1024
DEPTH = 1
DA_HEAD_DIM = 64
DA_V_DIM = 128
DA_HEADS = 8
ML_HEADS = 4
ML_HEAD_DIM = 256
ML_CONV = 4
MA_HEADS = 4
MA_HEAD_DIM = 256
N_BRANCH = 3
REL_BUCKETS = 32
REL_MAX_DIST = 128
N_EXPERTS = 256
TOP_K = 8
N_GROUP = 8
TOPK_GROUP = 4
D_EXPERT = 256
ROUTED_SCALE = 2.5
ALPHA = (2.0 * DEPTH) ** 0.25

LANES = 128
NEG = -1e30
VMEM_LIMIT = 56 * 1024 * 1024

OFF_DAQ, OFF_DAK, OFF_DAV = 0, 8, 16
OFF_MLQ, OFF_MLK, OFF_MLV, OFF_MLO, OFF_MAQ, OFF_GATE = 24, 32, 40, 48, 56, 64
N_MAIN = 88 * LANES


def _cparams(*sem):
    return pltpu.CompilerParams(dimension_semantics=sem, vmem_limit_bytes=VMEM_LIMIT)


def _layer_norm(z, g, b):
    mu = jnp.mean(z, axis=-1, keepdims=True)
    zc = z - mu
    var = jnp.mean(zc * zc, axis=-1, keepdims=True)
    return zc * lax.rsqrt(var + 1e-5) * g + b


def _proj_in_kernel(x_ref, w_ref, b_ref, wg_ref, bg_ref, u_ref, g_ref, xs_ref):
    @pl.when(pl.program_id(1) == 0)
    def _():
        xb = x_ref[...].astype(BF16)
        xs_ref[...] = xb
        g_ref[...] = jnp.dot(xb, wg_ref[...], preferred_element_type=F32) + bg_ref[...]

    acc = jnp.dot(xs_ref[...], w_ref[...], preferred_element_type=F32)
    u_ref[...] = (acc + b_ref[...]).astype(u_ref.dtype)


def _proj_in(x2, w_main, b_main, w_g, b_g, tm, tn):
    t, k = x2.shape
    n = w_main.shape[1]
    return pl.pallas_call(
        _proj_in_kernel,
        grid=(t // tm, n // tn),
        in_specs=[
            pl.BlockSpec((tm, k), lambda i, j: (i, 0)),
            pl.BlockSpec((k, tn), lambda i, j: (0, j)),
            pl.BlockSpec((1, tn), lambda i, j: (0, j)),
            pl.BlockSpec((k, LANES), lambda i, j: (0, 0)),
            pl.BlockSpec((1, LANES), lambda i, j: (0, 0)),
        ],
        out_specs=[
            pl.BlockSpec((tm, tn), lambda i, j: (i, j)),
            pl.BlockSpec((tm, LANES), lambda i, j: (i, 0)),
        ],
        out_shape=[jax.ShapeDtypeStruct((t, n), BF16), jax.ShapeDtypeStruct((t, LANES), F32)],
        scratch_shapes=[pltpu.VMEM((tm, k), BF16)],
        compiler_params=_cparams("parallel", "arbitrary"),
        name="proj_in",
    )(x2, w_main, b_main, w_g, b_g)


def _mm_kernel(x_ref, w_ref, o_ref):
    o_ref[...] = jnp.dot(x_ref[...].astype(BF16), w_ref[...],
                         preferred_element_type=F32).astype(o_ref.dtype)


def _mm(x2, w, out_dtype, tm, name):
    m, k = x2.shape
    n = w.shape[1]
    return pl.pallas_call(
        _mm_kernel,
        grid=(m // tm,),
        in_specs=[pl.BlockSpec((tm, k), lambda i: (i, 0)), pl.BlockSpec((k, n), lambda i: (0, 0))],
        out_specs=pl.BlockSpec((tm, n), lambda i: (i, 0)),
        out_shape=jax.ShapeDtypeStruct((m, n), out_dtype),
        compiler_params=_cparams("parallel"),
        name=name,
    )(x2, w)


def _dattn_kernel(q_ref, k_ref, v_ref, nb_ref, dl_ref, g_ref, o_ref, m_ref, l_ref, acc_ref, *,
                  tq, lam_init):
    i = pl.program_id(2)
    lane = lax.broadcasted_iota(jnp.int32, (tq, LANES), 1)
    qs = q_ref[0] * jnp.asarray(DA_HEAD_DIM ** -0.5, BF16)
    zero = jnp.zeros_like(qs)
    qm = (jnp.where(lane < DA_HEAD_DIM, qs, zero), jnp.where(lane >= DA_HEAD_DIM, qs, zero))
    m_ref[...] = jnp.full(m_ref.shape, NEG, F32)
    l_ref[...] = jnp.zeros(l_ref.shape, F32)
    acc_ref[...] = jnp.zeros(acc_ref.shape, F32)

    def step(j, bias):
        r = pl.multiple_of(j * tq, tq)
        kj = k_ref[0, pl.ds(r, tq), :]
        vj = v_ref[0, pl.ds(r, tq), :]
        for half in range(2):
            s = lax.dot_general(qm[half], kj, (((1,), (1,)), ((), ())), preferred_element_type=F32)
            if bias is not None:
                s = s + bias
            m_prev = m_ref[half]
            m_new = jnp.maximum(m_prev, jnp.max(s, axis=1, keepdims=True))
            p = jnp.exp(s - m_new)
            alpha = jnp.exp(m_prev - m_new)
            l_ref[half] = alpha * l_ref[half] + jnp.sum(p, axis=1, keepdims=True)
            acc_ref[half] = alpha * acc_ref[half] + jnp.dot(p.astype(BF16), vj,
                                                            preferred_element_type=F32)
            m_ref[half] = m_new

    def far(j, c):
        step(j, None)
        return c

    lax.fori_loop(0, i - 1, far, 0)

    @pl.when(i >= 1)
    def _():
        step(i - 1, nb_ref[0, 0])

    step(i, nb_ref[0, 1])

    dl = dl_ref[...]
    lam = (jnp.exp(jnp.sum(dl[0:1] * dl[1:2], axis=1, keepdims=True))
           - jnp.exp(jnp.sum(dl[2:3] * dl[3:4], axis=1, keepdims=True)) + lam_init)
    o = acc_ref[0] * (1.0 / l_ref[0]) - lam * (acc_ref[1] * (1.0 / l_ref[1]))
    ms = jnp.mean(o * o, axis=1, keepdims=True)
    y = o * lax.rsqrt(ms + 1e-5) * (g_ref[...] * (1.0 - lam_init))
    o_ref[0] = y.astype(o_ref.dtype)


def _t5_bucket_np(dist):
    n = np.maximum(dist, 0)
    max_exact = REL_BUCKETS // 2
    large = max_exact + (np.log(np.maximum(n, 1).astype(np.float32) / max_exact)
                         / math.log(REL_MAX_DIST / max_exact)
                         * (REL_BUCKETS - max_exact)).astype(np.int32)
    large = np.minimum(large, REL_BUCKETS - 1)
    return np.where(n < max_exact, n, large)


def _near_bias(rel_bias, tq):
    r = np.arange(tq)[:, None]
    c = np.arange(tq)[None, :]
    d_left = r + tq - c
    d_diag = r - c
    assert _t5_bucket_np(np.array([tq + 1]))[0] == REL_BUCKETS - 1
    rb = rel_bias.astype(F32)
    far = rb[REL_BUCKETS - 1]
    left = rb[_t5_bucket_np(d_left)] - far
    diag = rb[_t5_bucket_np(d_diag)] - far
    diag = jnp.where((d_diag >= 0)[:, :, None], diag, NEG)
    return jnp.stack([left, diag], axis=0).transpose(3, 0, 1, 2)


def _diff_attention(u3, nb, diff_lambda, subln_g, tq, lam_init):
    b, s, _ = u3.shape
    kern = functools.partial(_dattn_kernel, tq=tq, lam_init=lam_init)
    return pl.pallas_call(
        kern,
        grid=(b, DA_HEADS, s // tq),
        in_specs=[
            pl.BlockSpec((1, tq, LANES), lambda bi, h, i: (bi, i, OFF_DAQ + h)),
            pl.BlockSpec((1, s, LANES), lambda bi, h, i: (bi, 0, OFF_DAK + h)),
            pl.BlockSpec((1, s, LANES), lambda bi, h, i: (bi, 0, OFF_DAV + h)),
            pl.BlockSpec((1, 2, tq, tq), lambda bi, h, i: (h, 0, 0, 0)),
            pl.BlockSpec((4, DA_HEAD_DIM), lambda bi, h, i: (0, 0)),
            pl.BlockSpec((1, DA_V_DIM), lambda bi, h, i: (0, 0)),
        ],
        out_specs=pl.BlockSpec((1, tq, LANES), lambda bi, h, i: (bi, i, h)),
        out_shape=jax.ShapeDtypeStruct((b, s, DA_HEADS * DA_V_DIM), BF16),
        scratch_shapes=[pltpu.VMEM((2, tq, 1), F32), pltpu.VMEM((2, tq, 1), F32),
                        pltpu.VMEM((2, tq, DA_V_DIM), F32)],
        compiler_params=_cparams("parallel", "parallel", "arbitrary"),
        name="diff_attn",
    )(u3, u3, u3, nb, diff_lambda, subln_g)


def _mlstm_kernel(q_ref, k_ref, v_ref, og_ref, g_ref, cwq_ref, cwk_ref, ng_ref, y_ref,
                  ct_ref, gts_ref, *, chunk, seq):
    h = pl.program_id(1)
    nc = seq // chunk
    hd = ML_HEAD_DIM
    lane = lax.broadcasted_iota(jnp.int32, (chunk, LANES), 1)
    row = lax.broadcasted_iota(jnp.int32, (chunk, chunk), 0)
    col = lax.broadcasted_iota(jnp.int32, (chunk, chunk), 1)
    causal = row >= col
    tri = causal.astype(F32)
    cwq = cwq_ref[...]
    cwk = cwk_ref[...]
    ng = ng_ref[...]
    ct_ref[...] = jnp.zeros(ct_ref.shape, F32)

    def conv_silu(ref, cw, r0, c):
        cur = ref[0, pl.ds(r0, chunk), :].astype(F32)
        p0 = pl.multiple_of(jnp.maximum(r0 - 16, 0), 16)
        prev = ref[0, pl.ds(p0, 16), :].astype(F32)
        prev = jnp.where(c > 0, prev, 0.0)
        x = jnp.concatenate([prev, cur], axis=0)
        out = cw[ML_CONV - 1:ML_CONV] * cur
        for tap in range(ML_CONV - 1):
            shifted = pltpu.roll(x, ML_CONV - 1 - tap, 0)[16:]
            out = out + cw[tap:tap + 1] * shifted
        return out * jax.nn.sigmoid(out)

    def body(c, carry):
        m, n = carry
        r0 = pl.multiple_of(c * chunk, chunk)
        g = g_ref[0, pl.ds(r0, chunk), :]
        logf = jnp.minimum(g, 0.0) - jnp.log(1.0 + jnp.exp(-jnp.abs(g)))
        bc = jnp.dot(tri, logf, precision=lax.Precision.HIGHEST, preferred_element_type=F32)
        gts_ref[...] = jnp.where(lane < ML_HEADS, g, bc).T
        irow = gts_ref[pl.ds(h, 1), :]
        brow = gts_ref[pl.ds(ML_HEADS + h, 1), :]
        bcol = jnp.sum(jnp.where(lane == ML_HEADS + h, bc, 0.0), axis=1, keepdims=True)
        icol = jnp.sum(jnp.where(lane == h, g, 0.0), axis=1, keepdims=True)

        dmat = jnp.where(causal, bcol - brow + irow, NEG)
        inter = bcol + m
        m_row = jnp.maximum(inter, jnp.max(dmat, axis=1, keepdims=True))

        q = conv_silu(q_ref, cwq, r0, c)
        k = conv_silu(k_ref, cwk, r0, c) * (hd ** -0.5)
        qb = q.astype(BF16)
        kb = k.astype(BF16)
        vb = v_ref[0, pl.ds(r0, chunk), :]
        sqk = lax.dot_general(qb, kb, (((1,), (1,)), ((), ())), preferred_element_type=F32)
        w = jnp.exp(dmat - m_row) * sqk
        inter_w = jnp.exp(inter - m_row)
        ct = ct_ref[...]
        num = (inter_w * jnp.dot(qb, ct.astype(BF16), preferred_element_type=F32)
               + jnp.dot(w.astype(BF16), vb, preferred_element_type=F32))
        den = inter_w * jnp.sum(q * n, axis=1, keepdims=True) + jnp.sum(w, axis=1, keepdims=True)
        hout = num / jnp.maximum(jnp.abs(den), jnp.exp(-m_row))
        mu = jnp.mean(hout, axis=1, keepdims=True)
        hc = hout - mu
        var = jnp.mean(hc * hc, axis=1, keepdims=True)
        hn = hc * lax.rsqrt(var + 1e-5) * ng
        og = og_ref[0, pl.ds(r0, chunk), :].astype(F32)
        y_ref[0, pl.ds(r0, chunk), :] = (jax.nn.sigmoid(og) * hn).astype(y_ref.dtype)

        total = brow[:, chunk - 1:chunk]
        grow = total - brow + irow
        m_new = jnp.maximum(total + m, jnp.max(grow, axis=1, keepdims=True))
        decay = jnp.exp(total + m - m_new)
        ws = jnp.exp(total - bcol + icol - m_new)
        wsv = (ws * vb.astype(F32)).astype(BF16)
        ct_ref[...] = decay * ct + jnp.dot(k.T.astype(BF16), wsv, preferred_element_type=F32)
        n_new = decay * n + jnp.sum(ws * k, axis=0, keepdims=True)
        return m_new, n_new

    lax.fori_loop(0, nc, body, (jnp.zeros((1, 1), F32), jnp.zeros((1, hd), F32)))


def _mlstm(u3, gates3, conv_w, norm_g, chunk):
    b, s, _ = u3.shape
    hd = ML_HEAD_DIM
    nq = hd // LANES
    kern = functools.partial(_mlstm_kernel, chunk=chunk, seq=s)

    def ublock(off):
        return pl.BlockSpec((1, s, hd), lambda bi, h: (bi, 0, off // nq + h))

    return pl.pallas_call(
        kern,
        grid=(b, ML_HEADS),
        in_specs=[
            ublock(OFF_MLQ), ublock(OFF_MLK), ublock(OFF_MLV), ublock(OFF_MLO),
            pl.BlockSpec((1, s, LANES), lambda bi, h: (bi, 0, 0)),
            pl.BlockSpec((ML_CONV, hd), lambda bi, h: (0, h)),
            pl.BlockSpec((ML_CONV, hd), lambda bi, h: (0, ML_HEADS + h)),
            pl.BlockSpec((1, hd), lambda bi, h: (0, h)),
        ],
        out_specs=pl.BlockSpec((1, s, hd), lambda bi, h: (bi, 0, h)),
        out_shape=jax.ShapeDtypeStruct((b, s, ML_HEADS * hd), BF16),
        scratch_shapes=[pltpu.VMEM((hd, hd), F32), pltpu.VMEM((LANES, chunk), F32)],
        compiler_params=_cparams("parallel", "arbitrary"),
        name="mlstm",
    )(u3, u3, u3, u3, gates3, conv_w, conv_w, norm_g)


def _memattn_kernel(q_ref, k_ref, v_ref, o_ref, *, tq, seq):
    kb = k_ref[0]
    vb = v_ref[0]
    scale = jnp.asarray(MA_HEAD_DIM ** -0.5, BF16)
    for t in range(seq // tq):
        q = q_ref[0, t * tq:(t + 1) * tq, :] * scale
        s = lax.dot_general(q, kb, (((1,), (1,)), ((), ())), preferred_element_type=F32)
        p = jnp.exp(s - jnp.max(s, axis=1, keepdims=True))
        inv = 1.0 / jnp.sum(p, axis=1, keepdims=True)
        o = jnp.dot(p.astype(BF16), vb, preferred_element_type=F32) * inv
        o_ref[0, t * tq:(t + 1) * tq, :] = o.astype(o_ref.dtype)


def _mem_attention(u3, kv3, tq):
    b, s, _ = u3.shape
    mlen = kv3.shape[1]
    hd = MA_HEAD_DIM
    nq = hd // LANES
    kern = functools.partial(_memattn_kernel, tq=tq, seq=s)
    return pl.pallas_call(
        kern,
        grid=(b, MA_HEADS),
        in_specs=[
            pl.BlockSpec((1, s, hd), lambda bi, h: (bi, 0, OFF_MAQ // nq + h)),
            pl.BlockSpec((1, mlen, hd), lambda bi, h: (bi, 0, h)),
            pl.BlockSpec((1, mlen, hd), lambda bi, h: (bi, 0, MA_HEADS + h)),
        ],
        out_specs=pl.BlockSpec((1, s, hd), lambda bi, h: (bi, 0, h)),
        out_shape=jax.ShapeDtypeStruct((b, s, MA_HEADS * hd), BF16),
        compiler_params=_cparams("parallel", "parallel"),
        name="mem_attn",
    )(u3, kv3, kv3)


def _merge_kernel(ya_ref, ym_ref, yc_ref, g0_ref, g1_ref, g2_ref, x_ref, wb_ref, wo_ref,
                  lg_ref, lb_ref, x1_ref, x1b_ref):
    acc = None
    for n, (y_ref, g_ref) in enumerate(((ya_ref, g0_ref), (ym_ref, g1_ref), (yc_ref, g2_ref))):
        pr = jnp.dot(y_ref[...], wb_ref[n], preferred_element_type=F32)
        t = jax.nn.sigmoid(g_ref[...].astype(F32)) * pr
        acc = t if acc is None else acc + t
    out = jnp.dot(acc.astype(BF16), wo_ref[...], preferred_element_type=F32)
    x1 = _layer_norm(ALPHA * x_ref[...] + out, lg_ref[...], lb_ref[...])
    x1_ref[...] = x1
    x1b_ref[...] = x1.astype(BF16)


def _merge(ya, ym, yc, u2, x2, wb, wo, lg, lb, tm):
    t, d = x2.shape
    gb = OFF_GATE * LANES // d

    def rows(i):
        return (i, 0)

    return pl.pallas_call(
        _merge_kernel,
        grid=(t // tm,),
        in_specs=[
            pl.BlockSpec((tm, d), rows), pl.BlockSpec((tm, d), rows), pl.BlockSpec((tm, d), rows),
            pl.BlockSpec((tm, d), lambda i: (i, gb)),
            pl.BlockSpec((tm, d), lambda i: (i, gb + 1)),
            pl.BlockSpec((tm, d), lambda i: (i, gb + 2)),
            pl.BlockSpec((tm, d), rows),
            pl.BlockSpec((N_BRANCH, d, d), lambda i: (0, 0, 0)),
            pl.BlockSpec((d, d), lambda i: (0, 0)),
            pl.BlockSpec((1, d), lambda i: (0, 0)),
            pl.BlockSpec((1, d), lambda i: (0, 0)),
        ],
        out_specs=[pl.BlockSpec((tm, d), rows), pl.BlockSpec((tm, d), rows)],
        out_shape=[jax.ShapeDtypeStruct((t, d), F32), jax.ShapeDtypeStruct((t, d), BF16)],
        compiler_params=_cparams("parallel"),
        name="merge_ln1",
    )(ya, ym, yc, u2, u2, u2, x2, wb, wo, lg, lb)


def _expert_kernel(be_ref, nu_ref, x_ref, wg_ref, wu_ref, wd_ref, o_ref):
    i = pl.program_id(0)

    @pl.when(i < nu_ref[0])
    def _():
        xb = x_ref[...]
        hg = jnp.dot(xb, wg_ref[0].astype(BF16), preferred_element_type=F32)
        hu = jnp.dot(xb, wu_ref[0].astype(BF16), preferred_element_type=F32)
        act = (hg * jax.nn.sigmoid(hg) * hu).astype(BF16)
        o_ref[...] = jnp.dot(act, wd_ref[0].astype(BF16),
                             preferred_element_type=F32).astype(o_ref.dtype)

    @pl.when(i >= nu_ref[0])
    def _():
        o_ref[...] = jnp.zeros(o_ref.shape, o_ref.dtype)


def _experts(block_e, nused, xs, w_gate, w_up, w_down, bm):
    npad, d = xs.shape
    de = w_gate.shape[2]
    grid_spec = pltpu.PrefetchScalarGridSpec(
        num_scalar_prefetch=2,
        grid=(npad // bm,),
        in_specs=[
            pl.BlockSpec((bm, d), lambda i, be, nu: (i, 0)),
            pl.BlockSpec((1, d, de), lambda i, be, nu: (be[i], 0, 0)),
            pl.BlockSpec((1, d, de), lambda i, be, nu: (be[i], 0, 0)),
            pl.BlockSpec((1, de, d), lambda i, be, nu: (be[i], 0, 0)),
        ],
        out_specs=pl.BlockSpec((bm, d), lambda i, be, nu: (i, 0)),
    )
    return pl.pallas_call(
        _expert_kernel,
        grid_spec=grid_spec,
        out_shape=jax.ShapeDtypeStruct((npad, d), BF16),
        compiler_params=_cparams("arbitrary"),
        name="experts",
    )(block_e, nused, xs, w_gate, w_up, w_down)


def _ffn_out_kernel(xb_ref, x1_ref, r_ref, wg_ref, wu_ref, wd_ref, lg_ref, lb_ref, o_ref):
    xb = xb_ref[...]
    hg = jnp.dot(xb, wg_ref[...], preferred_element_type=F32)
    hu = jnp.dot(xb, wu_ref[...], preferred_element_type=F32)
    act = (hg * jax.nn.sigmoid(hg) * hu).astype(BF16)
    sh = jnp.dot(act, wd_ref[...], preferred_element_type=F32)
    z = ALPHA * x1_ref[...] + sh + r_ref[...]
    o_ref[...] = _layer_norm(z, lg_ref[...], lb_ref[...])


def _ffn_out(x1b, x1, routed, wg, wu, wd, lg, lb, tm):
    t, d = x1.shape
    ds = wg.shape[1]

    def rows(i):
        return (i, 0)

    def whole(i):
        return (0, 0)

    return pl.pallas_call(
        _ffn_out_kernel,
        grid=(t // tm,),
        in_specs=[
            pl.BlockSpec((tm, d), rows), pl.BlockSpec((tm, d), rows), pl.BlockSpec((tm, d), rows),
            pl.BlockSpec((d, ds), whole), pl.BlockSpec((d, ds), whole), pl.BlockSpec((ds, d), whole),
            pl.BlockSpec((1, d), whole), pl.BlockSpec((1, d), whole),
        ],
        out_specs=pl.BlockSpec((tm, d), rows),
        out_shape=jax.ShapeDtypeStruct((t, d), F32),
        compiler_params=_cparams("parallel"),
        name="ffn_out_ln2",
    )(x1b, x1, routed, wg, wu, wd, lg, lb)


def _route(logits, router_bias, bm):
    t = logits.shape[0]
    scores = jax.nn.sigmoid(logits)
    choice = scores + router_bias.astype(F32)
    grp = choice.reshape(t, N_GROUP, N_EXPERTS // N_GROUP)
    grp_score = lax.top_k(grp, 2)[0].sum(-1)
    _, top_g = lax.top_k(grp_score, TOPK_GROUP)
    gmask = jax.nn.one_hot(top_g, N_GROUP).sum(1) > 0
    choice = jnp.where(jnp.repeat(gmask, N_EXPERTS // N_GROUP, axis=1), choice, -jnp.inf)
    _, idx = lax.top_k(choice, TOP_K)
    wsel = jnp.take_along_axis(scores, idx, axis=1)
    wsel = wsel / (wsel.sum(-1, keepdims=True) + 1e-20) * ROUTED_SCALE

    n = t * TOP_K
    flat_e = idx.reshape(n)
    order = jnp.argsort(flat_e)
    se = flat_e[order]
    stok = (order // TOP_K).astype(jnp.int32)
    counts = jnp.zeros((N_EXPERTS,), jnp.int32).at[flat_e].add(1)
    starts = jnp.cumsum(counts) - counts
    nblk = (counts + bm - 1) // bm
    blk_end = jnp.cumsum(nblk)
    pstart = (blk_end - nblk) * bm
    dest_sorted = pstart[se] + jnp.arange(n, dtype=jnp.int32) - starts[se]
    nb = -(-n // bm) + N_EXPERTS
    buf_tok = jnp.zeros((nb * bm,), jnp.int32).at[dest_sorted].set(stok)
    block_e = jnp.minimum(jnp.searchsorted(blk_end, jnp.arange(nb), side='right'),
                          N_EXPERTS - 1).astype(jnp.int32)
    dest_tk = jnp.zeros((n,), jnp.int32).at[order].set(dest_sorted).reshape(t, TOP_K)
    nused = blk_end[-1:].astype(jnp.int32)
    return wsel, buf_tok, block_e, nused, dest_tk


def _layer(x, mem, rel_bias, w_in, b_in, conv_w, diff_lambda, subln_g, mlstm_norm_g, w_mem_kv,
           w_branch, w_out, ln1_g, ln1_b, w_router, router_bias, w_e_gate, w_e_up, w_e_down,
           w_s_gate, w_s_up, w_s_down, ln2_g, ln2_b, layer_idx, cfg):
    b, s, d = x.shape
    t = b * s
    x2 = x.reshape(t, d)

    g0 = (OFF_MLO + 8) * LANES
    w_main = jnp.concatenate([w_in[:, :g0], w_in[:, g0 + 2 * ML_HEADS:]], axis=1).astype(BF16)
    b_main = jnp.concatenate([b_in[:g0], b_in[g0 + 2 * ML_HEADS:]])[None, :]
    w_g = jnp.pad(w_in[:, g0:g0 + 2 * ML_HEADS], ((0, 0), (0, LANES - 2 * ML_HEADS))).astype(BF16)
    b_g = jnp.pad(b_in[g0:g0 + 2 * ML_HEADS], (0, LANES - 2 * ML_HEADS))[None, :]

    u2, gates2 = _proj_in(x2, w_main, b_main, w_g, b_g, cfg["proj_tm"], cfg["proj_tn"])
    u3 = u2.reshape(b, s, N_MAIN)
    gates3 = gates2.reshape(b, s, LANES)

    lam_init = 0.8 - 0.6 * math.exp(-0.3 * layer_idx)
    nb = _near_bias(rel_bias, cfg["attn_tq"])
    y_a = _diff_attention(u3, nb, diff_lambda, subln_g[None, :], cfg["attn_tq"], lam_init)
    y_m = _mlstm(u3, gates3, conv_w, mlstm_norm_g[None, :], cfg["ml_chunk"])
    kv = _mm(mem.reshape(-1, d), w_mem_kv.astype(BF16), BF16, cfg["kv_tm"], "mem_kv")
    y_c = _mem_attention(u3, kv.reshape(b, -1, 2 * MA_HEADS * MA_HEAD_DIM), cfg["ma_tq"])

    x1, x1b = _merge(y_a.reshape(t, d), y_m.reshape(t, d), y_c.reshape(t, d), u2, x2,
                     w_branch.astype(BF16), w_out.astype(BF16), ln1_g[None, :], ln1_b[None, :],
                     cfg["merge_tm"])

    logits = _mm(x1b, w_router.astype(BF16), F32, cfg["router_tm"], "router")
    bm = cfg["expert_bm"]
    wsel, buf_tok, block_e, nused, dest_tk = _route(logits, router_bias, bm)
    xs = jnp.take(x1b, buf_tok, axis=0)
    ys = _experts(block_e, nused, xs, w_e_gate, w_e_up, w_e_down, bm)
    routed = jnp.einsum('tk,tkd->td', wsel, jnp.take(ys, dest_tk, axis=0).astype(F32))

    out = _ffn_out(x1b, x1, routed, w_s_gate.astype(BF16), w_s_up.astype(BF16),
                   w_s_down.astype(BF16), ln2_g[None, :], ln2_b[None, :], cfg["ffn_tm"])
    return out.reshape(b, s, d)


def _config(b, s):
    t = b * s
    return {
        "proj_tm": min(1024, t), "proj_tn": 1024,
        "attn_tq": 256, "ml_chunk": 256, "kv_tm": 512, "ma_tq": min(512, s),
        "merge_tm": 256, "router_tm": min(2048, t), "expert_bm": 512, "ffn_tm": 512,
    }


def kernel(x, mem, rel_bias, w_in, b_in, conv_w, diff_lambda, subln_g, mlstm_norm_g, w_mem_kv,
           w_branch, w_out, ln1_g, ln1_b, w_router, router_bias, w_e_gate, w_e_up, w_e_down,
           w_s_gate, w_s_up, w_s_down, ln2_g, ln2_b):
    cfg = _config(x.shape[0], x.shape[1])
    for l in range(DEPTH):
        x = _layer(x, mem, rel_bias, w_in[l], b_in[l], conv_w[l], diff_lambda[l], subln_g[l],
                   mlstm_norm_g[l], w_mem_kv[l], w_branch[l], w_out[l], ln1_g[l], ln1_b[l],
                   w_router[l], router_bias[l], w_e_gate[l], w_e_up[l], w_e_down[l],
                   w_s_gate[l], w_s_up[l], w_s_down[l], ln2_g[l], ln2_b[l], l, cfg)
    return x
```

```python
import functools
import math

import numpy as np
import jax
import jax.numpy as jnp
from jax import lax
from jax.experimental import pallas as pl
from jax.experimental.pallas import tpu as pltpu

F32 = jnp.float32
BF16 = jnp.bfloat16
U32 = jnp.uint32
I32 = jnp.int32

D_MODEL = 1024
DEPTH = 1
DA_HEAD_DIM = 64
DA_V_DIM = 128
DA_HEADS = 8
ML_HEADS = 4
ML_HEAD_DIM = 256
ML_CONV = 4
MA_HEADS = 4
MA_HEAD_DIM = 256
N_BRANCH = 3
REL_BUCKETS = 32
REL_MAX_DIST = 128
N_EXPERTS = 256
TOP_K = 8
N_GROUP = 8
TOPK_GROUP = 4
D_EXPERT = 256
ROUTED_SCALE = 2.5
ALPHA = (2.0 * DEPTH) ** 0.25

LANES = 128
NEG = -1e30
VMEM_LIMIT = 56 * 1024 * 1024

OFF_DAQ, OFF_DAK, OFF_DAV = 0, 8, 16
OFF_MLQ, OFF_MLK, OFF_MLV, OFF_MLO, OFF_MAQ, OFF_GATE = 24, 32, 40, 48, 56, 64
N_MAIN = 88 * LANES


def _cparams(*sem):
    return pltpu.CompilerParams(dimension_semantics=sem, vmem_limit_bytes=VMEM_LIMIT)


def _layer_norm(z, g, b):
    mu = jnp.mean(z, axis=-1, keepdims=True)
    zc = z - mu
    var = jnp.mean(zc * zc, axis=-1, keepdims=True)
    return zc * lax.rsqrt(var + 1e-5) * g + b


def _proj_in_kernel(x_ref, w_ref, b_ref, wg_ref, bg_ref, u_ref, g_ref, xs_ref):
    @pl.when(pl.program_id(1) == 0)
    def _():
        xb = x_ref[...].astype(BF16)
        xs_ref[...] = xb
        g_ref[...] = jnp.dot(xb, wg_ref[...], preferred_element_type=F32) + bg_ref[...]

    acc = jnp.dot(xs_ref[...], w_ref[...], preferred_element_type=F32)
    u_ref[...] = (acc + b_ref[...]).astype(u_ref.dtype)


def _proj_in(x2, w_main, b_main, w_g, b_g, tm, tn):
    t, k = x2.shape
    n = w_main.shape[1]
    return pl.pallas_call(
        _proj_in_kernel,
        grid=(t // tm, n // tn),
        in_specs=[
            pl.BlockSpec((tm, k), lambda i, j: (i, 0)),
            pl.BlockSpec((k, tn), lambda i, j: (0, j)),
            pl.BlockSpec((1, tn), lambda i, j: (0, j)),
            pl.BlockSpec((k, LANES), lambda i, j: (0, 0)),
            pl.BlockSpec((1, LANES), lambda i, j: (0, 0)),
        ],
        out_specs=[
            pl.BlockSpec((tm, tn), lambda i, j: (i, j)),
            pl.BlockSpec((tm, LANES), lambda i, j: (i, 0)),
        ],
        out_shape=[jax.ShapeDtypeStruct((t, n), BF16), jax.ShapeDtypeStruct((t, LANES), F32)],
        scratch_shapes=[pltpu.VMEM((tm, k), BF16)],
        compiler_params=_cparams("parallel", "arbitrary"),
        name="proj_in",
    )(x2, w_main, b_main, w_g, b_g)


def _mm_kernel(x_ref, w_ref, o_ref):
    o_ref[...] = jnp.dot(x_ref[...].astype(BF16), w_ref[...],
                         preferred_element_type=F32).astype(o_ref.dtype)


def _mm(x2, w, out_dtype, tm, name):
    m, k = x2.shape
    n = w.shape[1]
    return pl.pallas_call(
        _mm_kernel,
        grid=(m // tm,),
        in_specs=[pl.BlockSpec((tm, k), lambda i: (i, 0)), pl.BlockSpec((k, n), lambda i: (0, 0))],
        out_specs=pl.BlockSpec((tm, n), lambda i: (i, 0)),
        out_shape=jax.ShapeDtypeStruct((m, n), out_dtype),
        compiler_params=_cparams("parallel"),
        name=name,
    )(x2, w)


def _dattn_kernel(q_ref, k_ref, v_ref, nb_ref, dl_ref, g_ref, o_ref, m_ref, l_ref, acc_ref, *,
                  tq, lam_init):
    i = pl.program_id(2)
    lane = lax.broadcasted_iota(jnp.int32, (tq, LANES), 1)
    qs = q_ref[0] * jnp.asarray(DA_HEAD_DIM ** -0.5, BF16)
    zero = jnp.zeros_like(qs)
    qm = (jnp.where(lane < DA_HEAD_DIM, qs, zero), jnp.where(lane >= DA_HEAD_DIM, qs, zero))
    m_ref[...] = jnp.full(m_ref.shape, NEG, F32)
    l_ref[...] = jnp.zeros(l_ref.shape, F32)
    acc_ref[...] = jnp.zeros(acc_ref.shape, F32)

    def step(j, bias):
        r = pl.multiple_of(j * tq, tq)
        kj = k_ref[0, pl.ds(r, tq), :]
        vj = v_ref[0, pl.ds(r, tq), :]
        for half in range(2):
            s = lax.dot_general(qm[half], kj, (((1,), (1,)), ((), ())), preferred_element_type=F32)
            if bias is not None:
                s = s + bias
            m_prev = m_ref[half]
            m_new = jnp.maximum(m_prev, jnp.max(s, axis=1, keepdims=True))
            p = jnp.exp(s - m_new)
            alpha = jnp.exp(m_prev - m_new)
            l_ref[half] = alpha * l_ref[half] + jnp.sum(p, axis=1, keepdims=True)
            acc_ref[half] = alpha * acc_ref[half] + jnp.dot(p.astype(BF16), vj,
                                                            preferred_element_type=F32)
            m_ref[half] = m_new

    def far(j, c):
        step(j, None)
        return c

    lax.fori_loop(0, i - 1, far, 0)

    @pl.when(i >= 1)
    def _():
        step(i - 1, nb_ref[0, 0])

    step(i, nb_ref[0, 1])

    dl = dl_ref[...]
    lam = (jnp.exp(jnp.sum(dl[0:1] * dl[1:2], axis=1, keepdims=True))
           - jnp.exp(jnp.sum(dl[2:3] * dl[3:4], axis=1, keepdims=True)) + lam_init)
    o = acc_ref[0] * (1.0 / l_ref[0]) - lam * (acc_ref[1] * (1.0 / l_ref[1]))
    ms = jnp.mean(o * o, axis=1, keepdims=True)
    y = o * lax.rsqrt(ms + 1e-5) * (g_ref[...] * (1.0 - lam_init))
    o_ref[0] = y.astype(o_ref.dtype)


def _t5_bucket_np(dist):
    n = np.maximum(dist, 0)
    max_exact = REL_BUCKETS // 2
    large = max_exact + (np.log(np.maximum(n, 1).astype(np.float32) / max_exact)
                         / math.log(REL_MAX_DIST / max_exact)
                         * (REL_BUCKETS - max_exact)).astype(np.int32)
    large = np.minimum(large, REL_BUCKETS - 1)
    return np.where(n < max_exact, n, large)


def _near_bias(rel_bias, tq):
    r = np.arange(tq)[:, None]
    c = np.arange(tq)[None, :]
    d_left = r + tq - c
    d_diag = r - c
    assert _t5_bucket_np(np.array([tq + 1]))[0] == REL_BUCKETS - 1
    rb = rel_bias.astype(F32)
    far = rb[REL_BUCKETS - 1]
    left = rb[_t5_bucket_np(d_left)] - far
    diag = rb[_t5_bucket_np(d_diag)] - far
    diag = jnp.where((d_diag >= 0)[:, :, None], diag, NEG)
    return jnp.stack([left, diag], axis=0).transpose(3, 0, 1, 2)


def _diff_attention(u3, nb, diff_lambda, subln_g, tq, lam_init):
    b, s, _ = u3.shape
    kern = functools.partial(_dattn_kernel, tq=tq, lam_init=lam_init)
    return pl.pallas_call(
        kern,
        grid=(b, DA_HEADS, s // tq),
        in_specs=[
            pl.BlockSpec((1, tq, LANES), lambda bi, h, i: (bi, i, OFF_DAQ + h)),
            pl.BlockSpec((1, s, LANES), lambda bi, h, i: (bi, 0, OFF_DAK + h)),
            pl.BlockSpec((1, s, LANES), lambda bi, h, i: (bi, 0, OFF_DAV + h)),
            pl.BlockSpec((1, 2, tq, tq), lambda bi, h, i: (h, 0, 0, 0)),
            pl.BlockSpec((4, DA_HEAD_DIM), lambda bi, h, i: (0, 0)),
            pl.BlockSpec((1, DA_V_DIM), lambda bi, h, i: (0, 0)),
        ],
        out_specs=pl.BlockSpec((1, tq, LANES), lambda bi, h, i: (bi, i, h)),
        out_shape=jax.ShapeDtypeStruct((b, s, DA_HEADS * DA_V_DIM), BF16),
        scratch_shapes=[pltpu.VMEM((2, tq, 1), F32), pltpu.VMEM((2, tq, 1), F32),
                        pltpu.VMEM((2, tq, DA_V_DIM), F32)],
        compiler_params=_cparams("parallel", "parallel", "arbitrary"),
        name="diff_attn",
    )(u3, u3, u3, nb, diff_lambda, subln_g)


def _mlstm_kernel(q_ref, k_ref, v_ref, og_ref, g_ref, cwq_ref, cwk_ref, ng_ref, y_ref,
                  ct_ref, gts_ref, *, chunk, seq):
    h = pl.program_id(1)
    nc = seq // chunk
    hd = ML_HEAD_DIM
    lane = lax.broadcasted_iota(jnp.int32, (chunk, LANES), 1)
    row = lax.broadcasted_iota(jnp.int32, (chunk, chunk), 0)
    col = lax.broadcasted_iota(jnp.int32, (chunk, chunk), 1)
    causal = row >= col
    tri = causal.astype(F32)
    cwq = cwq_ref[...]
    cwk = cwk_ref[...]
    ng = ng_ref[...]
    ct_ref[...] = jnp.zeros(ct_ref.shape, F32)

    def conv_silu(ref, cw, r0, c):
        cur = ref[0, pl.ds(r0, chunk), :].astype(F32)
        p0 = pl.multiple_of(jnp.maximum(r0 - 16, 0), 16)
        prev = ref[0, pl.ds(p0, 16), :].astype(F32)
        prev = jnp.where(c > 0, prev, 0.0)
        x = jnp.concatenate([prev, cur], axis=0)
        out = cw[ML_CONV - 1:ML_CONV] * cur
        for tap in range(ML_CONV - 1):
            shifted = pltpu.roll(x, ML_CONV - 1 - tap, 0)[16:]
            out = out + cw[tap:tap + 1] * shifted
        return out * jax.nn.sigmoid(out)

    def body(c, carry):
        m, n = carry
        r0 = pl.multiple_of(c * chunk, chunk)
        g = g_ref[0, pl.ds(r0, chunk), :]
        logf = jnp.minimum(g, 0.0) - jnp.log(1.0 + jnp.exp(-jnp.abs(g)))
        bc = jnp.dot(tri, logf, precision=lax.Precision.HIGHEST, preferred_element_type=F32)
        gts_ref[...] = jnp.where(lane < ML_HEADS, g, bc).T
        irow = gts_ref[pl.ds(h, 1), :]
        brow = gts_ref[pl.ds(ML_HEADS + h, 1), :]
        bcol = jnp.sum(jnp.where(lane == ML_HEADS + h, bc, 0.0), axis=1, keepdims=True)
        icol = jnp.sum(jnp.where(lane == h, g, 0.0), axis=1, keepdims=True)

        dmat = jnp.where(causal, bcol - brow + irow, NEG)
        inter = bcol + m
        m_row = jnp.maximum(inter, jnp.max(dmat, axis=1, keepdims=True))

        q = conv_silu(q_ref, cwq, r0, c)
        k = conv_silu(k_ref, cwk, r0, c) * (hd ** -0.5)
        qb = q.astype(BF16)
        kb = k.astype(BF16)
        vb = v_ref[0, pl.ds(r0, chunk), :]
        sqk = lax.dot_general(qb, kb, (((1,), (1,)), ((), ())), preferred_element_type=F32)
        w = jnp.exp(dmat - m_row) * sqk
        inter_w = jnp.exp(inter - m_row)
        ct = ct_ref[...]
        num = (inter_w * jnp.dot(qb, ct.astype(BF16), preferred_element_type=F32)
               + jnp.dot(w.astype(BF16), vb, preferred_element_type=F32))
        den = inter_w * jnp.sum(q * n, axis=1, keepdims=True) + jnp.sum(w, axis=1, keepdims=True)
        hout = num / jnp.maximum(jnp.abs(den), jnp.exp(-m_row))
        mu = jnp.mean(hout, axis=1, keepdims=True)
        hc = hout - mu
        var = jnp.mean(hc * hc, axis=1, keepdims=True)
        hn = hc * lax.rsqrt(var + 1e-5) * ng
        og = og_ref[0, pl.ds(r0, chunk), :].astype(F32)
        y_ref[0, pl.ds(r0, chunk), :] = (jax.nn.sigmoid(og) * hn).astype(y_ref.dtype)

        total = brow[:, chunk - 1:chunk]
        grow = total - brow + irow
        m_new = jnp.maximum(total + m, jnp.max(grow, axis=1, keepdims=True))
        decay = jnp.exp(total + m - m_new)
        ws = jnp.exp(total - bcol + icol - m_new)
        wsv = (ws * vb.astype(F32)).astype(BF16)
        ct_ref[...] = decay * ct + jnp.dot(k.T.astype(BF16), wsv, preferred_element_type=F32)
        n_new = decay * n + jnp.sum(ws * k, axis=0, keepdims=True)
        return m_new, n_new

    lax.fori_loop(0, nc, body, (jnp.zeros((1, 1), F32), jnp.zeros((1, hd), F32)))


def _mlstm(u3, gates3, conv_w, norm_g, chunk):
    b, s, _ = u3.shape
    hd = ML_HEAD_DIM
    nq = hd // LANES
    kern = functools.partial(_mlstm_kernel, chunk=chunk, seq=s)

    def ublock(off):
        return pl.BlockSpec((1, s, hd), lambda bi, h: (bi, 0, off // nq + h))

    return pl.pallas_call(
        kern,
        grid=(b, ML_HEADS),
        in_specs=[
            ublock(OFF_MLQ), ublock(OFF_MLK), ublock(OFF_MLV), ublock(OFF_MLO),
            pl.BlockSpec((1, s, LANES), lambda bi, h: (bi, 0, 0)),
            pl.BlockSpec((ML_CONV, hd), lambda bi, h: (0, h)),
            pl.BlockSpec((ML_CONV, hd), lambda bi, h: (0, ML_HEADS + h)),
            pl.BlockSpec((1, hd), lambda bi, h: (0, h)),
        ],
        out_specs=pl.BlockSpec((1, s, hd), lambda bi, h: (bi, 0, h)),
        out_shape=jax.ShapeDtypeStruct((b, s, ML_HEADS * hd), BF16),
        scratch_shapes=[pltpu.VMEM((hd, hd), F32), pltpu.VMEM((LANES, chunk), F32)],
        compiler_params=_cparams("parallel", "arbitrary"),
        name="mlstm",
    )(u3, u3, u3, u3, gates3, conv_w, conv_w, norm_g)


def _memattn_kernel(q_ref, k_ref, v_ref, o_ref, *, tq, seq):
    kb = k_ref[0]
    vb = v_ref[0]
    scale = jnp.asarray(MA_HEAD_DIM ** -0.5, BF16)
    for t in range(seq // tq):
        q = q_ref[0, t * tq:(t + 1) * tq, :] * scale
        s = lax.dot_general(q, kb, (((1,), (1,)), ((), ())), preferred_element_type=F32)
        p = jnp.exp(s - jnp.max(s, axis=1, keepdims=True))
        inv = 1.0 / jnp.sum(p, axis=1, keepdims=True)
        o = jnp.dot(p.astype(BF16), vb, preferred_element_type=F32) * inv
        o_ref[0, t * tq:(t + 1) * tq, :] = o.astype(o_ref.dtype)


def _mem_attention(u3, kv3, tq):
    b, s, _ = u3.shape
    mlen = kv3.shape[1]
    hd = MA_HEAD_DIM
    nq = hd // LANES
    kern = functools.partial(_memattn_kernel, tq=tq, seq=s)
    return pl.pallas_call(
        kern,
        grid=(b, MA_HEADS),
        in_specs=[
            pl.BlockSpec((1, s, hd), lambda bi, h: (bi, 0, OFF_MAQ // nq + h)),
            pl.BlockSpec((1, mlen, hd), lambda bi, h: (bi, 0, h)),
            pl.BlockSpec((1, mlen, hd), lambda bi, h: (bi, 0, MA_HEADS + h)),
        ],
        out_specs=pl.BlockSpec((1, s, hd), lambda bi, h: (bi, 0, h)),
        out_shape=jax.ShapeDtypeStruct((b, s, MA_HEADS * hd), BF16),
        compiler_params=_cparams("parallel", "parallel"),
        name="mem_attn",
    )(u3, kv3, kv3)


def _merge_kernel(ya_ref, ym_ref, yc_ref, g0_ref, g1_ref, g2_ref, x_ref, wb_ref, wo_ref,
                  lg_ref, lb_ref, x1_ref, x1b_ref, x1p_ref):
    acc = None
    for n, (y_ref, g_ref) in enumerate(((ya_ref, g0_ref), (ym_ref, g1_ref), (yc_ref, g2_ref))):
        pr = jnp.dot(y_ref[...], wb_ref[n], preferred_element_type=F32)
        t = jax.nn.sigmoid(g_ref[...].astype(F32)) * pr
        acc = t if acc is None else acc + t
    out = jnp.dot(acc.astype(BF16), wo_ref[...], preferred_element_type=F32)
    x1 = _layer_norm(ALPHA * x_ref[...] + out, lg_ref[...], lb_ref[...])
    x1_ref[...] = x1
    x1b = x1.astype(BF16)
    x1b_ref[...] = x1b
    x1p_ref[...] = _pack_halves(x1b)


def _pack_halves(vb):
    w = vb.shape[1] // 2
    hi = lax.bitcast_convert_type(vb[:, :w].astype(F32), U32)
    lo = lax.bitcast_convert_type(vb[:, w:].astype(F32), U32)
    return hi | (lo >> 16)


def _unpack_halves(u):
    hi = lax.bitcast_convert_type(u & jnp.uint32(0xFFFF0000), F32)
    lo = lax.bitcast_convert_type(u << 16, F32)
    return hi, lo


def _merge(ya, ym, yc, u2, x2, wb, wo, lg, lb, tm):
    t, d = x2.shape
    gb = OFF_GATE * LANES // d

    def rows(i):
        return (i, 0)

    return pl.pallas_call(
        _merge_kernel,
        grid=(t // tm,),
        in_specs=[
            pl.BlockSpec((tm, d), rows), pl.BlockSpec((tm, d), rows), pl.BlockSpec((tm, d), rows),
            pl.BlockSpec((tm, d), lambda i: (i, gb)),
            pl.BlockSpec((tm, d), lambda i: (i, gb + 1)),
            pl.BlockSpec((tm, d), lambda i: (i, gb + 2)),
            pl.BlockSpec((tm, d), rows),
            pl.BlockSpec((N_BRANCH, d, d), lambda i: (0, 0, 0)),
            pl.BlockSpec((d, d), lambda i: (0, 0)),
            pl.BlockSpec((1, d), lambda i: (0, 0)),
            pl.BlockSpec((1, d), lambda i: (0, 0)),
        ],
        out_specs=[pl.BlockSpec((tm, d), rows), pl.BlockSpec((tm, d), rows),
                   pl.BlockSpec((tm, d // 2), rows)],
        out_shape=[jax.ShapeDtypeStruct((t, d), F32), jax.ShapeDtypeStruct((t, d), BF16),
                   jax.ShapeDtypeStruct((t, d // 2), U32)],
        compiler_params=_cparams("parallel"),
        name="merge_ln1",
    )(ya, ym, yc, u2, u2, u2, x2, wb, wo, lg, lb)


def _expert_kernel(ib_ref, ie_ref, lo_ref, hi_ref, first_ref, nit_ref, x_ref, wg_ref, wu_ref,
                   wd_ref, o_ref):
    i = pl.program_id(0)

    @pl.when(i < nit_ref[0])
    def _():
        xl, xr = _unpack_halves(x_ref[...])
        xl = xl.astype(BF16)
        xr = xr.astype(BF16)
        half = xl.shape[1]

        def up(w_ref):
            w = w_ref[0]
            return (jnp.dot(xl, w[:half].astype(BF16), preferred_element_type=F32)
                    + jnp.dot(xr, w[half:].astype(BF16), preferred_element_type=F32))

        hg = up(wg_ref)
        hu = up(wu_ref)
        act = (hg * jax.nn.sigmoid(hg) * hu).astype(BF16)
        y = _pack_halves(jnp.dot(act, wd_ref[0].astype(BF16),
                                 preferred_element_type=F32).astype(BF16))
        row = lax.broadcasted_iota(I32, y.shape, 0)
        mine = (row >= lo_ref[i]) & (row < hi_ref[i])

        @pl.when(first_ref[i] == 1)
        def _():
            o_ref[...] = jnp.where(mine, y, jnp.uint32(0))

        @pl.when(first_ref[i] == 0)
        def _():
            o_ref[...] = jnp.where(mine, y, o_ref[...])


def _work_items(counts, n_pairs, bm):
    assert n_pairs % bm == 0
    nblocks = n_pairs // bm
    ends = jnp.cumsum(counts)
    starts = ends - counts
    first_blk = starts // bm
    n_e = jnp.where(counts > 0, (ends - 1) // bm - first_blk + 1, 0)
    item_end = jnp.cumsum(n_e)
    item_start = item_end - n_e
    n_items = item_end[-1]
    i = jnp.arange(nblocks + N_EXPERTS)
    valid = i < n_items
    e = jnp.minimum(jnp.searchsorted(item_end, jnp.minimum(i, n_items - 1), side='right'),
                    N_EXPERTS - 1)
    blk = jnp.where(valid, first_blk[e] + i - item_start[e], nblocks - 1)
    lo = jnp.clip(starts[e] - blk * bm, 0, bm)
    hi = jnp.where(valid, jnp.clip(ends[e] - blk * bm, 0, bm), 0)
    first = jnp.concatenate([jnp.ones((1,), I32), (blk[1:] != blk[:-1]).astype(I32)])
    items = tuple(a.astype(I32) for a in (blk, e, lo, hi, first, n_items[None]))
    return items, starts.astype(I32)


def _experts(items, xs, w_gate, w_up, w_down, bm):
    n, dh = xs.shape
    d = 2 * dh
    de = w_gate.shape[2]

    def rows(i, ib, ie, lo, hi, first, nit):
        return (ib[i], 0)

    def expert(i, ib, ie, lo, hi, first, nit):
        return (ie[i], 0, 0)

    grid_spec = pltpu.PrefetchScalarGridSpec(
        num_scalar_prefetch=6,
        grid=(items[0].shape[0],),
        in_specs=[
            pl.BlockSpec((bm, dh), rows),
            pl.BlockSpec((1, d, de), expert),
            pl.BlockSpec((1, d, de), expert),
            pl.BlockSpec((1, de, d), expert),
        ],
        out_specs=pl.BlockSpec((bm, dh), rows),
    )
    return pl.pallas_call(
        _expert_kernel,
        grid_spec=grid_spec,
        out_shape=jax.ShapeDtypeStruct((n, dh), U32),
        compiler_params=_cparams("arbitrary"),
        name="experts",
    )(*items, xs, w_gate, w_up, w_down)


def _fetch_tile_indices(dest_hbm, idx_smem, isem):
    i = pl.program_id(0)
    slot = lax.rem(i, 2)

    def idx_copy(step, sl):
        return pltpu.make_async_copy(dest_hbm.at[step], idx_smem.at[sl], isem.at[sl])

    @pl.when(i == 0)
    def _():
        idx_copy(0, 0).start()

    idx_copy(i, slot).wait()

    @pl.when(i + 1 < pl.num_programs(0))
    def _():
        idx_copy(i + 1, 1 - slot).start()

    return slot


def _ffn_out_kernel(dest_hbm, ys_hbm, xb_ref, x1_ref, wt_ref, wg_ref, wu_ref, wd_ref, lg_ref,
                    lb_ref, o_ref, idx_smem, ybuf, isem, rsem, *, tm):
    slot = _fetch_tile_indices(dest_hbm, idx_smem, isem)

    def issue(t, c):
        for k in range(TOP_K):
            r = idx_smem[slot, k, t]
            pltpu.make_async_copy(ys_hbm.at[pl.ds(r, 1)], ybuf.at[k, pl.ds(t, 1)], rsem).start()
        return c

    lax.fori_loop(0, tm, issue, 0)

    xb = xb_ref[...]
    hg = jnp.dot(xb, wg_ref[...], preferred_element_type=F32)
    hu = jnp.dot(xb, wu_ref[...], preferred_element_type=F32)
    act = (hg * jax.nn.sigmoid(hg) * hu).astype(BF16)
    sh = jnp.dot(act, wd_ref[...], preferred_element_type=F32)

    pltpu.make_async_copy(ybuf, ybuf, rsem).wait()
    wt = wt_ref[...]
    rl = None
    rr = None
    for k in range(TOP_K):
        hi, lo = _unpack_halves(ybuf[k])
        wk = wt[:, k:k + 1]
        rl = wk * hi if rl is None else rl + wk * hi
        rr = wk * lo if rr is None else rr + wk * lo
    z = ALPHA * x1_ref[...] + sh + jnp.concatenate([rl, rr], axis=1)
    o_ref[...] = _layer_norm(z, lg_ref[...], lb_ref[...])


def _ffn_out(dest, ys, x1b, x1, wt, wg, wu, wd, lg, lb, tm):
    t, d = x1.shape
    ds = wg.shape[1]
    dh = ys.shape[1]
    assert dest.shape == (t // tm, TOP_K, tm)

    def rows(i):
        return (i, 0)

    def whole(i):
        return (0, 0)

    return pl.pallas_call(
        functools.partial(_ffn_out_kernel, tm=tm),
        grid=(t // tm,),
        in_specs=[
            pl.BlockSpec(memory_space=pl.ANY), pl.BlockSpec(memory_space=pl.ANY),
            pl.BlockSpec((tm, d), rows), pl.BlockSpec((tm, d), rows), pl.BlockSpec((tm, LANES), rows),
            pl.BlockSpec((d, ds), whole), pl.BlockSpec((d, ds), whole), pl.BlockSpec((ds, d), whole),
            pl.BlockSpec((1, d), whole), pl.BlockSpec((1, d), whole),
        ],
        out_specs=pl.BlockSpec((tm, d), rows),
        out_shape=jax.ShapeDtypeStruct((t, d), F32),
        scratch_shapes=[pltpu.SMEM((2, TOP_K, tm), I32), pltpu.VMEM((TOP_K, tm, dh), U32),
                        pltpu.SemaphoreType.DMA((2,)), pltpu.SemaphoreType.DMA(())],
        compiler_params=_cparams("arbitrary"),
        name="ffn_out_ln2",
    )(dest, ys, x1b, x1, wt, wg, wu, wd, lg, lb)


def _dispatch_kernel(dest_hbm, xp_ref, xs_hbm, idx_smem, isem, rsem, *, tm):
    slot = _fetch_tile_indices(dest_hbm, idx_smem, isem)

    def issue(t, c):
        for k in range(TOP_K):
            r = idx_smem[slot, k, t]
            pltpu.make_async_copy(xp_ref.at[pl.ds(t, 1)], xs_hbm.at[pl.ds(r, 1)], rsem).start()
        return c

    lax.fori_loop(0, tm, issue, 0)
    done = xs_hbm.at[pl.ds(0, TOP_K * tm)]
    pltpu.make_async_copy(done, done, rsem).wait()


def _dispatch(dest, x1p, npad, tm):
    t, dh = x1p.shape
    assert npad >= TOP_K * tm
    return pl.pallas_call(
        functools.partial(_dispatch_kernel, tm=tm),
        grid=(t // tm,),
        in_specs=[pl.BlockSpec(memory_space=pl.ANY), pl.BlockSpec((tm, dh), lambda i: (i, 0))],
        out_specs=pl.BlockSpec(memory_space=pl.ANY),
        out_shape=jax.ShapeDtypeStruct((npad, dh), U32),
        scratch_shapes=[pltpu.SMEM((2, TOP_K, tm), I32), pltpu.SemaphoreType.DMA((2,)),
                        pltpu.SemaphoreType.DMA(())],
        compiler_params=_cparams("arbitrary"),
        name="dispatch",
    )(dest, x1p)


def _route_kernel(xb_ref, wrt_ref, rb_ref, ek_ref, rk_ref, wt_ref, cnt_ref, upper_ref, run_ref, *,
                  tm):
    i = pl.program_id(0)
    gsz = N_EXPERTS // N_GROUP
    ninf = -jnp.inf

    @pl.when(i == 0)
    def _():
        r = lax.broadcasted_iota(I32, (tm, tm), 0)
        c = lax.broadcasted_iota(I32, (tm, tm), 1)
        upper_ref[...] = jnp.where(r < c, 1.0, 0.0).astype(BF16)
        run_ref[...] = jnp.zeros(run_ref.shape, F32)

    logits = lax.dot_general(wrt_ref[...], xb_ref[...], (((1,), (1,)), ((), ())),
                             preferred_element_type=F32)
    scores = jax.nn.sigmoid(logits)
    choice = scores + rb_ref[...]

    ridx = lax.broadcasted_iota(I32, (gsz, tm), 0)
    gscore = []
    for g in range(N_GROUP):
        blk = choice[g * gsz:(g + 1) * gsz, :]
        m1 = jnp.max(blk, axis=0, keepdims=True)
        i1 = jnp.min(jnp.where(blk == m1, ridx, gsz), axis=0, keepdims=True)
        m2 = jnp.max(jnp.where(ridx == i1, ninf, blk), axis=0, keepdims=True)
        gscore.append(m1 + m2)
    masked = []
    for g in range(N_GROUP):
        beaten = jnp.zeros((1, tm), I32)
        for g2 in range(N_GROUP):
            if g2 == g:
                continue
            wins = (gscore[g2] >= gscore[g]) if g2 < g else (gscore[g2] > gscore[g])
            beaten = beaten + jnp.where(wins, 1, 0)
        masked.append(jnp.where(beaten < TOPK_GROUP, choice[g * gsz:(g + 1) * gsz, :], ninf))
    v = jnp.concatenate(masked, axis=0)

    eidx = lax.broadcasted_iota(I32, (N_EXPERTS, tm), 0)
    sel = jnp.zeros((N_EXPERTS, tm), F32)
    e_rows = []
    s_rows = []
    for k in range(TOP_K):
        m = jnp.max(v, axis=0, keepdims=True)
        ik = jnp.min(jnp.where(v == m, eidx, N_EXPERTS), axis=0, keepdims=True)
        hit = eidx == ik
        e_rows.append(ik)
        s_rows.append(jnp.sum(jnp.where(hit, scores, 0.0), axis=0, keepdims=True))
        v = jnp.where(hit, ninf, v)
        sel = jnp.where(hit, 1.0, sel)

    prefix = jnp.dot(sel.astype(BF16), upper_ref[...], preferred_element_type=F32)
    pos = prefix + run_ref[...]
    for k in range(TOP_K):
        rk = jnp.sum(jnp.where(eidx == e_rows[k], pos, 0.0), axis=0, keepdims=True)
        ek_ref[k:k + 1, :] = e_rows[k]
        rk_ref[k:k + 1, :] = rk.astype(I32)
    run_ref[...] = run_ref[...] + jnp.sum(sel, axis=1, keepdims=True)
    cnt_ref[...] = jnp.broadcast_to(run_ref[...], cnt_ref.shape).astype(I32)

    ssum = s_rows[0]
    for k in range(1, TOP_K):
        ssum = ssum + s_rows[k]
    w_rows = [s / (ssum + 1e-20) * ROUTED_SCALE for s in s_rows]
    w_rows.append(jnp.zeros((LANES - TOP_K, tm), F32))
    wt_ref[...] = jnp.concatenate(w_rows, axis=0).T


def _route(x1b, wrt, rbias, tm):
    t, d = x1b.shape
    return pl.pallas_call(
        functools.partial(_route_kernel, tm=tm),
        grid=(t // tm,),
        in_specs=[pl.BlockSpec((tm, d), lambda i: (i, 0)),
                  pl.BlockSpec((N_EXPERTS, d), lambda i: (0, 0)),
                  pl.BlockSpec((N_EXPERTS, 1), lambda i: (0, 0))],
        out_specs=[pl.BlockSpec((TOP_K, tm), lambda i: (0, i)),
                   pl.BlockSpec((TOP_K, tm), lambda i: (0, i)),
                   pl.BlockSpec((tm, LANES), lambda i: (i, 0)),
                   pl.BlockSpec((N_EXPERTS, LANES), lambda i: (0, 0))],
        out_shape=[jax.ShapeDtypeStruct((TOP_K, t), I32), jax.ShapeDtypeStruct((TOP_K, t), I32),
                   jax.ShapeDtypeStruct((t, LANES), F32),
                   jax.ShapeDtypeStruct((N_EXPERTS, LANES), I32)],
        scratch_shapes=[pltpu.VMEM((tm, tm), BF16), pltpu.VMEM((N_EXPERTS, 1), F32)],
        compiler_params=_cparams("arbitrary"),
        name="route",
    )(x1b, wrt, rbias)


def _dest_kernel(ps_ref, ek_ref, rk_ref, d_ref):
    e = ek_ref[...]

    def body(j, acc):
        return acc + jnp.where(e == j, ps_ref[j], 0)

    base = lax.fori_loop(0, N_EXPERTS, body, jnp.zeros(e.shape, I32))
    d_ref[0] = base + rk_ref[...]


def _dest(pstart, ek, rk, tm):
    t = ek.shape[1]
    grid_spec = pltpu.PrefetchScalarGridSpec(
        num_scalar_prefetch=1,
        grid=(t // tm,),
        in_specs=[pl.BlockSpec((TOP_K, tm), lambda i, ps: (0, i)),
                  pl.BlockSpec((TOP_K, tm), lambda i, ps: (0, i))],
        out_specs=pl.BlockSpec((1, TOP_K, tm), lambda i, ps: (i, 0, 0)),
    )
    return pl.pallas_call(
        _dest_kernel,
        grid_spec=grid_spec,
        out_shape=jax.ShapeDtypeStruct((t // tm, TOP_K, tm), I32),
        compiler_params=_cparams("parallel"),
        name="dest",
    )(pstart, ek, rk)


def _layer(x, mem, rel_bias, w_in, b_in, conv_w, diff_lambda, subln_g, mlstm_norm_g, w_mem_kv,
           w_branch, w_out, ln1_g, ln1_b, w_router, router_bias, w_e_gate, w_e_up, w_e_down,
           w_s_gate, w_s_up, w_s_down, ln2_g, ln2_b, layer_idx, cfg):
    b, s, d = x.shape
    t = b * s
    x2 = x.reshape(t, d)

    g0 = (OFF_MLO + 8) * LANES
    w_main = jnp.concatenate([w_in[:, :g0], w_in[:, g0 + 2 * ML_HEADS:]], axis=1).astype(BF16)
    b_main = jnp.concatenate([b_in[:g0], b_in[g0 + 2 * ML_HEADS:]])[None, :]
    w_g = jnp.pad(w_in[:, g0:g0 + 2 * ML_HEADS], ((0, 0), (0, LANES - 2 * ML_HEADS))).astype(BF16)
    b_g = jnp.pad(b_in[g0:g0 + 2 * ML_HEADS], (0, LANES - 2 * ML_HEADS))[None, :]

    u2, gates2 = _proj_in(x2, w_main, b_main, w_g, b_g, cfg["proj_tm"], cfg["proj_tn"])
    u3 = u2.reshape(b, s, N_MAIN)
    gates3 = gates2.reshape(b, s, LANES)

    lam_init = 0.8 - 0.6 * math.exp(-0.3 * layer_idx)
    nb = _near_bias(rel_bias, cfg["attn_tq"])
    y_a = _diff_attention(u3, nb, diff_lambda, subln_g[None, :], cfg["attn_tq"], lam_init)
    y_m = _mlstm(u3, gates3, conv_w, mlstm_norm_g[None, :], cfg["ml_chunk"])
    kv = _mm(mem.reshape(-1, d), w_mem_kv.astype(BF16), BF16, cfg["kv_tm"], "mem_kv")
    y_c = _mem_attention(u3, kv.reshape(b, -1, 2 * MA_HEADS * MA_HEAD_DIM), cfg["ma_tq"])

    x1, x1b, x1p = _merge(y_a.reshape(t, d), y_m.reshape(t, d), y_c.reshape(t, d), u2, x2,
                          w_branch.astype(BF16), w_out.astype(BF16), ln1_g[None, :],
                          ln1_b[None, :], cfg["merge_tm"])

    ek, rk, wt, cnt = _route(x1b, w_router.T.astype(BF16), router_bias.astype(F32)[:, None],
                             cfg["route_tm"])
    bm = cfg["expert_bm"]
    items, starts = _work_items(cnt[:, 0], t * TOP_K, bm)
    dest = _dest(starts, ek, rk, cfg["moe_tm"])
    xs = _dispatch(dest, x1p, t * TOP_K, cfg["moe_tm"])
    ys = _experts(items, xs, w_e_gate, w_e_up, w_e_down, bm)
    out = _ffn_out(dest, ys, x1b, x1, wt, w_s_gate.astype(BF16), w_s_up.astype(BF16),
                   w_s_down.astype(BF16), ln2_g[None, :], ln2_b[None, :], cfg["moe_tm"])
    return out.reshape(b, s, d)


def _config(b, s):
    t = b * s
    return {
        "proj_tm": min(1024, t), "proj_tn": 1024,
        "attn_tq": 256, "ml_chunk": 256, "kv_tm": 512, "ma_tq": min(512, s),
        "merge_tm": 256, "route_tm": 512, "expert_bm": 512, "moe_tm": 256,
    }


def kernel(x, mem, rel_bias, w_in, b_in, conv_w, diff_lambda, subln_g, mlstm_norm_g, w_mem_kv,
           w_branch, w_out, ln1_g, ln1_b, w_router, router_bias, w_e_gate, w_e_up, w_e_down,
           w_s_gate, w_s_up, w_s_down, ln2_g, ln2_b):
    cfg = _config(x.shape[0], x.shape[1])
    for l in range(DEPTH):
        x = _layer(x, mem, rel_bias, w_in[l], b_in[l], conv_w[l], diff_lambda[l], subln_g[l],
                   mlstm_norm_g[l], w_mem_kv[l], w_branch[l], w_out[l], ln1_g[l], ln1_b[l],
                   w_router[l], router_bias[l], w_e_gate[l], w_e_up[l], w_e_down[l],
                   w_s_gate[l], w_s_up[l], w_s_down[l], ln2_g[l], ln2_b[l], l, cfg)
    return x
```

```python
import functools
import math

import numpy as np
import jax
import jax.numpy as jnp
from jax import lax
from jax.experimental import pallas as pl
from jax.experimental.pallas import tpu as pltpu

F32 = jnp.float32
BF16 = jnp.bfloat16
U32 = jnp.uint32
I32 = jnp.int32

D_MODEL = 1024
DEPTH = 1
DA_HEAD_DIM = 64
DA_V_DIM = 128
DA_HEADS = 8
ML_HEADS = 4
ML_HEAD_DIM = 256
ML_CONV = 4
MA_HEADS = 4
MA_HEAD_DIM = 256
N_BRANCH = 3
REL_BUCKETS = 32
REL_MAX_DIST = 128
N_EXPERTS = 256
TOP_K = 8
N_GROUP = 8
TOPK_GROUP = 4
D_EXPERT = 256
ROUTED_SCALE = 2.5
ALPHA = (2.0 * DEPTH) ** 0.25

LANES = 128
NEG = -1e30
LOG2E = math.log2(math.e)
VMEM_LIMIT = 56 * 1024 * 1024

OFF_DAQ, OFF_DAK, OFF_DAV = 0, 8, 16
OFF_MLQ, OFF_MLK, OFF_MLV, OFF_MLO, OFF_MAQ, OFF_GATE = 24, 32, 40, 48, 56, 64
N_MAIN = 88 * LANES


def _cparams(*sem):
    return pltpu.CompilerParams(dimension_semantics=sem, vmem_limit_bytes=VMEM_LIMIT)


def _layer_norm(z, g, b):
    mu = jnp.mean(z, axis=-1, keepdims=True)
    zc = z - mu
    var = jnp.mean(zc * zc, axis=-1, keepdims=True)
    return zc * lax.rsqrt(var + 1e-5) * g + b


def _proj_in_kernel(x_ref, w_ref, b_ref, wg_ref, bg_ref, u_ref, g_ref, xs_ref):
    @pl.when(pl.program_id(1) == 0)
    def _():
        xb = x_ref[...].astype(BF16)
        xs_ref[...] = xb
        g_ref[...] = jnp.dot(xb, wg_ref[...], preferred_element_type=F32) + bg_ref[...]

    acc = jnp.dot(xs_ref[...], w_ref[...], preferred_element_type=F32)
    u_ref[...] = (acc + b_ref[...]).astype(u_ref.dtype)


def _proj_in(x2, w_main, b_main, w_g, b_g, tm, tn):
    t, k = x2.shape
    n = w_main.shape[1]
    return pl.pallas_call(
        _proj_in_kernel,
        grid=(t // tm, n // tn),
        in_specs=[
            pl.BlockSpec((tm, k), lambda i, j: (i, 0)),
            pl.BlockSpec((k, tn), lambda i, j: (0, j)),
            pl.BlockSpec((1, tn), lambda i, j: (0, j)),
            pl.BlockSpec((k, LANES), lambda i, j: (0, 0)),
            pl.BlockSpec((1, LANES), lambda i, j: (0, 0)),
        ],
        out_specs=[
            pl.BlockSpec((tm, tn), lambda i, j: (i, j)),
            pl.BlockSpec((tm, LANES), lambda i, j: (i, 0)),
        ],
        out_shape=[jax.ShapeDtypeStruct((t, n), BF16), jax.ShapeDtypeStruct((t, LANES), F32)],
        scratch_shapes=[pltpu.VMEM((tm, k), BF16)],
        compiler_params=_cparams("parallel", "arbitrary"),
        name="proj_in",
    )(x2, w_main, b_main, w_g, b_g)


def _mm_kernel(x_ref, w_ref, o_ref):
    o_ref[...] = jnp.dot(x_ref[...].astype(BF16), w_ref[...],
                         preferred_element_type=F32).astype(o_ref.dtype)


def _mm(x2, w, out_dtype, tm, name):
    m, k = x2.shape
    n = w.shape[1]
    return pl.pallas_call(
        _mm_kernel,
        grid=(m // tm,),
        in_specs=[pl.BlockSpec((tm, k), lambda i: (i, 0)), pl.BlockSpec((k, n), lambda i: (0, 0))],
        out_specs=pl.BlockSpec((tm, n), lambda i: (i, 0)),
        out_shape=jax.ShapeDtypeStruct((m, n), out_dtype),
        compiler_params=_cparams("parallel"),
        name=name,
    )(x2, w)


def _fold8(x, op):
    parts = [x[8 * a:8 * a + 8, :] for a in range(x.shape[0] // 8)]
    while len(parts) > 1:
        pairs = [op(parts[a], parts[a + 1]) for a in range(0, len(parts) - 1, 2)]
        parts = pairs + parts[len(parts) - len(parts) % 2:]
    return parts[0]


def _dattn_kernel(rb_ref, q_ref, k_ref, v_ref, bkt_ref, dl_ref, g_ref, o_ref,
                  st_ref, vt_ref, nb_ref, *, tq, seq, lam_init):
    h = pl.program_id(0)
    nq = seq // tq
    nt = (((1,), (1,)), ((), ()))

    @pl.when(pl.program_id(1) == 0)
    def _():
        far = rb_ref[REL_BUCKETS - 1, h]
        for t in range(2):
            bk = bkt_ref[t]
            tile = jnp.full((tq, tq), NEG, F32)
            for bb in range(REL_BUCKETS):
                tile = jnp.where(bk == bb, (rb_ref[bb, h] - far) * LOG2E, tile)
            nb_ref[t * tq:(t + 1) * tq, :] = tile

    for j in range(nq):
        vt_ref[:, j * tq:(j + 1) * tq] = v_ref[0, j * tq:(j + 1) * tq, :].astype(F32).T.astype(BF16)

    dl = dl_ref[...]
    lam = (jnp.exp(jnp.sum(dl[0:1] * dl[1:2], axis=1, keepdims=True))
           - jnp.exp(jnp.sum(dl[2:3] * dl[3:4], axis=1, keepdims=True)) + lam_init)
    lane = lax.broadcasted_iota(I32, (tq, LANES), 1)
    gain = g_ref[...] * (1.0 - lam_init)

    for i in range(nq):
        qs = q_ref[0, i * tq:(i + 1) * tq, :] * jnp.asarray(DA_HEAD_DIM ** -0.5 * LOG2E, BF16)
        zero = jnp.zeros_like(qs)
        qm = (jnp.where(lane < DA_HEAD_DIM, qs, zero), jnp.where(lane >= DA_HEAD_DIM, qs, zero))
        kv = (i + 1) * tq
        far = max(i - 1, 0) * tq
        bias = nb_ref[...] if i >= 1 else nb_ref[tq:, :]
        sb = 2 * (i % 2)
        l8 = []
        for half in range(2):
            st = st_ref.at[sb + half]
            s = lax.dot_general(k_ref[0, far:kv, :], qm[half], nt, preferred_element_type=F32) + bias
            st[far:kv, :] = s
            mx = _fold8(s, jnp.maximum)
            if far > 0:
                s = lax.dot_general(k_ref[0, 0:far, :], qm[half], nt, preferred_element_type=F32)
                st[0:far, :] = s
                mx = jnp.maximum(mx, _fold8(s, jnp.maximum))
            m = jnp.max(mx, axis=0, keepdims=True)
            p = jnp.exp2(st[0:kv, :] - m)
            st[0:kv, :] = p
            l8.append(_fold8(p, jnp.add))
        l1 = jnp.sum(l8[0], axis=0, keepdims=True)
        l2 = jnp.sum(l8[1], axis=0, keepdims=True)
        a = st_ref[sb, 0:kv, :] - st_ref[sb + 1, 0:kv, :] * (lam * l1 / l2)
        o_t = jnp.dot(vt_ref[:, 0:kv], a.astype(BF16), preferred_element_type=F32) * (1.0 / l1)
        o = o_t.T
        ms = jnp.mean(o * o, axis=1, keepdims=True)
        y = o * lax.rsqrt(ms + 1e-5) * gain
        o_ref[0, i * tq:(i + 1) * tq, :] = y.astype(o_ref.dtype)


def _t5_bucket_np(dist):
    n = np.maximum(dist, 0)
    max_exact = REL_BUCKETS // 2
    large = max_exact + (np.log(np.maximum(n, 1).astype(np.float32) / max_exact)
                         / math.log(REL_MAX_DIST / max_exact)
                         * (REL_BUCKETS - max_exact)).astype(np.int32)
    large = np.minimum(large, REL_BUCKETS - 1)
    return np.where(n < max_exact, n, large)


def _near_buckets(tq):
    assert _t5_bucket_np(np.array([tq + 1]))[0] == REL_BUCKETS - 1
    c = np.arange(tq)[:, None]
    r = np.arange(tq)[None, :]
    left = _t5_bucket_np(r + tq - c)
    diag = np.where(r >= c, _t5_bucket_np(r - c), -1)
    return np.stack([left, diag], axis=0).astype(np.int32)


def _diff_attention(u3, rel_bias, diff_lambda, subln_g, tq, lam_init):
    b, s, _ = u3.shape
    kern = functools.partial(_dattn_kernel, tq=tq, seq=s, lam_init=lam_init)
    grid_spec = pltpu.PrefetchScalarGridSpec(
        num_scalar_prefetch=1,
        grid=(DA_HEADS, b),
        in_specs=[
            pl.BlockSpec((1, s, LANES), lambda h, bi, rb: (bi, 0, OFF_DAQ + h)),
            pl.BlockSpec((1, s, LANES), lambda h, bi, rb: (bi, 0, OFF_DAK + h)),
            pl.BlockSpec((1, s, LANES), lambda h, bi, rb: (bi, 0, OFF_DAV + h)),
            pl.BlockSpec((2, tq, tq), lambda h, bi, rb: (0, 0, 0)),
            pl.BlockSpec((4, DA_HEAD_DIM), lambda h, bi, rb: (0, 0)),
            pl.BlockSpec((1, DA_V_DIM), lambda h, bi, rb: (0, 0)),
        ],
        out_specs=pl.BlockSpec((1, s, LANES), lambda h, bi, rb: (bi, 0, h)),
        scratch_shapes=[pltpu.VMEM((4, s, tq), F32), pltpu.VMEM((DA_V_DIM, s), BF16),
                        pltpu.VMEM((2 * tq, tq), F32)],
    )
    return pl.pallas_call(
        kern,
        grid_spec=grid_spec,
        out_shape=jax.ShapeDtypeStruct((b, s, DA_HEADS * DA_V_DIM), BF16),
        compiler_params=_cparams("arbitrary", "arbitrary"),
        name="diff_attn",
    )(rel_bias.astype(F32), u3, u3, u3, jnp.asarray(_near_buckets(tq)), diff_lambda, subln_g)


def _mlstm_kernel(q_ref, k_ref, v_ref, og_ref, g_ref, cwq_ref, cwk_ref, ng_ref, y_ref,
                  ct_ref, gts_ref, *, chunk, seq):
    h = pl.program_id(1)
    nc = seq // chunk
    hd = ML_HEAD_DIM
    lane = lax.broadcasted_iota(jnp.int32, (chunk, LANES), 1)
    row = lax.broadcasted_iota(jnp.int32, (chunk, chunk), 0)
    col = lax.broadcasted_iota(jnp.int32, (chunk, chunk), 1)
    causal = row >= col
    tri = causal.astype(F32)
    cwq = cwq_ref[...]
    cwk = cwk_ref[...]
    ng = ng_ref[...]
    ct_ref[...] = jnp.zeros(ct_ref.shape, F32)

    def conv_silu(ref, cw, r0, c):
        cur = ref[0, pl.ds(r0, chunk), :].astype(F32)
        p0 = pl.multiple_of(jnp.maximum(r0 - 16, 0), 16)
        prev = ref[0, pl.ds(p0, 16), :].astype(F32)
        prev = jnp.where(c > 0, prev, 0.0)
        x = jnp.concatenate([prev, cur], axis=0)
        out = cw[ML_CONV - 1:ML_CONV] * cur
        for tap in range(ML_CONV - 1):
            shifted = pltpu.roll(x, ML_CONV - 1 - tap, 0)[16:]
            out = out + cw[tap:tap + 1] * shifted
        return out * jax.nn.sigmoid(out)

    def body(c, carry):
        m, n = carry
        r0 = pl.multiple_of(c * chunk, chunk)
        g = g_ref[0, pl.ds(r0, chunk), :]
        logf = jnp.minimum(g, 0.0) - jnp.log(1.0 + jnp.exp(-jnp.abs(g)))
        bc = jnp.dot(tri, logf, precision=lax.Precision.HIGHEST, preferred_element_type=F32)
        gts_ref[...] = jnp.where(lane < ML_HEADS, g, bc).T
        irow = gts_ref[pl.ds(h, 1), :]
        brow = gts_ref[pl.ds(ML_HEADS + h, 1), :]
        bcol = jnp.sum(jnp.where(lane == ML_HEADS + h, bc, 0.0), axis=1, keepdims=True)
        icol = jnp.sum(jnp.where(lane == h, g, 0.0), axis=1, keepdims=True)

        dmat = jnp.where(causal, bcol - brow + irow, NEG)
        inter = bcol + m
        m_row = jnp.maximum(inter, jnp.max(dmat, axis=1, keepdims=True))

        q = conv_silu(q_ref, cwq, r0, c)
        k = conv_silu(k_ref, cwk, r0, c) * (hd ** -0.5)
        qb = q.astype(BF16)
        kb = k.astype(BF16)
        vb = v_ref[0, pl.ds(r0, chunk), :]
        sqk = lax.dot_general(qb, kb, (((1,), (1,)), ((), ())), preferred_element_type=F32)
        w = jnp.exp(dmat - m_row) * sqk
        inter_w = jnp.exp(inter - m_row)
        ct = ct_ref[...]
        num = (inter_w * jnp.dot(qb, ct.astype(BF16), preferred_element_type=F32)
               + jnp.dot(w.astype(BF16), vb, preferred_element_type=F32))
        den = inter_w * jnp.sum(q * n, axis=1, keepdims=True) + jnp.sum(w, axis=1, keepdims=True)
        hout = num / jnp.maximum(jnp.abs(den), jnp.exp(-m_row))
        mu = jnp.mean(hout, axis=1, keepdims=True)
        hc = hout - mu
        var = jnp.mean(hc * hc, axis=1, keepdims=True)
        hn = hc * lax.rsqrt(var + 1e-5) * ng
        og = og_ref[0, pl.ds(r0, chunk), :].astype(F32)
        y_ref[0, pl.ds(r0, chunk), :] = (jax.nn.sigmoid(og) * hn).astype(y_ref.dtype)

        total = brow[:, chunk - 1:chunk]
        grow = total - brow + irow
        m_new = jnp.maximum(total + m, jnp.max(grow, axis=1, keepdims=True))
        decay = jnp.exp(total + m - m_new)
        ws = jnp.exp(total - bcol + icol - m_new)
        wsv = (ws * vb.astype(F32)).astype(BF16)
        ct_ref[...] = decay * ct + jnp.dot(k.T.astype(BF16), wsv, preferred_element_type=F32)
        n_new = decay * n + jnp.sum(ws * k, axis=0, keepdims=True)
        return m_new, n_new

    lax.fori_loop(0, nc, body, (jnp.zeros((1, 1), F32), jnp.zeros((1, hd), F32)))


def _mlstm(u3, gates3, conv_w, norm_g, chunk):
    b, s, _ = u3.shape
    hd = ML_HEAD_DIM
    nq = hd // LANES
    kern = functools.partial(_mlstm_kernel, chunk=chunk, seq=s)

    def ublock(off):
        return pl.BlockSpec((1, s, hd), lambda bi, h: (bi, 0, off // nq + h))

    return pl.pallas_call(
        kern,
        grid=(b, ML_HEADS),
        in_specs=[
            ublock(OFF_MLQ), ublock(OFF_MLK), ublock(OFF_MLV), ublock(OFF_MLO),
            pl.BlockSpec((1, s, LANES), lambda bi, h: (bi, 0, 0)),
            pl.BlockSpec((ML_CONV, hd), lambda bi, h: (0, h)),
            pl.BlockSpec((ML_CONV, hd), lambda bi, h: (0, ML_HEADS + h)),
            pl.BlockSpec((1, hd), lambda bi, h: (0, h)),
        ],
        out_specs=pl.BlockSpec((1, s, hd), lambda bi, h: (bi, 0, h)),
        out_shape=jax.ShapeDtypeStruct((b, s, ML_HEADS * hd), BF16),
        scratch_shapes=[pltpu.VMEM((hd, hd), F32), pltpu.VMEM((LANES, chunk), F32)],
        compiler_params=_cparams("parallel", "arbitrary"),
        name="mlstm",
    )(u3, u3, u3, u3, gates3, conv_w, conv_w, norm_g)


def _memattn_kernel(q_ref, k_ref, v_ref, o_ref, *, tq, seq):
    kb = k_ref[0]
    vb = v_ref[0]
    scale = jnp.asarray(MA_HEAD_DIM ** -0.5, BF16)
    for t in range(seq // tq):
        q = q_ref[0, t * tq:(t + 1) * tq, :] * scale
        s = lax.dot_general(q, kb, (((1,), (1,)), ((), ())), preferred_element_type=F32)
        p = jnp.exp(s - jnp.max(s, axis=1, keepdims=True))
        inv = 1.0 / jnp.sum(p, axis=1, keepdims=True)
        o = jnp.dot(p.astype(BF16), vb, preferred_element_type=F32) * inv
        o_ref[0, t * tq:(t + 1) * tq, :] = o.astype(o_ref.dtype)


def _mem_attention(u3, kv3, tq):
    b, s, _ = u3.shape
    mlen = kv3.shape[1]
    hd = MA_HEAD_DIM
    nq = hd // LANES
    kern = functools.partial(_memattn_kernel, tq=tq, seq=s)
    return pl.pallas_call(
        kern,
        grid=(b, MA_HEADS),
        in_specs=[
            pl.BlockSpec((1, s, hd), lambda bi, h: (bi, 0, OFF_MAQ // nq + h)),
            pl.BlockSpec((1, mlen, hd), lambda bi, h: (bi, 0, h)),
            pl.BlockSpec((1, mlen, hd), lambda bi, h: (bi, 0, MA_HEADS + h)),
        ],
        out_specs=pl.BlockSpec((1, s, hd), lambda bi, h: (bi, 0, h)),
        out_shape=jax.ShapeDtypeStruct((b, s, MA_HEADS * hd), BF16),
        compiler_params=_cparams("parallel", "parallel"),
        name="mem_attn",
    )(u3, kv3, kv3)


def _merge_kernel(ya_ref, ym_ref, yc_ref, g0_ref, g1_ref, g2_ref, x_ref, wb_ref, wo_ref,
                  lg_ref, lb_ref, x1_ref, x1b_ref, x1p_ref):
    acc = None
    for n, (y_ref, g_ref) in enumerate(((ya_ref, g0_ref), (ym_ref, g1_ref), (yc_ref, g2_ref))):
        pr = jnp.dot(y_ref[...], wb_ref[n], preferred_element_type=F32)
        t = jax.nn.sigmoid(g_ref[...].astype(F32)) * pr
        acc = t if acc is None else acc + t
    out = jnp.dot(acc.astype(BF16), wo_ref[...], preferred_element_type=F32)
    x1 = _layer_norm(ALPHA * x_ref[...] + out, lg_ref[...], lb_ref[...])
    x1_ref[...] = x1
    x1b = x1.astype(BF16)
    x1b_ref[...] = x1b
    x1p_ref[...] = _pack_halves(x1b)


def _pack_halves(vb):
    w = vb.shape[1] // 2
    hi = lax.bitcast_convert_type(vb[:, :w].astype(F32), U32)
    lo = lax.bitcast_convert_type(vb[:, w:].astype(F32), U32)
    return hi | (lo >> 16)


def _unpack_halves(u):
    hi = lax.bitcast_convert_type(u & jnp.uint32(0xFFFF0000), F32)
    lo = lax.bitcast_convert_type(u << 16, F32)
    return hi, lo


def _merge(ya, ym, yc, u2, x2, wb, wo, lg, lb, tm):
    t, d = x2.shape
    gb = OFF_GATE * LANES // d

    def rows(i):
        return (i, 0)

    return pl.pallas_call(
        _merge_kernel,
        grid=(t // tm,),
        in_specs=[
            pl.BlockSpec((tm, d), rows), pl.BlockSpec((tm, d), rows), pl.BlockSpec((tm, d), rows),
            pl.BlockSpec((tm, d), lambda i: (i, gb)),
            pl.BlockSpec((tm, d), lambda i: (i, gb + 1)),
            pl.BlockSpec((tm, d), lambda i: (i, gb + 2)),
            pl.BlockSpec((tm, d), rows),
            pl.BlockSpec((N_BRANCH, d, d), lambda i: (0, 0, 0)),
            pl.BlockSpec((d, d), lambda i: (0, 0)),
            pl.BlockSpec((1, d), lambda i: (0, 0)),
            pl.BlockSpec((1, d), lambda i: (0, 0)),
        ],
        out_specs=[pl.BlockSpec((tm, d), rows), pl.BlockSpec((tm, d), rows),
                   pl.BlockSpec((tm, d // 2), rows)],
        out_shape=[jax.ShapeDtypeStruct((t, d), F32), jax.ShapeDtypeStruct((t, d), BF16),
                   jax.ShapeDtypeStruct((t, d // 2), U32)],
        compiler_params=_cparams("parallel"),
        name="merge_ln1",
    )(ya, ym, yc, u2, u2, u2, x2, wb, wo, lg, lb)


def _expert_kernel(ib_ref, ie_ref, lo_ref, hi_ref, first_ref, nit_ref, x_ref, wg_ref, wu_ref,
                   wd_ref, o_ref):
    i = pl.program_id(0)

    @pl.when(i < nit_ref[0])
    def _():
        xl, xr = _unpack_halves(x_ref[...])
        xl = xl.astype(BF16)
        xr = xr.astype(BF16)
        half = xl.shape[1]

        def up(w_ref):
            w = w_ref[0]
            return (jnp.dot(xl, w[:half].astype(BF16), preferred_element_type=F32)
                    + jnp.dot(xr, w[half:].astype(BF16), preferred_element_type=F32))

        hg = up(wg_ref)
        hu = up(wu_ref)
        act = (hg * jax.nn.sigmoid(hg) * hu).astype(BF16)
        y = _pack_halves(jnp.dot(act, wd_ref[0].astype(BF16),
                                 preferred_element_type=F32).astype(BF16))
        row = lax.broadcasted_iota(I32, y.shape, 0)
        mine = (row >= lo_ref[i]) & (row < hi_ref[i])

        @pl.when(first_ref[i] == 1)
        def _():
            o_ref[...] = jnp.where(mine, y, jnp.uint32(0))

        @pl.when(first_ref[i] == 0)
        def _():
            o_ref[...] = jnp.where(mine, y, o_ref[...])


def _work_items(counts, n_pairs, bm):
    assert n_pairs % bm == 0
    nblocks = n_pairs // bm
    ends = jnp.cumsum(counts)
    starts = ends - counts
    first_blk = starts // bm
    n_e = jnp.where(counts > 0, (ends - 1) // bm - first_blk + 1, 0)
    item_end = jnp.cumsum(n_e)
    item_start = item_end - n_e
    n_items = item_end[-1]
    i = jnp.arange(nblocks + N_EXPERTS)
    valid = i < n_items
    e = jnp.minimum(jnp.searchsorted(item_end, jnp.minimum(i, n_items - 1), side='right'),
                    N_EXPERTS - 1)
    blk = jnp.where(valid, first_blk[e] + i - item_start[e], nblocks - 1)
    lo = jnp.clip(starts[e] - blk * bm, 0, bm)
    hi = jnp.where(valid, jnp.clip(ends[e] - blk * bm, 0, bm), 0)
    first = jnp.concatenate([jnp.ones((1,), I32), (blk[1:] != blk[:-1]).astype(I32)])
    items = tuple(a.astype(I32) for a in (blk, e, lo, hi, first, n_items[None]))
    return items, starts.astype(I32)


def _experts(items, xs, w_gate, w_up, w_down, bm):
    n, dh = xs.shape
    d = 2 * dh
    de = w_gate.shape[2]

    def rows(i, ib, ie, lo, hi, first, nit):
        return (ib[i], 0)

    def expert(i, ib, ie, lo, hi, first, nit):
        return (ie[i], 0, 0)

    grid_spec = pltpu.PrefetchScalarGridSpec(
        num_scalar_prefetch=6,
        grid=(items[0].shape[0],),
        in_specs=[
            pl.BlockSpec((bm, dh), rows),
            pl.BlockSpec((1, d, de), expert),
            pl.BlockSpec((1, d, de), expert),
            pl.BlockSpec((1, de, d), expert),
        ],
        out_specs=pl.BlockSpec((bm, dh), rows),
    )
    return pl.pallas_call(
        _expert_kernel,
        grid_spec=grid_spec,
        out_shape=jax.ShapeDtypeStruct((n, dh), U32),
        compiler_params=_cparams("arbitrary"),
        name="experts",
    )(*items, xs, w_gate, w_up, w_down)


def _fetch_tile_indices(dest_hbm, idx_smem, isem):
    i = pl.program_id(0)
    slot = lax.rem(i, 2)
    w = idx_smem.shape[0] // 2

    def idx_copy(step, sl):
        return pltpu.make_async_copy(dest_hbm.at[pl.ds(step * w, w)], idx_smem.at[pl.ds(sl * w, w)],
                                     isem.at[sl])

    @pl.when(i == 0)
    def _():
        idx_copy(0, 0).start()

    idx_copy(i, slot).wait()

    @pl.when(i + 1 < pl.num_programs(0))
    def _():
        idx_copy(i + 1, 1 - slot).start()

    return slot


ROW_GROUP = 8


def _issue_row_copies(idx_smem, slot, tm, make_copy):
    def group(g, c):
        t0 = pl.multiple_of(g * ROW_GROUP, ROW_GROUP)
        base = slot * (TOP_K * tm) + t0
        for u in range(ROW_GROUP):
            for k in range(TOP_K):
                r = idx_smem[base + (k * tm + u)]
                make_copy(r, k, t0 + u).start(priority=k % 2)
        return c

    lax.fori_loop(0, tm // ROW_GROUP, group, 0)


def _ffn_out_kernel(dest_hbm, ys_hbm, xb_ref, x1_ref, wt_ref, wg_ref, wu_ref, wd_ref, lg_ref,
                    lb_ref, o_ref, idx_smem, ybuf, isem, rsem, *, tm):
    slot = _fetch_tile_indices(dest_hbm, idx_smem, isem)

    _issue_row_copies(
        idx_smem, slot, tm,
        lambda r, k, t: pltpu.make_async_copy(ys_hbm.at[pl.ds(r, 1)], ybuf.at[k, pl.ds(t, 1)], rsem))

    xb = xb_ref[...]
    hg = jnp.dot(xb, wg_ref[...], preferred_element_type=F32)
    hu = jnp.dot(xb, wu_ref[...], preferred_element_type=F32)
    act = (hg * jax.nn.sigmoid(hg) * hu).astype(BF16)
    sh = jnp.dot(act, wd_ref[...], preferred_element_type=F32)

    pltpu.make_async_copy(ybuf, ybuf, rsem).wait()
    wt = wt_ref[...]
    rl = None
    rr = None
    for k in range(TOP_K):
        hi, lo = _unpack_halves(ybuf[k])
        wk = wt[:, k:k + 1]
        rl = wk * hi if rl is None else rl + wk * hi
        rr = wk * lo if rr is None else rr + wk * lo
    z = ALPHA * x1_ref[...] + sh + jnp.concatenate([rl, rr], axis=1)
    o_ref[...] = _layer_norm(z, lg_ref[...], lb_ref[...])


def _ffn_out(dest, ys, x1b, x1, wt, wg, wu, wd, lg, lb, tm):
    t, d = x1.shape
    ds = wg.shape[1]
    dh = ys.shape[1]
    assert dest.shape == (t * TOP_K,)

    def rows(i):
        return (i, 0)

    def whole(i):
        return (0, 0)

    return pl.pallas_call(
        functools.partial(_ffn_out_kernel, tm=tm),
        grid=(t // tm,),
        in_specs=[
            pl.BlockSpec(memory_space=pl.ANY), pl.BlockSpec(memory_space=pl.ANY),
            pl.BlockSpec((tm, d), rows), pl.BlockSpec((tm, d), rows), pl.BlockSpec((tm, LANES), rows),
            pl.BlockSpec((d, ds), whole), pl.BlockSpec((d, ds), whole), pl.BlockSpec((ds, d), whole),
            pl.BlockSpec((1, d), whole), pl.BlockSpec((1, d), whole),
        ],
        out_specs=pl.BlockSpec((tm, d), rows),
        out_shape=jax.ShapeDtypeStruct((t, d), F32),
        scratch_shapes=[pltpu.SMEM((2 * TOP_K * tm,), I32), pltpu.VMEM((TOP_K, tm, dh), U32),
                        pltpu.SemaphoreType.DMA((2,)), pltpu.SemaphoreType.DMA(())],
        compiler_params=_cparams("arbitrary"),
        name="ffn_out_ln2",
    )(dest, ys, x1b, x1, wt, wg, wu, wd, lg, lb)


def _dispatch_kernel(dest_hbm, xp_ref, xs_hbm, idx_smem, isem, rsem, *, tm):
    slot = _fetch_tile_indices(dest_hbm, idx_smem, isem)

    _issue_row_copies(
        idx_smem, slot, tm,
        lambda r, k, t: pltpu.make_async_copy(xp_ref.at[pl.ds(t, 1)], xs_hbm.at[pl.ds(r, 1)], rsem))
    done = xs_hbm.at[pl.ds(0, TOP_K * tm)]
    pltpu.make_async_copy(done, done, rsem).wait()


def _dispatch(dest, x1p, npad, tm):
    t, dh = x1p.shape
    assert npad >= TOP_K * tm
    return pl.pallas_call(
        functools.partial(_dispatch_kernel, tm=tm),
        grid=(t // tm,),
        in_specs=[pl.BlockSpec(memory_space=pl.ANY), pl.BlockSpec((tm, dh), lambda i: (i, 0))],
        out_specs=pl.BlockSpec(memory_space=pl.ANY),
        out_shape=jax.ShapeDtypeStruct((npad, dh), U32),
        scratch_shapes=[pltpu.SMEM((2 * TOP_K * tm,), I32), pltpu.SemaphoreType.DMA((2,)),
                        pltpu.SemaphoreType.DMA(())],
        compiler_params=_cparams("arbitrary"),
        name="dispatch",
    )(dest, x1p)


def _route_kernel(xb_ref, wrt_ref, rb_ref, ek_ref, rk_ref, wt_ref, cnt_ref, upper_ref, run_ref, *,
                  tm):
    i = pl.program_id(0)
    gsz = N_EXPERTS // N_GROUP
    ninf = -jnp.inf

    @pl.when(i == 0)
    def _():
        r = lax.broadcasted_iota(I32, (tm, tm), 0)
        c = lax.broadcasted_iota(I32, (tm, tm), 1)
        upper_ref[...] = jnp.where(r < c, 1.0, 0.0).astype(BF16)
        run_ref[...] = jnp.zeros(run_ref.shape, F32)

    logits = lax.dot_general(wrt_ref[...], xb_ref[...], (((1,), (1,)), ((), ())),
                             preferred_element_type=F32)
    scores = jax.nn.sigmoid(logits)
    choice = scores + rb_ref[...]

    ridx = lax.broadcasted_iota(I32, (gsz, tm), 0)
    gscore = []
    for g in range(N_GROUP):
        blk = choice[g * gsz:(g + 1) * gsz, :]
        m1 = jnp.max(blk, axis=0, keepdims=True)
        i1 = jnp.min(jnp.where(blk == m1, ridx, gsz), axis=0, keepdims=True)
        m2 = jnp.max(jnp.where(ridx == i1, ninf, blk), axis=0, keepdims=True)
        gscore.append(m1 + m2)
    masked = []
    for g in range(N_GROUP):
        beaten = jnp.zeros((1, tm), I32)
        for g2 in range(N_GROUP):
            if g2 == g:
                continue
            wins = (gscore[g2] >= gscore[g]) if g2 < g else (gscore[g2] > gscore[g])
            beaten = beaten + jnp.where(wins, 1, 0)
        masked.append(jnp.where(beaten < TOPK_GROUP, choice[g * gsz:(g + 1) * gsz, :], ninf))
    v = jnp.concatenate(masked, axis=0)

    eidx = lax.broadcasted_iota(I32, (N_EXPERTS, tm), 0)
    sel = jnp.zeros((N_EXPERTS, tm), F32)
    e_rows = []
    s_rows = []
    for k in range(TOP_K):
        m = jnp.max(v, axis=0, keepdims=True)
        ik = jnp.min(jnp.where(v == m, eidx, N_EXPERTS), axis=0, keepdims=True)
        hit = eidx == ik
        e_rows.append(ik)
        s_rows.append(jnp.sum(jnp.where(hit, scores, 0.0), axis=0, keepdims=True))
        v = jnp.where(hit, ninf, v)
        sel = jnp.where(hit, 1.0, sel)

    prefix = jnp.dot(sel.astype(BF16), upper_ref[...], preferred_element_type=F32)
    pos = prefix + run_ref[...]
    for k in range(TOP_K):
        rk = jnp.sum(jnp.where(eidx == e_rows[k], pos, 0.0), axis=0, keepdims=True)
        ek_ref[k:k + 1, :] = e_rows[k]
        rk_ref[k:k + 1, :] = rk.astype(I32)
    run_ref[...] = run_ref[...] + jnp.sum(sel, axis=1, keepdims=True)
    cnt_ref[...] = jnp.broadcast_to(run_ref[...], cnt_ref.shape).astype(I32)

    ssum = s_rows[0]
    for k in range(1, TOP_K):
        ssum = ssum + s_rows[k]
    w_rows = [s / (ssum + 1e-20) * ROUTED_SCALE for s in s_rows]
    w_rows.append(jnp.zeros((LANES - TOP_K, tm), F32))
    wt_ref[...] = jnp.concatenate(w_rows, axis=0).T


def _route(x1b, wrt, rbias, tm):
    t, d = x1b.shape
    return pl.pallas_call(
        functools.partial(_route_kernel, tm=tm),
        grid=(t // tm,),
        in_specs=[pl.BlockSpec((tm, d), lambda i: (i, 0)),
                  pl.BlockSpec((N_EXPERTS, d), lambda i: (0, 0)),
                  pl.BlockSpec((N_EXPERTS, 1), lambda i: (0, 0))],
        out_specs=[pl.BlockSpec((TOP_K, tm), lambda i: (0, i)),
                   pl.BlockSpec((TOP_K, tm), lambda i: (0, i)),
                   pl.BlockSpec((tm, LANES), lambda i: (i, 0)),
                   pl.BlockSpec((N_EXPERTS, LANES), lambda i: (0, 0))],
        out_shape=[jax.ShapeDtypeStruct((TOP_K, t), I32), jax.ShapeDtypeStruct((TOP_K, t), I32),
                   jax.ShapeDtypeStruct((t, LANES), F32),
                   jax.ShapeDtypeStruct((N_EXPERTS, LANES), I32)],
        scratch_shapes=[pltpu.VMEM((tm, tm), BF16), pltpu.VMEM((N_EXPERTS, 1), F32)],
        compiler_params=_cparams("arbitrary"),
        name="route",
    )(x1b, wrt, rbias)


def _dest_kernel(ps_ref, ek_ref, rk_ref, d_ref, *, tm):
    e = ek_ref[...]

    def body(j, acc):
        return acc + jnp.where(e == j, ps_ref[j], 0)

    res = lax.fori_loop(0, N_EXPERTS, body, jnp.zeros(e.shape, I32)) + rk_ref[...]
    for a in range(e.shape[1] // tm):
        for k in range(TOP_K):
            d_ref[a:a + 1, k * tm:(k + 1) * tm] = res[k:k + 1, a * tm:(a + 1) * tm]


def _dest(pstart, ek, rk, tm, tw):
    t = ek.shape[1]
    grid_spec = pltpu.PrefetchScalarGridSpec(
        num_scalar_prefetch=1,
        grid=(t // tw,),
        in_specs=[pl.BlockSpec((TOP_K, tw), lambda i, ps: (0, i)),
                  pl.BlockSpec((TOP_K, tw), lambda i, ps: (0, i))],
        out_specs=pl.BlockSpec((tw // tm, TOP_K * tm), lambda i, ps: (i, 0)),
    )
    return pl.pallas_call(
        functools.partial(_dest_kernel, tm=tm),
        grid_spec=grid_spec,
        out_shape=jax.ShapeDtypeStruct((t // tm, TOP_K * tm), I32),
        compiler_params=_cparams("parallel"),
        name="dest",
    )(pstart, ek, rk)


def _layer(x, mem, rel_bias, w_in, b_in, conv_w, diff_lambda, subln_g, mlstm_norm_g, w_mem_kv,
           w_branch, w_out, ln1_g, ln1_b, w_router, router_bias, w_e_gate, w_e_up, w_e_down,
           w_s_gate, w_s_up, w_s_down, ln2_g, ln2_b, layer_idx, cfg):
    b, s, d = x.shape
    t = b * s
    x2 = x.reshape(t, d)

    g0 = (OFF_MLO + 8) * LANES
    w_main = jnp.concatenate([w_in[:, :g0], w_in[:, g0 + 2 * ML_HEADS:]], axis=1).astype(BF16)
    b_main = jnp.concatenate([b_in[:g0], b_in[g0 + 2 * ML_HEADS:]])[None, :]
    w_g = jnp.pad(w_in[:, g0:g0 + 2 * ML_HEADS], ((0, 0), (0, LANES - 2 * ML_HEADS))).astype(BF16)
    b_g = jnp.pad(b_in[g0:g0 + 2 * ML_HEADS], (0, LANES - 2 * ML_HEADS))[None, :]

    u2, gates2 = _proj_in(x2, w_main, b_main, w_g, b_g, cfg["proj_tm"], cfg["proj_tn"])
    u3 = u2.reshape(b, s, N_MAIN)
    gates3 = gates2.reshape(b, s, LANES)

    lam_init = 0.8 - 0.6 * math.exp(-0.3 * layer_idx)
    y_a = _diff_attention(u3, rel_bias, diff_lambda, subln_g[None, :], cfg["attn_tq"], lam_init)
    y_m = _mlstm(u3, gates3, conv_w, mlstm_norm_g[None, :], cfg["ml_chunk"])
    kv = _mm(mem.reshape(-1, d), w_mem_kv.astype(BF16), BF16, cfg["kv_tm"], "mem_kv")
    y_c = _mem_attention(u3, kv.reshape(b, -1, 2 * MA_HEADS * MA_HEAD_DIM), cfg["ma_tq"])

    x1, x1b, x1p = _merge(y_a.reshape(t, d), y_m.reshape(t, d), y_c.reshape(t, d), u2, x2,
                          w_branch.astype(BF16), w_out.astype(BF16), ln1_g[None, :],
                          ln1_b[None, :], cfg["merge_tm"])

    ek, rk, wt, cnt = _route(x1b, w_router.T.astype(BF16), router_bias.astype(F32)[:, None],
                             cfg["route_tm"])
    bm = cfg["expert_bm"]
    items, starts = _work_items(cnt[:, 0], t * TOP_K, bm)
    dest = _dest(starts, ek, rk, cfg["moe_tm"], cfg["dest_tw"]).reshape(-1)
    xs = _dispatch(dest, x1p, t * TOP_K, cfg["moe_tm"])
    ys = _experts(items, xs, w_e_gate, w_e_up, w_e_down, bm)
    out = _ffn_out(dest, ys, x1b, x1, wt, w_s_gate.astype(BF16), w_s_up.astype(BF16),
                   w_s_down.astype(BF16), ln2_g[None, :], ln2_b[None, :], cfg["moe_tm"])
    return out.reshape(b, s, d)


def _config(b, s):
    t = b * s
    return {
        "proj_tm": min(1024, t), "proj_tn": 1024,
        "attn_tq": 256, "ml_chunk": 256, "kv_tm": 512, "ma_tq": min(512, s),
        "merge_tm": 256, "route_tm": 512, "expert_bm": 512, "moe_tm": 256, "dest_tw": 2048,
    }


def kernel(x, mem, rel_bias, w_in, b_in, conv_w, diff_lambda, subln_g, mlstm_norm_g, w_mem_kv,
           w_branch, w_out, ln1_g, ln1_b, w_router, router_bias, w_e_gate, w_e_up, w_e_down,
           w_s_gate, w_s_up, w_s_down, ln2_g, ln2_b):
    cfg = _config(x.shape[0], x.shape[1])
    for l in range(DEPTH):
        x = _layer(x, mem, rel_bias, w_in[l], b_in[l], conv_w[l], diff_lambda[l], subln_g[l],
                   mlstm_norm_g[l], w_mem_kv[l], w_branch[l], w_out[l], ln1_g[l], ln1_b[l],
                   w_router[l], router_bias[l], w_e_gate[l], w_e_up[l], w_e_down[l],
                   w_s_gate[l], w_s_up[l], w_s_down[l], ln2_g[l], ln2_b[l], l, cfg)
    return x
```

```python
import functools
import math

import numpy as np
import jax
import jax.numpy as jnp
from jax import lax
from jax.experimental import pallas as pl
from jax.experimental.pallas import tpu as pltpu

F32 = jnp.float32
BF16 = jnp.bfloat16
U32 = jnp.uint32
I32 = jnp.int32

D_MODEL = 1024
DEPTH = 1
DA_HEAD_DIM = 64
DA_V_DIM = 128
DA_HEADS = 8
ML_HEADS = 4
ML_HEAD_DIM = 256
ML_CONV = 4
MA_HEADS = 4
MA_HEAD_DIM = 256
N_BRANCH = 3
REL_BUCKETS = 32
REL_MAX_DIST = 128
N_EXPERTS = 256
TOP_K = 8
N_GROUP = 8
TOPK_GROUP = 4
D_EXPERT = 256
ROUTED_SCALE = 2.5
ALPHA = (2.0 * DEPTH) ** 0.25

LANES = 128
NEG = -1e30
LOG2E = math.log2(math.e)
FOLD_ACCS = 4
VMEM_LIMIT = 56 * 1024 * 1024

OFF_DAQ, OFF_DAK, OFF_DAV = 0, 8, 16
OFF_MLQ, OFF_MLK, OFF_MLV, OFF_MLO, OFF_MAQ, OFF_GATE = 24, 32, 40, 48, 56, 64
N_MAIN = 88 * LANES


def _cparams(*sem):
    return pltpu.CompilerParams(dimension_semantics=sem, vmem_limit_bytes=VMEM_LIMIT)


def _layer_norm(z, g, b):
    mu = jnp.mean(z, axis=-1, keepdims=True)
    zc = z - mu
    var = jnp.mean(zc * zc, axis=-1, keepdims=True)
    return zc * lax.rsqrt(var + 1e-5) * g + b


def _proj_in_kernel(x_ref, w_ref, b_ref, wg_ref, bg_ref, u_ref, g_ref, xs_ref):
    @pl.when(pl.program_id(1) == 0)
    def _():
        xb = x_ref[...].astype(BF16)
        xs_ref[...] = xb
        g_ref[...] = jnp.dot(xb, wg_ref[...], preferred_element_type=F32) + bg_ref[...]

    acc = jnp.dot(xs_ref[...], w_ref[...], preferred_element_type=F32)
    u_ref[...] = (acc + b_ref[...]).astype(u_ref.dtype)


def _proj_in(x2, w_main, b_main, w_g, b_g, tm, tn):
    t, k = x2.shape
    n = w_main.shape[1]
    return pl.pallas_call(
        _proj_in_kernel,
        grid=(t // tm, n // tn),
        in_specs=[
            pl.BlockSpec((tm, k), lambda i, j: (i, 0)),
            pl.BlockSpec((k, tn), lambda i, j: (0, j)),
            pl.BlockSpec((1, tn), lambda i, j: (0, j)),
            pl.BlockSpec((k, LANES), lambda i, j: (0, 0)),
            pl.BlockSpec((1, LANES), lambda i, j: (0, 0)),
        ],
        out_specs=[
            pl.BlockSpec((tm, tn), lambda i, j: (i, j)),
            pl.BlockSpec((tm, LANES), lambda i, j: (i, 0)),
        ],
        out_shape=[jax.ShapeDtypeStruct((t, n), BF16), jax.ShapeDtypeStruct((t, LANES), F32)],
        scratch_shapes=[pltpu.VMEM((tm, k), BF16)],
        compiler_params=_cparams("parallel", "arbitrary"),
        name="proj_in",
    )(x2, w_main, b_main, w_g, b_g)


def _mm_kernel(x_ref, w_ref, o_ref):
    o_ref[...] = jnp.dot(x_ref[...].astype(BF16), w_ref[...],
                         preferred_element_type=F32).astype(o_ref.dtype)


def _mm(x2, w, out_dtype, tm, name):
    m, k = x2.shape
    n = w.shape[1]
    return pl.pallas_call(
        _mm_kernel,
        grid=(m // tm,),
        in_specs=[pl.BlockSpec((tm, k), lambda i: (i, 0)), pl.BlockSpec((k, n), lambda i: (0, 0))],
        out_specs=pl.BlockSpec((tm, n), lambda i: (i, 0)),
        out_shape=jax.ShapeDtypeStruct((m, n), out_dtype),
        compiler_params=_cparams("parallel"),
        name=name,
    )(x2, w)


def _fold8(x, op):
    n = x.shape[0] // 8
    accs = [x[8 * a:8 * a + 8, :] for a in range(min(FOLD_ACCS, n))]
    for a in range(FOLD_ACCS, n):
        accs[a % FOLD_ACCS] = op(accs[a % FOLD_ACCS], x[8 * a:8 * a + 8, :])
    while len(accs) > 1:
        accs = [op(accs[a], accs[a + 1]) for a in range(0, len(accs), 2)]
    return accs[0]


def _dattn_kernel(rb_ref, q_ref, k_ref, v_ref, bkt_ref, dl_ref, g_ref, o_ref,
                  st_ref, pt_ref, vt_ref, nb_ref, *, tq, seq, lam_init):
    h = pl.program_id(0)
    nq = seq // tq
    nt = (((1,), (1,)), ((), ()))

    @pl.when(pl.program_id(1) == 0)
    def _():
        far = rb_ref[REL_BUCKETS - 1, h]
        for t in range(2):
            bk = bkt_ref[t]
            tile = jnp.full((tq, tq), NEG, F32)
            for bb in range(REL_BUCKETS):
                tile = jnp.where(bk == bb, (rb_ref[bb, h] - far) * LOG2E, tile)
            nb_ref[t * tq:(t + 1) * tq, :] = tile

    for j in range(nq):
        vt_ref[:, j * tq:(j + 1) * tq] = v_ref[0, j * tq:(j + 1) * tq, :].astype(F32).T.astype(BF16)

    dl = dl_ref[...]
    lam = (jnp.exp(jnp.sum(dl[0:1] * dl[1:2], axis=1, keepdims=True))
           - jnp.exp(jnp.sum(dl[2:3] * dl[3:4], axis=1, keepdims=True)) + lam_init)
    lane = lax.broadcasted_iota(I32, (tq, LANES), 1)
    gain = g_ref[...] * (1.0 - lam_init)

    def logits_pass(i):
        qs = q_ref[0, i * tq:(i + 1) * tq, :] * jnp.asarray(DA_HEAD_DIM ** -0.5 * LOG2E, BF16)
        zero = jnp.zeros_like(qs)
        qm = (jnp.where(lane < DA_HEAD_DIM, qs, zero), jnp.where(lane >= DA_HEAD_DIM, qs, zero))
        mx = [None, None]

        def chunk(c):
            for half in range(2):
                s = lax.dot_general(k_ref[0, c * tq:(c + 1) * tq, :], qm[half], nt,
                                    preferred_element_type=F32)
                if c >= i - 1:
                    s = s + nb_ref[(c - i + 1) * tq:(c - i + 2) * tq, :]
                st_ref[2 * (i % 2) + half, c * tq:(c + 1) * tq, :] = s
                f = _fold8(s, jnp.maximum)
                mx[half] = f if mx[half] is None else jnp.maximum(mx[half], f)

        def result():
            return [jnp.max(mx[half], axis=0, keepdims=True) for half in range(2)]

        return [functools.partial(chunk, c) for c in range(i + 1)], result

    def exp_pass(i, m):
        l8 = [None, None]

        def chunk(c):
            for half in range(2):
                p = jnp.exp2(st_ref[2 * (i % 2) + half, c * tq:(c + 1) * tq, :] - m[half])
                pt_ref[2 * (i % 2) + half, c * tq:(c + 1) * tq, :] = p.astype(BF16)
                f = _fold8(p, jnp.add)
                l8[half] = f if l8[half] is None else l8[half] + f

        def result():
            return [jnp.sum(l8[half], axis=0, keepdims=True) for half in range(2)]

        return [functools.partial(chunk, c) for c in range(i + 1)], result

    def value_pass(i, l1, l2):
        kv = (i + 1) * tq
        acc = [jnp.dot(vt_ref[:, 0:kv], pt_ref[2 * (i % 2) + half, 0:kv, :],
                       preferred_element_type=F32) for half in range(2)]
        o = (acc[0] * (1.0 / l1) - acc[1] * (lam / l2)).T
        ms = jnp.mean(o * o, axis=1, keepdims=True)
        y = o * lax.rsqrt(ms + 1e-5) * gain
        o_ref[0, i * tq:(i + 1) * tq, :] = y.astype(o_ref.dtype)

    chunks, result = logits_pass(0)
    for run in chunks:
        run()
    m_next = result()
    for i in range(nq):
        exp_chunks, exp_result = exp_pass(i, m_next)
        next_chunks, next_result = logits_pass(i + 1) if i + 1 < nq else ([], None)
        for c in range(max(len(exp_chunks), len(next_chunks))):
            if c < len(next_chunks):
                next_chunks[c]()
            if c < len(exp_chunks):
                exp_chunks[c]()
        if next_result is not None:
            m_next = next_result()
        value_pass(i, *exp_result())


def _t5_bucket_np(dist):
    n = np.maximum(dist, 0)
    max_exact = REL_BUCKETS // 2
    large = max_exact + (np.log(np.maximum(n, 1).astype(np.float32) / max_exact)
                         / math.log(REL_MAX_DIST / max_exact)
                         * (REL_BUCKETS - max_exact)).astype(np.int32)
    large = np.minimum(large, REL_BUCKETS - 1)
    return np.where(n < max_exact, n, large)


def _near_buckets(tq):
    assert _t5_bucket_np(np.array([tq + 1]))[0] == REL_BUCKETS - 1
    c = np.arange(tq)[:, None]
    r = np.arange(tq)[None, :]
    left = _t5_bucket_np(r + tq - c)
    diag = np.where(r >= c, _t5_bucket_np(r - c), -1)
    return np.stack([left, diag], axis=0).astype(np.int32)


def _diff_attention(u3, rel_bias, diff_lambda, subln_g, tq, lam_init):
    b, s, _ = u3.shape
    kern = functools.partial(_dattn_kernel, tq=tq, seq=s, lam_init=lam_init)
    grid_spec = pltpu.PrefetchScalarGridSpec(
        num_scalar_prefetch=1,
        grid=(DA_HEADS, b),
        in_specs=[
            pl.BlockSpec((1, s, LANES), lambda h, bi, rb: (bi, 0, OFF_DAQ + h)),
            pl.BlockSpec((1, s, LANES), lambda h, bi, rb: (bi, 0, OFF_DAK + h)),
            pl.BlockSpec((1, s, LANES), lambda h, bi, rb: (bi, 0, OFF_DAV + h)),
            pl.BlockSpec((2, tq, tq), lambda h, bi, rb: (0, 0, 0)),
            pl.BlockSpec((4, DA_HEAD_DIM), lambda h, bi, rb: (0, 0)),
            pl.BlockSpec((1, DA_V_DIM), lambda h, bi, rb: (0, 0)),
        ],
        out_specs=pl.BlockSpec((1, s, LANES), lambda h, bi, rb: (bi, 0, h)),
        scratch_shapes=[pltpu.VMEM((4, s, tq), F32), pltpu.VMEM((4, s, tq), BF16),
                        pltpu.VMEM((DA_V_DIM, s), BF16),
                        pltpu.VMEM((2 * tq, tq), F32)],
    )
    return pl.pallas_call(
        kern,
        grid_spec=grid_spec,
        out_shape=jax.ShapeDtypeStruct((b, s, DA_HEADS * DA_V_DIM), BF16),
        compiler_params=_cparams("arbitrary", "arbitrary"),
        name="diff_attn",
    )(rel_bias.astype(F32), u3, u3, u3, jnp.asarray(_near_buckets(tq)), diff_lambda, subln_g)


def _mlstm_kernel(q_ref, k_ref, v_ref, og_ref, g_ref, cwq_ref, cwk_ref, ng_ref, y_ref,
                  ct_ref, gts_ref, *, chunk, seq):
    h = pl.program_id(1)
    nc = seq // chunk
    hd = ML_HEAD_DIM
    lane = lax.broadcasted_iota(jnp.int32, (chunk, LANES), 1)
    row = lax.broadcasted_iota(jnp.int32, (chunk, chunk), 0)
    col = lax.broadcasted_iota(jnp.int32, (chunk, chunk), 1)
    causal = row >= col
    tri = causal.astype(F32)
    cwq = cwq_ref[...]
    cwk = cwk_ref[...]
    ng = ng_ref[...]
    ct_ref[...] = jnp.zeros(ct_ref.shape, F32)

    def conv_silu(ref, cw, r0, c):
        cur = ref[0, pl.ds(r0, chunk), :].astype(F32)
        p0 = pl.multiple_of(jnp.maximum(r0 - 16, 0), 16)
        prev = ref[0, pl.ds(p0, 16), :].astype(F32)
        prev = jnp.where(c > 0, prev, 0.0)
        x = jnp.concatenate([prev, cur], axis=0)
        out = cw[ML_CONV - 1:ML_CONV] * cur
        for tap in range(ML_CONV - 1):
            shifted = pltpu.roll(x, ML_CONV - 1 - tap, 0)[16:]
            out = out + cw[tap:tap + 1] * shifted
        return out * jax.nn.sigmoid(out)

    def body(c, carry):
        m, n = carry
        r0 = pl.multiple_of(c * chunk, chunk)
        g = g_ref[0, pl.ds(r0, chunk), :]
        logf = jnp.minimum(g, 0.0) - jnp.log(1.0 + jnp.exp(-jnp.abs(g)))
        bc = jnp.dot(tri, logf, precision=lax.Precision.HIGHEST, preferred_element_type=F32)
        gts_ref[...] = jnp.where(lane < ML_HEADS, g, bc).T
        irow = gts_ref[pl.ds(h, 1), :]
        brow = gts_ref[pl.ds(ML_HEADS + h, 1), :]
        bcol = jnp.sum(jnp.where(lane == ML_HEADS + h, bc, 0.0), axis=1, keepdims=True)
        icol = jnp.sum(jnp.where(lane == h, g, 0.0), axis=1, keepdims=True)

        dmat = jnp.where(causal, bcol - brow + irow, NEG)
        inter = bcol + m
        m_row = jnp.maximum(inter, jnp.max(dmat, axis=1, keepdims=True))

        q = conv_silu(q_ref, cwq, r0, c)
        k = conv_silu(k_ref, cwk, r0, c) * (hd ** -0.5)
        qb = q.astype(BF16)
        kb = k.astype(BF16)
        vb = v_ref[0, pl.ds(r0, chunk), :]
        sqk = lax.dot_general(qb, kb, (((1,), (1,)), ((), ())), preferred_element_type=F32)
        w = jnp.exp(dmat - m_row) * sqk
        inter_w = jnp.exp(inter - m_row)
        ct = ct_ref[...]
        num = (inter_w * jnp.dot(qb, ct.astype(BF16), preferred_element_type=F32)
               + jnp.dot(w.astype(BF16), vb, preferred_element_type=F32))
        den = inter_w * jnp.sum(q * n, axis=1, keepdims=True) + jnp.sum(w, axis=1, keepdims=True)
        hout = num / jnp.maximum(jnp.abs(den), jnp.exp(-m_row))
        mu = jnp.mean(hout, axis=1, keepdims=True)
        hc = hout - mu
        var = jnp.mean(hc * hc, axis=1, keepdims=True)
        hn = hc * lax.rsqrt(var + 1e-5) * ng
        og = og_ref[0, pl.ds(r0, chunk), :].astype(F32)
        y_ref[0, pl.ds(r0, chunk), :] = (jax.nn.sigmoid(og) * hn).astype(y_ref.dtype)

        total = brow[:, chunk - 1:chunk]
        grow = total - brow + irow
        m_new = jnp.maximum(total + m, jnp.max(grow, axis=1, keepdims=True))
        decay = jnp.exp(total + m - m_new)
        ws = jnp.exp(total - bcol + icol - m_new)
        wsv = (ws * vb.astype(F32)).astype(BF16)
        ct_ref[...] = decay * ct + jnp.dot(k.T.astype(BF16), wsv, preferred_element_type=F32)
        n_new = decay * n + jnp.sum(ws * k, axis=0, keepdims=True)
        return m_new, n_new

    lax.fori_loop(0, nc, body, (jnp.zeros((1, 1), F32), jnp.zeros((1, hd), F32)))


def _mlstm(u3, gates3, conv_w, norm_g, chunk):
    b, s, _ = u3.shape
    hd = ML_HEAD_DIM
    nq = hd // LANES
    kern = functools.partial(_mlstm_kernel, chunk=chunk, seq=s)

    def ublock(off):
        return pl.BlockSpec((1, s, hd), lambda bi, h: (bi, 0, off // nq + h))

    return pl.pallas_call(
        kern,
        grid=(b, ML_HEADS),
        in_specs=[
            ublock(OFF_MLQ), ublock(OFF_MLK), ublock(OFF_MLV), ublock(OFF_MLO),
            pl.BlockSpec((1, s, LANES), lambda bi, h: (bi, 0, 0)),
            pl.BlockSpec((ML_CONV, hd), lambda bi, h: (0, h)),
            pl.BlockSpec((ML_CONV, hd), lambda bi, h: (0, ML_HEADS + h)),
            pl.BlockSpec((1, hd), lambda bi, h: (0, h)),
        ],
        out_specs=pl.BlockSpec((1, s, hd), lambda bi, h: (bi, 0, h)),
        out_shape=jax.ShapeDtypeStruct((b, s, ML_HEADS * hd), BF16),
        scratch_shapes=[pltpu.VMEM((hd, hd), F32), pltpu.VMEM((LANES, chunk), F32)],
        compiler_params=_cparams("parallel", "arbitrary"),
        name="mlstm",
    )(u3, u3, u3, u3, gates3, conv_w, conv_w, norm_g)


def _memattn_kernel(q_ref, k_ref, v_ref, o_ref, *, tq, seq):
    kb = k_ref[0]
    vb = v_ref[0]
    scale = jnp.asarray(MA_HEAD_DIM ** -0.5, BF16)
    for t in range(seq // tq):
        q = q_ref[0, t * tq:(t + 1) * tq, :] * scale
        s = lax.dot_general(q, kb, (((1,), (1,)), ((), ())), preferred_element_type=F32)
        p = jnp.exp(s - jnp.max(s, axis=1, keepdims=True))
        inv = 1.0 / jnp.sum(p, axis=1, keepdims=True)
        o = jnp.dot(p.astype(BF16), vb, preferred_element_type=F32) * inv
        o_ref[0, t * tq:(t + 1) * tq, :] = o.astype(o_ref.dtype)


def _mem_attention(u3, kv3, tq):
    b, s, _ = u3.shape
    mlen = kv3.shape[1]
    hd = MA_HEAD_DIM
    nq = hd // LANES
    kern = functools.partial(_memattn_kernel, tq=tq, seq=s)
    return pl.pallas_call(
        kern,
        grid=(b, MA_HEADS),
        in_specs=[
            pl.BlockSpec((1, s, hd), lambda bi, h: (bi, 0, OFF_MAQ // nq + h)),
            pl.BlockSpec((1, mlen, hd), lambda bi, h: (bi, 0, h)),
            pl.BlockSpec((1, mlen, hd), lambda bi, h: (bi, 0, MA_HEADS + h)),
        ],
        out_specs=pl.BlockSpec((1, s, hd), lambda bi, h: (bi, 0, h)),
        out_shape=jax.ShapeDtypeStruct((b, s, MA_HEADS * hd), BF16),
        compiler_params=_cparams("parallel", "parallel"),
        name="mem_attn",
    )(u3, kv3, kv3)


def _merge_kernel(ya_ref, ym_ref, yc_ref, g0_ref, g1_ref, g2_ref, x_ref, wb_ref, wo_ref,
                  lg_ref, lb_ref, x1_ref, x1b_ref, x1p_ref):
    acc = None
    for n, (y_ref, g_ref) in enumerate(((ya_ref, g0_ref), (ym_ref, g1_ref), (yc_ref, g2_ref))):
        pr = jnp.dot(y_ref[...], wb_ref[n], preferred_element_type=F32)
        t = jax.nn.sigmoid(g_ref[...].astype(F32)) * pr
        acc = t if acc is None else acc + t
    out = jnp.dot(acc.astype(BF16), wo_ref[...], preferred_element_type=F32)
    x1 = _layer_norm(ALPHA * x_ref[...] + out, lg_ref[...], lb_ref[...])
    x1_ref[...] = x1
    x1b = x1.astype(BF16)
    x1b_ref[...] = x1b
    x1p_ref[...] = _pack_halves(x1b)


def _pack_halves(vb):
    w = vb.shape[1] // 2
    hi = lax.bitcast_convert_type(vb[:, :w].astype(F32), U32)
    lo = lax.bitcast_convert_type(vb[:, w:].astype(F32), U32)
    return hi | (lo >> 16)


def _unpack_halves(u):
    hi = lax.bitcast_convert_type(u & jnp.uint32(0xFFFF0000), F32)
    lo = lax.bitcast_convert_type(u << 16, F32)
    return hi, lo


def _merge(ya, ym, yc, u2, x2, wb, wo, lg, lb, tm):
    t, d = x2.shape
    gb = OFF_GATE * LANES // d

    def rows(i):
        return (i, 0)

    return pl.pallas_call(
        _merge_kernel,
        grid=(t // tm,),
        in_specs=[
            pl.BlockSpec((tm, d), rows), pl.BlockSpec((tm, d), rows), pl.BlockSpec((tm, d), rows),
            pl.BlockSpec((tm, d), lambda i: (i, gb)),
            pl.BlockSpec((tm, d), lambda i: (i, gb + 1)),
            pl.BlockSpec((tm, d), lambda i: (i, gb + 2)),
            pl.BlockSpec((tm, d), rows),
            pl.BlockSpec((N_BRANCH, d, d), lambda i: (0, 0, 0)),
            pl.BlockSpec((d, d), lambda i: (0, 0)),
            pl.BlockSpec((1, d), lambda i: (0, 0)),
            pl.BlockSpec((1, d), lambda i: (0, 0)),
        ],
        out_specs=[pl.BlockSpec((tm, d), rows), pl.BlockSpec((tm, d), rows),
                   pl.BlockSpec((tm, d // 2), rows)],
        out_shape=[jax.ShapeDtypeStruct((t, d), F32), jax.ShapeDtypeStruct((t, d), BF16),
                   jax.ShapeDtypeStruct((t, d // 2), U32)],
        compiler_params=_cparams("parallel"),
        name="merge_ln1",
    )(ya, ym, yc, u2, u2, u2, x2, wb, wo, lg, lb)


def _expert_kernel(ib_ref, ie_ref, lo_ref, hi_ref, first_ref, nit_ref, x_ref, wg_ref, wu_ref,
                   wd_ref, o_ref, wgb_ref, wub_ref, wdb_ref):
    i = pl.program_id(0)

    @pl.when((i < nit_ref[0]) & ((i == 0) | (ie_ref[i] != ie_ref[jnp.maximum(i - 1, 0)])))
    def _():
        wgb_ref[...] = wg_ref[0].astype(BF16)
        wub_ref[...] = wu_ref[0].astype(BF16)
        wdb_ref[...] = wd_ref[0].astype(BF16)

    @pl.when(i < nit_ref[0])
    def _():
        xl, xr = _unpack_halves(x_ref[...])
        xl = xl.astype(BF16)
        xr = xr.astype(BF16)
        half = xl.shape[1]

        def up(w_ref):
            return (jnp.dot(xl, w_ref[:half, :], preferred_element_type=F32)
                    + jnp.dot(xr, w_ref[half:, :], preferred_element_type=F32))

        hg = up(wgb_ref)
        hu = up(wub_ref)
        act = (hg * jax.nn.sigmoid(hg) * hu).astype(BF16)
        y = _pack_halves(jnp.dot(act, wdb_ref[...], preferred_element_type=F32).astype(BF16))
        row = lax.broadcasted_iota(I32, y.shape, 0)
        mine = (row >= lo_ref[i]) & (row < hi_ref[i])

        @pl.when(first_ref[i] == 1)
        def _():
            o_ref[...] = jnp.where(mine, y, jnp.uint32(0))

        @pl.when(first_ref[i] == 0)
        def _():
            o_ref[...] = jnp.where(mine, y, o_ref[...])


def _work_items(counts, n_pairs, bm):
    assert n_pairs % bm == 0
    nblocks = n_pairs // bm
    ends = jnp.cumsum(counts)
    starts = ends - counts
    first_blk = starts // bm
    n_e = jnp.where(counts > 0, (ends - 1) // bm - first_blk + 1, 0)
    item_end = jnp.cumsum(n_e)
    item_start = item_end - n_e
    n_items = item_end[-1]
    i = jnp.arange(nblocks + N_EXPERTS)
    valid = i < n_items
    e = jnp.minimum(jnp.sum(item_end[None, :] <= jnp.minimum(i, n_items - 1)[:, None], axis=1),
                    N_EXPERTS - 1)
    blk = jnp.where(valid, first_blk[e] + i - item_start[e], nblocks - 1)
    lo = jnp.clip(starts[e] - blk * bm, 0, bm)
    hi = jnp.where(valid, jnp.clip(ends[e] - blk * bm, 0, bm), 0)
    first = jnp.concatenate([jnp.ones((1,), I32), (blk[1:] != blk[:-1]).astype(I32)])
    items = tuple(a.astype(I32) for a in (blk, e, lo, hi, first, n_items[None]))
    return items, starts.astype(I32)


def _experts(items, xs, w_gate, w_up, w_down, bm):
    n, dh = xs.shape
    d = 2 * dh
    de = w_gate.shape[2]

    def rows(i, ib, ie, lo, hi, first, nit):
        return (ib[i], 0)

    def expert(i, ib, ie, lo, hi, first, nit):
        return (ie[i], 0, 0)

    grid_spec = pltpu.PrefetchScalarGridSpec(
        num_scalar_prefetch=6,
        grid=(items[0].shape[0],),
        in_specs=[
            pl.BlockSpec((bm, dh), rows),
            pl.BlockSpec((1, d, de), expert),
            pl.BlockSpec((1, d, de), expert),
            pl.BlockSpec((1, de, d), expert),
        ],
        out_specs=pl.BlockSpec((bm, dh), rows),
        scratch_shapes=[pltpu.VMEM((d, de), BF16), pltpu.VMEM((d, de), BF16),
                        pltpu.VMEM((de, d), BF16)],
    )
    return pl.pallas_call(
        _expert_kernel,
        grid_spec=grid_spec,
        out_shape=jax.ShapeDtypeStruct((n, dh), U32),
        compiler_params=_cparams("arbitrary"),
        name="experts",
    )(*items, xs, w_gate, w_up, w_down)


def _fetch_tile_indices(dest_hbm, idx_smem, isem):
    i = pl.program_id(0)
    slot = lax.rem(i, 2)
    w = idx_smem.shape[0] // 2

    def idx_copy(step, sl):
        return pltpu.make_async_copy(dest_hbm.at[pl.ds(step * w, w)], idx_smem.at[pl.ds(sl * w, w)],
                                     isem.at[sl])

    @pl.when(i == 0)
    def _():
        idx_copy(0, 0).start()

    idx_copy(i, slot).wait()

    @pl.when(i + 1 < pl.num_programs(0))
    def _():
        idx_copy(i + 1, 1 - slot).start()

    return slot


ROW_GROUP = 8


def _issue_row_copies(idx_smem, slot, tm, make_copy):
    def group(g, c):
        t0 = pl.multiple_of(g * ROW_GROUP, ROW_GROUP)
        base = slot * (TOP_K * tm) + t0
        for u in range(ROW_GROUP):
            for k in range(TOP_K):
                r = idx_smem[base + (k * tm + u)]
                make_copy(r, k, t0 + u).start(priority=k % 2)
        return c

    lax.fori_loop(0, tm // ROW_GROUP, group, 0)


def _ffn_out_kernel(dest_hbm, ys_hbm, xb_ref, x1_ref, wt_ref, wg_ref, wu_ref, wd_ref, lg_ref,
                    lb_ref, o_ref, idx_smem, ybuf, isem, rsem, *, tm):
    slot = _fetch_tile_indices(dest_hbm, idx_smem, isem)

    _issue_row_copies(
        idx_smem, slot, tm,
        lambda r, k, t: pltpu.make_async_copy(ys_hbm.at[pl.ds(r, 1)], ybuf.at[k, pl.ds(t, 1)], rsem))

    xb = xb_ref[...]
    hg = jnp.dot(xb, wg_ref[...], preferred_element_type=F32)
    hu = jnp.dot(xb, wu_ref[...], preferred_element_type=F32)
    act = (hg * jax.nn.sigmoid(hg) * hu).astype(BF16)
    sh = jnp.dot(act, wd_ref[...], preferred_element_type=F32)

    pltpu.make_async_copy(ybuf, ybuf, rsem).wait()
    wt = wt_ref[...]
    rl = None
    rr = None
    for k in range(TOP_K):
        hi, lo = _unpack_halves(ybuf[k])
        wk = wt[:, k:k + 1]
        rl = wk * hi if rl is None else rl + wk * hi
        rr = wk * lo if rr is None else rr + wk * lo
    z = ALPHA * x1_ref[...] + sh + jnp.concatenate([rl, rr], axis=1)
    o_ref[...] = _layer_norm(z, lg_ref[...], lb_ref[...])


def _ffn_out(dest, ys, x1b, x1, wt, wg, wu, wd, lg, lb, tm):
    t, d = x1.shape
    ds = wg.shape[1]
    dh = ys.shape[1]
    assert dest.shape == (t * TOP_K,)

    def rows(i):
        return (i, 0)

    def whole(i):
        return (0, 0)

    return pl.pallas_call(
        functools.partial(_ffn_out_kernel, tm=tm),
        grid=(t // tm,),
        in_specs=[
            pl.BlockSpec(memory_space=pl.ANY), pl.BlockSpec(memory_space=pl.ANY),
            pl.BlockSpec((tm, d), rows), pl.BlockSpec((tm, d), rows), pl.BlockSpec((tm, LANES), rows),
            pl.BlockSpec((d, ds), whole), pl.BlockSpec((d, ds), whole), pl.BlockSpec((ds, d), whole),
            pl.BlockSpec((1, d), whole), pl.BlockSpec((1, d), whole),
        ],
        out_specs=pl.BlockSpec((tm, d), rows),
        out_shape=jax.ShapeDtypeStruct((t, d), F32),
        scratch_shapes=[pltpu.SMEM((2 * TOP_K * tm,), I32), pltpu.VMEM((TOP_K, tm, dh), U32),
                        pltpu.SemaphoreType.DMA((2,)), pltpu.SemaphoreType.DMA(())],
        compiler_params=_cparams("arbitrary"),
        name="ffn_out_ln2",
    )(dest, ys, x1b, x1, wt, wg, wu, wd, lg, lb)


def _dispatch_kernel(dest_hbm, xp_ref, xs_hbm, idx_smem, isem, rsem, *, tm):
    slot = _fetch_tile_indices(dest_hbm, idx_smem, isem)

    _issue_row_copies(
        idx_smem, slot, tm,
        lambda r, k, t: pltpu.make_async_copy(xp_ref.at[pl.ds(t, 1)], xs_hbm.at[pl.ds(r, 1)], rsem))
    done = xs_hbm.at[pl.ds(0, TOP_K * tm)]
    pltpu.make_async_copy(done, done, rsem).wait()


def _dispatch(dest, x1p, npad, tm):
    t, dh = x1p.shape
    assert npad >= TOP_K * tm
    return pl.pallas_call(
        functools.partial(_dispatch_kernel, tm=tm),
        grid=(t // tm,),
        in_specs=[pl.BlockSpec(memory_space=pl.ANY), pl.BlockSpec((tm, dh), lambda i: (i, 0))],
        out_specs=pl.BlockSpec(memory_space=pl.ANY),
        out_shape=jax.ShapeDtypeStruct((npad, dh), U32),
        scratch_shapes=[pltpu.SMEM((2 * TOP_K * tm,), I32), pltpu.SemaphoreType.DMA((2,)),
                        pltpu.SemaphoreType.DMA(())],
        compiler_params=_cparams("arbitrary"),
        name="dispatch",
    )(dest, x1p)


def _route_kernel(xb_ref, wrt_ref, rb_ref, ek_ref, rk_ref, wt_ref, cnt_ref, upper_ref, run_ref, *,
                  tm):
    i = pl.program_id(0)
    gsz = N_EXPERTS // N_GROUP
    ninf = -jnp.inf

    @pl.when(i == 0)
    def _():
        r = lax.broadcasted_iota(I32, (tm, tm), 0)
        c = lax.broadcasted_iota(I32, (tm, tm), 1)
        upper_ref[...] = jnp.where(r < c, 1.0, 0.0).astype(BF16)
        run_ref[...] = jnp.zeros(run_ref.shape, F32)

    logits = lax.dot_general(wrt_ref[...], xb_ref[...], (((1,), (1,)), ((), ())),
                             preferred_element_type=F32)
    scores = jax.nn.sigmoid(logits)
    choice = scores + rb_ref[...]

    ridx = lax.broadcasted_iota(I32, (gsz, tm), 0)
    gscore = []
    for g in range(N_GROUP):
        blk = choice[g * gsz:(g + 1) * gsz, :]
        m1 = jnp.max(blk, axis=0, keepdims=True)
        i1 = jnp.min(jnp.where(blk == m1, ridx, gsz), axis=0, keepdims=True)
        m2 = jnp.max(jnp.where(ridx == i1, ninf, blk), axis=0, keepdims=True)
        gscore.append(m1 + m2)
    masked = []
    for g in range(N_GROUP):
        beaten = jnp.zeros((1, tm), I32)
        for g2 in range(N_GROUP):
            if g2 == g:
                continue
            wins = (gscore[g2] >= gscore[g]) if g2 < g else (gscore[g2] > gscore[g])
            beaten = beaten + jnp.where(wins, 1, 0)
        masked.append(jnp.where(beaten < TOPK_GROUP, choice[g * gsz:(g + 1) * gsz, :], ninf))
    v = jnp.concatenate(masked, axis=0)

    eidx = lax.broadcasted_iota(I32, (N_EXPERTS, tm), 0)
    sel = jnp.zeros((N_EXPERTS, tm), F32)
    e_rows = []
    s_rows = []
    for k in range(TOP_K):
        m = jnp.max(v, axis=0, keepdims=True)
        ik = jnp.min(jnp.where(v == m, eidx, N_EXPERTS), axis=0, keepdims=True)
        hit = eidx == ik
        e_rows.append(ik)
        s_rows.append(jnp.sum(jnp.where(hit, scores, 0.0), axis=0, keepdims=True))
        v = jnp.where(hit, ninf, v)
        sel = jnp.where(hit, 1.0, sel)

    prefix = jnp.dot(sel.astype(BF16), upper_ref[...], preferred_element_type=F32)
    pos = prefix + run_ref[...]
    for k in range(TOP_K):
        rk = jnp.sum(jnp.where(eidx == e_rows[k], pos, 0.0), axis=0, keepdims=True)
        ek_ref[k:k + 1, :] = e_rows[k]
        rk_ref[k:k + 1, :] = rk.astype(I32)
    run_ref[...] = run_ref[...] + jnp.sum(sel, axis=1, keepdims=True)
    cnt_ref[...] = jnp.broadcast_to(run_ref[...], cnt_ref.shape).astype(I32)

    ssum = s_rows[0]
    for k in range(1, TOP_K):
        ssum = ssum + s_rows[k]
    w_rows = [s / (ssum + 1e-20) * ROUTED_SCALE for s in s_rows]
    w_rows.append(jnp.zeros((LANES - TOP_K, tm), F32))
    wt_ref[...] = jnp.concatenate(w_rows, axis=0).T


def _route(x1b, wrt, rbias, tm):
    t, d = x1b.shape
    return pl.pallas_call(
        functools.partial(_route_kernel, tm=tm),
        grid=(t // tm,),
        in_specs=[pl.BlockSpec((tm, d), lambda i: (i, 0)),
                  pl.BlockSpec((N_EXPERTS, d), lambda i: (0, 0)),
                  pl.BlockSpec((N_EXPERTS, 1), lambda i: (0, 0))],
        out_specs=[pl.BlockSpec((TOP_K, tm), lambda i: (0, i)),
                   pl.BlockSpec((TOP_K, tm), lambda i: (0, i)),
                   pl.BlockSpec((tm, LANES), lambda i: (i, 0)),
                   pl.BlockSpec((N_EXPERTS, LANES), lambda i: (0, 0))],
        out_shape=[jax.ShapeDtypeStruct((TOP_K, t), I32), jax.ShapeDtypeStruct((TOP_K, t), I32),
                   jax.ShapeDtypeStruct((t, LANES), F32),
                   jax.ShapeDtypeStruct((N_EXPERTS, LANES), I32)],
        scratch_shapes=[pltpu.VMEM((tm, tm), BF16), pltpu.VMEM((N_EXPERTS, 1), F32)],
        compiler_params=_cparams("arbitrary"),
        name="route",
    )(x1b, wrt, rbias)


def _dest_kernel(ps_ref, ek_ref, rk_ref, d_ref, *, tm):
    e = ek_ref[...]

    def body(j, acc):
        return acc + jnp.where(e == j, ps_ref[j], 0)

    res = lax.fori_loop(0, N_EXPERTS, body, jnp.zeros(e.shape, I32)) + rk_ref[...]
    for a in range(e.shape[1] // tm):
        for k in range(TOP_K):
            d_ref[a:a + 1, k * tm:(k + 1) * tm] = res[k:k + 1, a * tm:(a + 1) * tm]


def _dest(pstart, ek, rk, tm, tw):
    t = ek.shape[1]
    grid_spec = pltpu.PrefetchScalarGridSpec(
        num_scalar_prefetch=1,
        grid=(t // tw,),
        in_specs=[pl.BlockSpec((TOP_K, tw), lambda i, ps: (0, i)),
                  pl.BlockSpec((TOP_K, tw), lambda i, ps: (0, i))],
        out_specs=pl.BlockSpec((tw // tm, TOP_K * tm), lambda i, ps: (i, 0)),
    )
    return pl.pallas_call(
        functools.partial(_dest_kernel, tm=tm),
        grid_spec=grid_spec,
        out_shape=jax.ShapeDtypeStruct((t // tm, TOP_K * tm), I32),
        compiler_params=_cparams("parallel"),
        name="dest",
    )(pstart, ek, rk)


def _layer(x, mem, rel_bias, w_in, b_in, conv_w, diff_lambda, subln_g, mlstm_norm_g, w_mem_kv,
           w_branch, w_out, ln1_g, ln1_b, w_router, router_bias, w_e_gate, w_e_up, w_e_down,
           w_s_gate, w_s_up, w_s_down, ln2_g, ln2_b, layer_idx, cfg):
    b, s, d = x.shape
    t = b * s
    x2 = x.reshape(t, d)

    g0 = (OFF_MLO + 8) * LANES
    w_main = jnp.concatenate([w_in[:, :g0], w_in[:, g0 + 2 * ML_HEADS:]], axis=1).astype(BF16)
    b_main = jnp.concatenate([b_in[:g0], b_in[g0 + 2 * ML_HEADS:]])[None, :]
    w_g = jnp.pad(w_in[:, g0:g0 + 2 * ML_HEADS], ((0, 0), (0, LANES - 2 * ML_HEADS))).astype(BF16)
    b_g = jnp.pad(b_in[g0:g0 + 2 * ML_HEADS], (0, LANES - 2 * ML_HEADS))[None, :]

    u2, gates2 = _proj_in(x2, w_main, b_main, w_g, b_g, cfg["proj_tm"], cfg["proj_tn"])
    u3 = u2.reshape(b, s, N_MAIN)
    gates3 = gates2.reshape(b, s, LANES)

    lam_init = 0.8 - 0.6 * math.exp(-0.3 * layer_idx)
    y_a = _diff_attention(u3, rel_bias, diff_lambda, subln_g[None, :], cfg["attn_tq"], lam_init)
    y_m = _mlstm(u3, gates3, conv_w, mlstm_norm_g[None, :], cfg["ml_chunk"])
    kv = _mm(mem.reshape(-1, d), w_mem_kv.astype(BF16), BF16, cfg["kv_tm"], "mem_kv")
    y_c = _mem_attention(u3, kv.reshape(b, -1, 2 * MA_HEADS * MA_HEAD_DIM), cfg["ma_tq"])

    x1, x1b, x1p = _merge(y_a.reshape(t, d), y_m.reshape(t, d), y_c.reshape(t, d), u2, x2,
                          w_branch.astype(BF16), w_out.astype(BF16), ln1_g[None, :],
                          ln1_b[None, :], cfg["merge_tm"])

    ek, rk, wt, cnt = _route(x1b, w_router.T.astype(BF16), router_bias.astype(F32)[:, None],
                             cfg["route_tm"])
    bm = cfg["expert_bm"]
    items, starts = _work_items(cnt[:, 0], t * TOP_K, bm)
    dest = _dest(starts, ek, rk, cfg["moe_tm"], cfg["dest_tw"]).reshape(-1)
    xs = _dispatch(dest, x1p, t * TOP_K, cfg["moe_tm"])
    ys = _experts(items, xs, w_e_gate, w_e_up, w_e_down, bm)
    out = _ffn_out(dest, ys, x1b, x1, wt, w_s_gate.astype(BF16), w_s_up.astype(BF16),
                   w_s_down.astype(BF16), ln2_g[None, :], ln2_b[None, :], cfg["moe_tm"])
    return out.reshape(b, s, d)


def _config(b, s):
    t = b * s
    return {
        "proj_tm": min(1024, t), "proj_tn": 1024,
        "attn_tq": 256, "ml_chunk": 256, "kv_tm": 512, "ma_tq": min(512, s),
        "merge_tm": 256, "route_tm": 512, "expert_bm": 512, "moe_tm": 256, "dest_tw": 2048,
    }


def kernel(x, mem, rel_bias, w_in, b_in, conv_w, diff_lambda, subln_g, mlstm_norm_g, w_mem_kv,
           w_branch, w_out, ln1_g, ln1_b, w_router, router_bias, w_e_gate, w_e_up, w_e_down,
           w_s_gate, w_s_up, w_s_down, ln2_g, ln2_b):
    cfg = _config(x.shape[0], x.shape[1])
    for l in range(DEPTH):
        x = _layer(x, mem, rel_bias, w_in[l], b_in[l], conv_w[l], diff_lambda[l], subln_g[l],
                   mlstm_norm_g[l], w_mem_kv[l], w_branch[l], w_out[l], ln1_g[l], ln1_b[l],
                   w_router[l], router_bias[l], w_e_gate[l], w_e_up[l], w_e_down[l],
                   w_s_gate[l], w_s_up[l], w_s_down[l], ln2_g[l], ln2_b[l], l, cfg)
    return x
```

```python
import functools
import math

import numpy as np
import jax
import jax.numpy as jnp
from jax import lax
from jax.experimental import pallas as pl
from jax.experimental.pallas import tpu as pltpu
from jax.experimental.pallas import tpu_sc as plsc

F32 = jnp.float32
BF16 = jnp.bfloat16
U32 = jnp.uint32
I32 = jnp.int32

D_MODEL = 1024
DEPTH = 1
DA_HEAD_DIM = 64
DA_V_DIM = 128
DA_HEADS = 8
ML_HEADS = 4
ML_HEAD_DIM = 256
ML_CONV = 4
MA_HEADS = 4
MA_HEAD_DIM = 256
N_BRANCH = 3
REL_BUCKETS = 32
REL_MAX_DIST = 128
N_EXPERTS = 256
TOP_K = 8
N_GROUP = 8
TOPK_GROUP = 4
D_EXPERT = 256
ROUTED_SCALE = 2.5
ALPHA = (2.0 * DEPTH) ** 0.25

LANES = 128
NEG = -1e30
LOG2E = math.log2(math.e)
FOLD_ACCS = 4
VMEM_LIMIT = 56 * 1024 * 1024

OFF_DAQ, OFF_DAK, OFF_DAV = 0, 8, 16
OFF_MLQ, OFF_MLK, OFF_MLV, OFF_MLO, OFF_MAQ, OFF_GATE = 24, 32, 40, 48, 56, 64
N_MAIN = 88 * LANES


def _cparams(*sem):
    return pltpu.CompilerParams(dimension_semantics=sem, vmem_limit_bytes=VMEM_LIMIT)


def _layer_norm(z, g, b):
    mu = jnp.mean(z, axis=-1, keepdims=True)
    zc = z - mu
    var = jnp.mean(zc * zc, axis=-1, keepdims=True)
    return zc * lax.rsqrt(var + 1e-5) * g + b


def _proj_in_kernel(x_ref, w_ref, b_ref, wg_ref, bg_ref, u_ref, g_ref, xs_ref):
    @pl.when(pl.program_id(1) == 0)
    def _():
        xb = x_ref[...].astype(BF16)
        xs_ref[...] = xb
        g_ref[...] = jnp.dot(xb, wg_ref[...], preferred_element_type=F32) + bg_ref[...]

    acc = jnp.dot(xs_ref[...], w_ref[...], preferred_element_type=F32)
    u_ref[...] = (acc + b_ref[...]).astype(u_ref.dtype)


def _proj_in(x2, w_main, b_main, w_g, b_g, tm, tn):
    t, k = x2.shape
    n = w_main.shape[1]
    return pl.pallas_call(
        _proj_in_kernel,
        grid=(t // tm, n // tn),
        in_specs=[
            pl.BlockSpec((tm, k), lambda i, j: (i, 0)),
            pl.BlockSpec((k, tn), lambda i, j: (0, j)),
            pl.BlockSpec((1, tn), lambda i, j: (0, j)),
            pl.BlockSpec((k, LANES), lambda i, j: (0, 0)),
            pl.BlockSpec((1, LANES), lambda i, j: (0, 0)),
        ],
        out_specs=[
            pl.BlockSpec((tm, tn), lambda i, j: (i, j)),
            pl.BlockSpec((tm, LANES), lambda i, j: (i, 0)),
        ],
        out_shape=[jax.ShapeDtypeStruct((t, n), BF16), jax.ShapeDtypeStruct((t, LANES), F32)],
        scratch_shapes=[pltpu.VMEM((tm, k), BF16)],
        compiler_params=_cparams("parallel", "arbitrary"),
        name="proj_in",
    )(x2, w_main, b_main, w_g, b_g)


def _mm_kernel(x_ref, w_ref, o_ref):
    o_ref[...] = jnp.dot(x_ref[...].astype(BF16), w_ref[...],
                         preferred_element_type=F32).astype(o_ref.dtype)


def _mm(x2, w, out_dtype, tm, name):
    m, k = x2.shape
    n = w.shape[1]
    return pl.pallas_call(
        _mm_kernel,
        grid=(m // tm,),
        in_specs=[pl.BlockSpec((tm, k), lambda i: (i, 0)), pl.BlockSpec((k, n), lambda i: (0, 0))],
        out_specs=pl.BlockSpec((tm, n), lambda i: (i, 0)),
        out_shape=jax.ShapeDtypeStruct((m, n), out_dtype),
        compiler_params=_cparams("parallel"),
        name=name,
    )(x2, w)


def _fold8(x, op):
    n = x.shape[0] // 8
    accs = [x[8 * a:8 * a + 8, :] for a in range(min(FOLD_ACCS, n))]
    for a in range(FOLD_ACCS, n):
        accs[a % FOLD_ACCS] = op(accs[a % FOLD_ACCS], x[8 * a:8 * a + 8, :])
    while len(accs) > 1:
        accs = [op(accs[a], accs[a + 1]) for a in range(0, len(accs), 2)]
    return accs[0]


def _dattn_kernel(rb_ref, q_ref, k_ref, v_ref, bkt_ref, dl_ref, g_ref, o_ref,
                  st_ref, pt_ref, vt_ref, nb_ref, *, tq, seq, lam_init):
    h = pl.program_id(0)
    nq = seq // tq
    nt = (((1,), (1,)), ((), ()))

    @pl.when(pl.program_id(1) == 0)
    def _():
        far = rb_ref[REL_BUCKETS - 1, h]
        for t in range(2):
            bk = bkt_ref[t]
            tile = jnp.full((tq, tq), NEG, F32)
            for bb in range(REL_BUCKETS):
                tile = jnp.where(bk == bb, (rb_ref[bb, h] - far) * LOG2E, tile)
            nb_ref[t * tq:(t + 1) * tq, :] = tile

    for j in range(nq):
        vt_ref[:, j * tq:(j + 1) * tq] = v_ref[0, j * tq:(j + 1) * tq, :].astype(F32).T.astype(BF16)

    dl = dl_ref[...]
    lam = (jnp.exp(jnp.sum(dl[0:1] * dl[1:2], axis=1, keepdims=True))
           - jnp.exp(jnp.sum(dl[2:3] * dl[3:4], axis=1, keepdims=True)) + lam_init)
    lane = lax.broadcasted_iota(I32, (tq, LANES), 1)
    gain = g_ref[...] * (1.0 - lam_init)

    def logits_pass(i):
        qs = q_ref[0, i * tq:(i + 1) * tq, :] * jnp.asarray(DA_HEAD_DIM ** -0.5 * LOG2E, BF16)
        zero = jnp.zeros_like(qs)
        qm = (jnp.where(lane < DA_HEAD_DIM, qs, zero), jnp.where(lane >= DA_HEAD_DIM, qs, zero))
        mx = [None, None]

        def chunk(c):
            for half in range(2):
                s = lax.dot_general(k_ref[0, c * tq:(c + 1) * tq, :], qm[half], nt,
                                    preferred_element_type=F32)
                if c >= i - 1:
                    s = s + nb_ref[(c - i + 1) * tq:(c - i + 2) * tq, :]
                st_ref[2 * (i % 2) + half, c * tq:(c + 1) * tq, :] = s
                f = _fold8(s, jnp.maximum)
                mx[half] = f if mx[half] is None else jnp.maximum(mx[half], f)

        def result():
            return [jnp.max(mx[half], axis=0, keepdims=True) for half in range(2)]

        return [functools.partial(chunk, c) for c in range(i + 1)], result

    def exp_pass(i, m):
        l8 = [None, None]

        def chunk(c):
            for half in range(2):
                p = jnp.exp2(st_ref[2 * (i % 2) + half, c * tq:(c + 1) * tq, :] - m[half])
                pt_ref[2 * (i % 2) + half, c * tq:(c + 1) * tq, :] = p.astype(BF16)
                f = _fold8(p, jnp.add)
                l8[half] = f if l8[half] is None else l8[half] + f

        def result():
            return [jnp.sum(l8[half], axis=0, keepdims=True) for half in range(2)]

        return [functools.partial(chunk, c) for c in range(i + 1)], result

    def value_pass(i, l1, l2):
        kv = (i + 1) * tq
        acc = [jnp.dot(vt_ref[:, 0:kv], pt_ref[2 * (i % 2) + half, 0:kv, :],
                       preferred_element_type=F32) for half in range(2)]
        o = (acc[0] * (1.0 / l1) - acc[1] * (lam / l2)).T
        ms = jnp.mean(o * o, axis=1, keepdims=True)
        y = o * lax.rsqrt(ms + 1e-5) * gain
        o_ref[0, i * tq:(i + 1) * tq, :] = y.astype(o_ref.dtype)

    chunks, result = logits_pass(0)
    for run in chunks:
        run()
    m_next = result()
    for i in range(nq):
        exp_chunks, exp_result = exp_pass(i, m_next)
        next_chunks, next_result = logits_pass(i + 1) if i + 1 < nq else ([], None)
        for c in range(max(len(exp_chunks), len(next_chunks))):
            if c < len(next_chunks):
                next_chunks[c]()
            if c < len(exp_chunks):
                exp_chunks[c]()
        if next_result is not None:
            m_next = next_result()
        value_pass(i, *exp_result())


def _t5_bucket_np(dist):
    n = np.maximum(dist, 0)
    max_exact = REL_BUCKETS // 2
    large = max_exact + (np.log(np.maximum(n, 1).astype(np.float32) / max_exact)
                         / math.log(REL_MAX_DIST / max_exact)
                         * (REL_BUCKETS - max_exact)).astype(np.int32)
    large = np.minimum(large, REL_BUCKETS - 1)
    return np.where(n < max_exact, n, large)


def _near_buckets(tq):
    assert _t5_bucket_np(np.array([tq + 1]))[0] == REL_BUCKETS - 1
    c = np.arange(tq)[:, None]
    r = np.arange(tq)[None, :]
    left = _t5_bucket_np(r + tq - c)
    diag = np.where(r >= c, _t5_bucket_np(r - c), -1)
    return np.stack([left, diag], axis=0).astype(np.int32)


def _diff_attention(u3, rel_bias, diff_lambda, subln_g, tq, lam_init):
    b, s, _ = u3.shape
    kern = functools.partial(_dattn_kernel, tq=tq, seq=s, lam_init=lam_init)
    grid_spec = pltpu.PrefetchScalarGridSpec(
        num_scalar_prefetch=1,
        grid=(DA_HEADS, b),
        in_specs=[
            pl.BlockSpec((1, s, LANES), lambda h, bi, rb: (bi, 0, OFF_DAQ + h)),
            pl.BlockSpec((1, s, LANES), lambda h, bi, rb: (bi, 0, OFF_DAK + h)),
            pl.BlockSpec((1, s, LANES), lambda h, bi, rb: (bi, 0, OFF_DAV + h)),
            pl.BlockSpec((2, tq, tq), lambda h, bi, rb: (0, 0, 0)),
            pl.BlockSpec((4, DA_HEAD_DIM), lambda h, bi, rb: (0, 0)),
            pl.BlockSpec((1, DA_V_DIM), lambda h, bi, rb: (0, 0)),
        ],
        out_specs=pl.BlockSpec((1, s, LANES), lambda h, bi, rb: (bi, 0, h)),
        scratch_shapes=[pltpu.VMEM((4, s, tq), F32), pltpu.VMEM((4, s, tq), BF16),
                        pltpu.VMEM((DA_V_DIM, s), BF16),
                        pltpu.VMEM((2 * tq, tq), F32)],
    )
    return pl.pallas_call(
        kern,
        grid_spec=grid_spec,
        out_shape=jax.ShapeDtypeStruct((b, s, DA_HEADS * DA_V_DIM), BF16),
        compiler_params=_cparams("arbitrary", "arbitrary"),
        name="diff_attn",
    )(rel_bias.astype(F32), u3, u3, u3, jnp.asarray(_near_buckets(tq)), diff_lambda, subln_g)


def _mlstm_kernel(q_ref, k_ref, v_ref, og_ref, g_ref, cwq_ref, cwk_ref, ng_ref, y_ref,
                  ct_ref, gts_ref, *, chunk, seq):
    h = pl.program_id(1)
    nc = seq // chunk
    hd = ML_HEAD_DIM
    lane = lax.broadcasted_iota(jnp.int32, (chunk, LANES), 1)
    row = lax.broadcasted_iota(jnp.int32, (chunk, chunk), 0)
    col = lax.broadcasted_iota(jnp.int32, (chunk, chunk), 1)
    causal = row >= col
    tri = causal.astype(F32)
    cwq = cwq_ref[...]
    cwk = cwk_ref[...]
    ng = ng_ref[...]
    ct_ref[...] = jnp.zeros(ct_ref.shape, F32)

    def conv_silu(ref, cw, r0, c):
        cur = ref[0, pl.ds(r0, chunk), :].astype(F32)
        p0 = pl.multiple_of(jnp.maximum(r0 - 16, 0), 16)
        prev = ref[0, pl.ds(p0, 16), :].astype(F32)
        prev = jnp.where(c > 0, prev, 0.0)
        x = jnp.concatenate([prev, cur], axis=0)
        out = cw[ML_CONV - 1:ML_CONV] * cur
        for tap in range(ML_CONV - 1):
            shifted = pltpu.roll(x, ML_CONV - 1 - tap, 0)[16:]
            out = out + cw[tap:tap + 1] * shifted
        return out * jax.nn.sigmoid(out)

    def body(c, carry):
        m, n = carry
        r0 = pl.multiple_of(c * chunk, chunk)
        g = g_ref[0, pl.ds(r0, chunk), :]
        logf = jnp.minimum(g, 0.0) - jnp.log(1.0 + jnp.exp(-jnp.abs(g)))
        bc = jnp.dot(tri, logf, precision=lax.Precision.HIGHEST, preferred_element_type=F32)
        gts_ref[...] = jnp.where(lane < ML_HEADS, g, bc).T
        irow = gts_ref[pl.ds(h, 1), :]
        brow = gts_ref[pl.ds(ML_HEADS + h, 1), :]
        bcol = jnp.sum(jnp.where(lane == ML_HEADS + h, bc, 0.0), axis=1, keepdims=True)
        icol = jnp.sum(jnp.where(lane == h, g, 0.0), axis=1, keepdims=True)

        dmat = jnp.where(causal, bcol - brow + irow, NEG)
        inter = bcol + m
        m_row = jnp.maximum(inter, jnp.max(dmat, axis=1, keepdims=True))

        q = conv_silu(q_ref, cwq, r0, c)
        k = conv_silu(k_ref, cwk, r0, c) * (hd ** -0.5)
        qb = q.astype(BF16)
        kb = k.astype(BF16)
        vb = v_ref[0, pl.ds(r0, chunk), :]
        sqk = lax.dot_general(qb, kb, (((1,), (1,)), ((), ())), preferred_element_type=F32)
        w = jnp.exp(dmat - m_row) * sqk
        inter_w = jnp.exp(inter - m_row)
        ct = ct_ref[...]
        num = (inter_w * jnp.dot(qb, ct.astype(BF16), preferred_element_type=F32)
               + jnp.dot(w.astype(BF16), vb, preferred_element_type=F32))
        den = inter_w * jnp.sum(q * n, axis=1, keepdims=True) + jnp.sum(w, axis=1, keepdims=True)
        hout = num / jnp.maximum(jnp.abs(den), jnp.exp(-m_row))
        mu = jnp.mean(hout, axis=1, keepdims=True)
        hc = hout - mu
        var = jnp.mean(hc * hc, axis=1, keepdims=True)
        hn = hc * lax.rsqrt(var + 1e-5) * ng
        og = og_ref[0, pl.ds(r0, chunk), :].astype(F32)
        y_ref[0, pl.ds(r0, chunk), :] = (jax.nn.sigmoid(og) * hn).astype(y_ref.dtype)

        total = brow[:, chunk - 1:chunk]
        grow = total - brow + irow
        m_new = jnp.maximum(total + m, jnp.max(grow, axis=1, keepdims=True))
        decay = jnp.exp(total + m - m_new)
        ws = jnp.exp(total - bcol + icol - m_new)
        wsv = (ws * vb.astype(F32)).astype(BF16)
        ct_ref[...] = decay * ct + jnp.dot(k.T.astype(BF16), wsv, preferred_element_type=F32)
        n_new = decay * n + jnp.sum(ws * k, axis=0, keepdims=True)
        return m_new, n_new

    lax.fori_loop(0, nc, body, (jnp.zeros((1, 1), F32), jnp.zeros((1, hd), F32)))


def _mlstm(u3, gates3, conv_w, norm_g, chunk):
    b, s, _ = u3.shape
    hd = ML_HEAD_DIM
    nq = hd // LANES
    kern = functools.partial(_mlstm_kernel, chunk=chunk, seq=s)

    def ublock(off):
        return pl.BlockSpec((1, s, hd), lambda bi, h: (bi, 0, off // nq + h))

    return pl.pallas_call(
        kern,
        grid=(b, ML_HEADS),
        in_specs=[
            ublock(OFF_MLQ), ublock(OFF_MLK), ublock(OFF_MLV), ublock(OFF_MLO),
            pl.BlockSpec((1, s, LANES), lambda bi, h: (bi, 0, 0)),
            pl.BlockSpec((ML_CONV, hd), lambda bi, h: (0, h)),
            pl.BlockSpec((ML_CONV, hd), lambda bi, h: (0, ML_HEADS + h)),
            pl.BlockSpec((1, hd), lambda bi, h: (0, h)),
        ],
        out_specs=pl.BlockSpec((1, s, hd), lambda bi, h: (bi, 0, h)),
        out_shape=jax.ShapeDtypeStruct((b, s, ML_HEADS * hd), BF16),
        scratch_shapes=[pltpu.VMEM((hd, hd), F32), pltpu.VMEM((LANES, chunk), F32)],
        compiler_params=_cparams("parallel", "arbitrary"),
        name="mlstm",
    )(u3, u3, u3, u3, gates3, conv_w, conv_w, norm_g)


def _memattn_kernel(q_ref, k_ref, v_ref, o_ref, *, tq, seq):
    kb = k_ref[0]
    vb = v_ref[0]
    scale = jnp.asarray(MA_HEAD_DIM ** -0.5, BF16)
    for t in range(seq // tq):
        q = q_ref[0, t * tq:(t + 1) * tq, :] * scale
        s = lax.dot_general(q, kb, (((1,), (1,)), ((), ())), preferred_element_type=F32)
        p = jnp.exp(s - jnp.max(s, axis=1, keepdims=True))
        inv = 1.0 / jnp.sum(p, axis=1, keepdims=True)
        o = jnp.dot(p.astype(BF16), vb, preferred_element_type=F32) * inv
        o_ref[0, t * tq:(t + 1) * tq, :] = o.astype(o_ref.dtype)


def _mem_attention(u3, kv3, tq):
    b, s, _ = u3.shape
    mlen = kv3.shape[1]
    hd = MA_HEAD_DIM
    nq = hd // LANES
    kern = functools.partial(_memattn_kernel, tq=tq, seq=s)
    return pl.pallas_call(
        kern,
        grid=(b, MA_HEADS),
        in_specs=[
            pl.BlockSpec((1, s, hd), lambda bi, h: (bi, 0, OFF_MAQ // nq + h)),
            pl.BlockSpec((1, mlen, hd), lambda bi, h: (bi, 0, h)),
            pl.BlockSpec((1, mlen, hd), lambda bi, h: (bi, 0, MA_HEADS + h)),
        ],
        out_specs=pl.BlockSpec((1, s, hd), lambda bi, h: (bi, 0, h)),
        out_shape=jax.ShapeDtypeStruct((b, s, MA_HEADS * hd), BF16),
        compiler_params=_cparams("parallel", "parallel"),
        name="mem_attn",
    )(u3, kv3, kv3)


def _merge_kernel(ya_ref, ym_ref, yc_ref, g0_ref, g1_ref, g2_ref, x_ref, wb_ref, wo_ref,
                  lg_ref, lb_ref, x1_ref, x1b_ref, x1p_ref):
    acc = None
    for n, (y_ref, g_ref) in enumerate(((ya_ref, g0_ref), (ym_ref, g1_ref), (yc_ref, g2_ref))):
        pr = jnp.dot(y_ref[...], wb_ref[n], preferred_element_type=F32)
        t = jax.nn.sigmoid(g_ref[...].astype(F32)) * pr
        acc = t if acc is None else acc + t
    out = jnp.dot(acc.astype(BF16), wo_ref[...], preferred_element_type=F32)
    x1 = _layer_norm(ALPHA * x_ref[...] + out, lg_ref[...], lb_ref[...])
    x1_ref[...] = x1
    x1b = x1.astype(BF16)
    x1b_ref[...] = x1b
    x1p_ref[...] = _pack_halves(x1b)


def _pack_halves(vb):
    w = vb.shape[1] // 2
    hi = lax.bitcast_convert_type(vb[:, :w].astype(F32), U32)
    lo = lax.bitcast_convert_type(vb[:, w:].astype(F32), U32)
    return hi | (lo >> 16)


def _unpack_halves(u):
    hi = lax.bitcast_convert_type(u & jnp.uint32(0xFFFF0000), F32)
    lo = lax.bitcast_convert_type(u << 16, F32)
    return hi, lo


def _merge(ya, ym, yc, u2, x2, wb, wo, lg, lb, tm):
    t, d = x2.shape
    gb = OFF_GATE * LANES // d

    def rows(i):
        return (i, 0)

    return pl.pallas_call(
        _merge_kernel,
        grid=(t // tm,),
        in_specs=[
            pl.BlockSpec((tm, d), rows), pl.BlockSpec((tm, d), rows), pl.BlockSpec((tm, d), rows),
            pl.BlockSpec((tm, d), lambda i: (i, gb)),
            pl.BlockSpec((tm, d), lambda i: (i, gb + 1)),
            pl.BlockSpec((tm, d), lambda i: (i, gb + 2)),
            pl.BlockSpec((tm, d), rows),
            pl.BlockSpec((N_BRANCH, d, d), lambda i: (0, 0, 0)),
            pl.BlockSpec((d, d), lambda i: (0, 0)),
            pl.BlockSpec((1, d), lambda i: (0, 0)),
            pl.BlockSpec((1, d), lambda i: (0, 0)),
        ],
        out_specs=[pl.BlockSpec((tm, d), rows), pl.BlockSpec((tm, d), rows),
                   pl.BlockSpec((tm, d // 2), rows)],
        out_shape=[jax.ShapeDtypeStruct((t, d), F32), jax.ShapeDtypeStruct((t, d), BF16),
                   jax.ShapeDtypeStruct((t, d // 2), U32)],
        compiler_params=_cparams("parallel"),
        name="merge_ln1",
    )(ya, ym, yc, u2, u2, u2, x2, wb, wo, lg, lb)


def _expert_kernel(ib_ref, ie_ref, lo_ref, hi_ref, first_ref, nit_ref, x_ref, wg_ref, wu_ref,
                   wd_ref, o_ref, wgb_ref, wub_ref, wdb_ref):
    i = pl.program_id(0)

    @pl.when((i < nit_ref[0]) & ((i == 0) | (ie_ref[i] != ie_ref[jnp.maximum(i - 1, 0)])))
    def _():
        wgb_ref[...] = wg_ref[0].astype(BF16)
        wub_ref[...] = wu_ref[0].astype(BF16)
        wdb_ref[...] = wd_ref[0].astype(BF16)

    @pl.when(i < nit_ref[0])
    def _():
        xl, xr = _unpack_halves(x_ref[...])
        xl = xl.astype(BF16)
        xr = xr.astype(BF16)
        half = xl.shape[1]

        def up(w_ref):
            return (jnp.dot(xl, w_ref[:half, :], preferred_element_type=F32)
                    + jnp.dot(xr, w_ref[half:, :], preferred_element_type=F32))

        hg = up(wgb_ref)
        hu = up(wub_ref)
        act = (hg * jax.nn.sigmoid(hg) * hu).astype(BF16)
        y = _pack_halves(jnp.dot(act, wdb_ref[...], preferred_element_type=F32).astype(BF16))
        row = lax.broadcasted_iota(I32, y.shape, 0)
        mine = (row >= lo_ref[i]) & (row < hi_ref[i])

        @pl.when(first_ref[i] == 1)
        def _():
            o_ref[...] = jnp.where(mine, y, jnp.uint32(0))

        @pl.when(first_ref[i] == 0)
        def _():
            o_ref[...] = jnp.where(mine, y, o_ref[...])


def _work_items(counts, n_pairs, bm):
    assert n_pairs % bm == 0
    nblocks = n_pairs // bm
    ends = jnp.cumsum(counts)
    starts = ends - counts
    first_blk = starts // bm
    n_e = jnp.where(counts > 0, (ends - 1) // bm - first_blk + 1, 0)
    item_end = jnp.cumsum(n_e)
    item_start = item_end - n_e
    n_items = item_end[-1]
    i = jnp.arange(nblocks + N_EXPERTS)
    valid = i < n_items
    e = jnp.minimum(jnp.sum(item_end[None, :] <= jnp.minimum(i, n_items - 1)[:, None], axis=1),
                    N_EXPERTS - 1)
    blk = jnp.where(valid, first_blk[e] + i - item_start[e], nblocks - 1)
    lo = jnp.clip(starts[e] - blk * bm, 0, bm)
    hi = jnp.where(valid, jnp.clip(ends[e] - blk * bm, 0, bm), 0)
    first = jnp.concatenate([jnp.ones((1,), I32), (blk[1:] != blk[:-1]).astype(I32)])
    items = tuple(a.astype(I32) for a in (blk, e, lo, hi, first, n_items[None]))
    return items, starts.astype(I32)


def _experts(items, xs, w_gate, w_up, w_down, bm):
    n, dh = xs.shape
    d = 2 * dh
    de = w_gate.shape[2]

    def rows(i, ib, ie, lo, hi, first, nit):
        return (ib[i], 0)

    def expert(i, ib, ie, lo, hi, first, nit):
        return (ie[i], 0, 0)

    grid_spec = pltpu.PrefetchScalarGridSpec(
        num_scalar_prefetch=6,
        grid=(items[0].shape[0],),
        in_specs=[
            pl.BlockSpec((bm, dh), rows),
            pl.BlockSpec((1, d, de), expert),
            pl.BlockSpec((1, d, de), expert),
            pl.BlockSpec((1, de, d), expert),
        ],
        out_specs=pl.BlockSpec((bm, dh), rows),
        scratch_shapes=[pltpu.VMEM((d, de), BF16), pltpu.VMEM((d, de), BF16),
                        pltpu.VMEM((de, d), BF16)],
    )
    return pl.pallas_call(
        _expert_kernel,
        grid_spec=grid_spec,
        out_shape=jax.ShapeDtypeStruct((n, dh), U32),
        compiler_params=_cparams("arbitrary"),
        name="experts",
    )(*items, xs, w_gate, w_up, w_down)


def _fetch_tile_indices(dest_hbm, idx_smem, isem):
    i = pl.program_id(0)
    slot = lax.rem(i, 2)
    w = idx_smem.shape[0] // 2

    def idx_copy(step, sl):
        return pltpu.make_async_copy(dest_hbm.at[pl.ds(step * w, w)], idx_smem.at[pl.ds(sl * w, w)],
                                     isem.at[sl])

    @pl.when(i == 0)
    def _():
        idx_copy(0, 0).start()

    idx_copy(i, slot).wait()

    @pl.when(i + 1 < pl.num_programs(0))
    def _():
        idx_copy(i + 1, 1 - slot).start()

    return slot


ROW_GROUP = 8


def _issue_row_copies(idx_smem, slot, tm, make_copy):
    def group(g, c):
        t0 = pl.multiple_of(g * ROW_GROUP, ROW_GROUP)
        base = slot * (TOP_K * tm) + t0
        for u in range(ROW_GROUP):
            for k in range(TOP_K):
                r = idx_smem[base + (k * tm + u)]
                make_copy(r, k, t0 + u).start(priority=k % 2)
        return c

    lax.fori_loop(0, tm // ROW_GROUP, group, 0)


SC_WINDOW = 128
SC_WORKERS = 32


def _sc_gather_rows(src, idx):
    n = idx.shape[0]
    dh = src.shape[1]
    per = n // SC_WORKERS
    assert per % SC_WINDOW == 0
    mesh = plsc.VectorSubcoreMesh(core_axis_name="core", subcore_axis_name="subcore")

    @pl.kernel(out_type=jax.ShapeDtypeStruct((n, dh), src.dtype), mesh=mesh,
               scratch_types=[pltpu.VMEM((SC_WINDOW,), I32), pltpu.VMEM((SC_WINDOW, dh), src.dtype)])
    def gather(src_hbm, idx_hbm, out_hbm, idx_vmem, rows_vmem):
        worker = lax.axis_index("core") * (SC_WORKERS // 2) + lax.axis_index("subcore")

        @pl.loop(0, per // SC_WINDOW)
        def _(j):
            base = worker * per + j * SC_WINDOW
            pltpu.sync_copy(idx_hbm.at[pl.ds(base, SC_WINDOW)], idx_vmem)
            pltpu.sync_copy(src_hbm.at[idx_vmem], rows_vmem)
            pltpu.sync_copy(rows_vmem, out_hbm.at[pl.ds(base, SC_WINDOW)])

    return gather(src, idx)


def _ffn_out_kernel(yt_ref, xb_ref, x1_ref, wt_ref, wg_ref, wu_ref, wd_ref, lg_ref, lb_ref, o_ref):
    xb = xb_ref[...]
    hg = jnp.dot(xb, wg_ref[...], preferred_element_type=F32)
    hu = jnp.dot(xb, wu_ref[...], preferred_element_type=F32)
    act = (hg * jax.nn.sigmoid(hg) * hu).astype(BF16)
    sh = jnp.dot(act, wd_ref[...], preferred_element_type=F32)
    wt = wt_ref[...]
    rl = None
    rr = None
    for k in range(TOP_K):
        hi, lo = _unpack_halves(yt_ref[0, k])
        wk = wt[:, k:k + 1]
        rl = wk * hi if rl is None else rl + wk * hi
        rr = wk * lo if rr is None else rr + wk * lo
    z = ALPHA * x1_ref[...] + sh + jnp.concatenate([rl, rr], axis=1)
    o_ref[...] = _layer_norm(z, lg_ref[...], lb_ref[...])


def _ffn_out(yt, x1b, x1, wt, wg, wu, wd, lg, lb, tm):
    t, d = x1.shape
    ds = wg.shape[1]
    dh = yt.shape[-1]
    assert yt.shape == (t // tm, TOP_K, tm, dh)

    def rows(i):
        return (i, 0)

    def whole(i):
        return (0, 0)

    return pl.pallas_call(
        _ffn_out_kernel,
        grid=(t // tm,),
        in_specs=[
            pl.BlockSpec((1, TOP_K, tm, dh), lambda i: (i, 0, 0, 0)),
            pl.BlockSpec((tm, d), rows), pl.BlockSpec((tm, d), rows), pl.BlockSpec((tm, LANES), rows),
            pl.BlockSpec((d, ds), whole), pl.BlockSpec((d, ds), whole), pl.BlockSpec((ds, d), whole),
            pl.BlockSpec((1, d), whole), pl.BlockSpec((1, d), whole),
        ],
        out_specs=pl.BlockSpec((tm, d), rows),
        out_shape=jax.ShapeDtypeStruct((t, d), F32),
        compiler_params=_cparams("parallel"),
        name="ffn_out_ln2",
    )(yt, x1b, x1, wt, wg, wu, wd, lg, lb)


def _dispatch_kernel(dest_hbm, xp_ref, xs_hbm, idx_smem, isem, rsem, *, tm):
    slot = _fetch_tile_indices(dest_hbm, idx_smem, isem)

    _issue_row_copies(
        idx_smem, slot, tm,
        lambda r, k, t: pltpu.make_async_copy(xp_ref.at[pl.ds(t, 1)], xs_hbm.at[pl.ds(r, 1)], rsem))
    done = xs_hbm.at[pl.ds(0, TOP_K * tm)]
    pltpu.make_async_copy(done, done, rsem).wait()


def _dispatch(dest, x1p, npad, tm):
    t, dh = x1p.shape
    assert npad >= TOP_K * tm
    return pl.pallas_call(
        functools.partial(_dispatch_kernel, tm=tm),
        grid=(t // tm,),
        in_specs=[pl.BlockSpec(memory_space=pl.ANY), pl.BlockSpec((tm, dh), lambda i: (i, 0))],
        out_specs=pl.BlockSpec(memory_space=pl.ANY),
        out_shape=jax.ShapeDtypeStruct((npad, dh), U32),
        scratch_shapes=[pltpu.SMEM((2 * TOP_K * tm,), I32), pltpu.SemaphoreType.DMA((2,)),
                        pltpu.SemaphoreType.DMA(())],
        compiler_params=_cparams("arbitrary"),
        name="dispatch",
    )(dest, x1p)


def _route_kernel(xb_ref, wrt_ref, rb_ref, ek_ref, rk_ref, wt_ref, cnt_ref, upper_ref, run_ref, *,
                  tm):
    i = pl.program_id(0)
    gsz = N_EXPERTS // N_GROUP
    ninf = -jnp.inf

    @pl.when(i == 0)
    def _():
        r = lax.broadcasted_iota(I32, (tm, tm), 0)
        c = lax.broadcasted_iota(I32, (tm, tm), 1)
        upper_ref[...] = jnp.where(r < c, 1.0, 0.0).astype(BF16)
        run_ref[...] = jnp.zeros(run_ref.shape, F32)

    logits = lax.dot_general(wrt_ref[...], xb_ref[...], (((1,), (1,)), ((), ())),
                             preferred_element_type=F32)
    scores = jax.nn.sigmoid(logits)
    choice = scores + rb_ref[...]

    ridx = lax.broadcasted_iota(I32, (gsz, tm), 0)
    gscore = []
    for g in range(N_GROUP):
        blk = choice[g * gsz:(g + 1) * gsz, :]
        m1 = jnp.max(blk, axis=0, keepdims=True)
        i1 = jnp.min(jnp.where(blk == m1, ridx, gsz), axis=0, keepdims=True)
        m2 = jnp.max(jnp.where(ridx == i1, ninf, blk), axis=0, keepdims=True)
        gscore.append(m1 + m2)
    masked = []
    for g in range(N_GROUP):
        beaten = jnp.zeros((1, tm), I32)
        for g2 in range(N_GROUP):
            if g2 == g:
                continue
            wins = (gscore[g2] >= gscore[g]) if g2 < g else (gscore[g2] > gscore[g])
            beaten = beaten + jnp.where(wins, 1, 0)
        masked.append(jnp.where(beaten < TOPK_GROUP, choice[g * gsz:(g + 1) * gsz, :], ninf))
    v = jnp.concatenate(masked, axis=0)

    eidx = lax.broadcasted_iota(I32, (N_EXPERTS, tm), 0)
    sel = jnp.zeros((N_EXPERTS, tm), F32)
    e_rows = []
    s_rows = []
    for k in range(TOP_K):
        m = jnp.max(v, axis=0, keepdims=True)
        ik = jnp.min(jnp.where(v == m, eidx, N_EXPERTS), axis=0, keepdims=True)
        hit = eidx == ik
        e_rows.append(ik)
        s_rows.append(jnp.sum(jnp.where(hit, scores, 0.0), axis=0, keepdims=True))
        v = jnp.where(hit, ninf, v)
        sel = jnp.where(hit, 1.0, sel)

    prefix = jnp.dot(sel.astype(BF16), upper_ref[...], preferred_element_type=F32)
    pos = prefix + run_ref[...]
    for k in range(TOP_K):
        rk = jnp.sum(jnp.where(eidx == e_rows[k], pos, 0.0), axis=0, keepdims=True)
        ek_ref[k:k + 1, :] = e_rows[k]
        rk_ref[k:k + 1, :] = rk.astype(I32)
    run_ref[...] = run_ref[...] + jnp.sum(sel, axis=1, keepdims=True)
    cnt_ref[...] = jnp.broadcast_to(run_ref[...], cnt_ref.shape).astype(I32)

    ssum = s_rows[0]
    for k in range(1, TOP_K):
        ssum = ssum + s_rows[k]
    w_rows = [s / (ssum + 1e-20) * ROUTED_SCALE for s in s_rows]
    w_rows.append(jnp.zeros((LANES - TOP_K, tm), F32))
    wt_ref[...] = jnp.concatenate(w_rows, axis=0).T


def _route(x1b, wrt, rbias, tm):
    t, d = x1b.shape
    return pl.pallas_call(
        functools.partial(_route_kernel, tm=tm),
        grid=(t // tm,),
        in_specs=[pl.BlockSpec((tm, d), lambda i: (i, 0)),
                  pl.BlockSpec((N_EXPERTS, d), lambda i: (0, 0)),
                  pl.BlockSpec((N_EXPERTS, 1), lambda i: (0, 0))],
        out_specs=[pl.BlockSpec((TOP_K, tm), lambda i: (0, i)),
                   pl.BlockSpec((TOP_K, tm), lambda i: (0, i)),
                   pl.BlockSpec((tm, LANES), lambda i: (i, 0)),
                   pl.BlockSpec((N_EXPERTS, LANES), lambda i: (0, 0))],
        out_shape=[jax.ShapeDtypeStruct((TOP_K, t), I32), jax.ShapeDtypeStruct((TOP_K, t), I32),
                   jax.ShapeDtypeStruct((t, LANES), F32),
                   jax.ShapeDtypeStruct((N_EXPERTS, LANES), I32)],
        scratch_shapes=[pltpu.VMEM((tm, tm), BF16), pltpu.VMEM((N_EXPERTS, 1), F32)],
        compiler_params=_cparams("arbitrary"),
        name="route",
    )(x1b, wrt, rbias)


def _dest_kernel(ps_ref, ek_ref, rk_ref, d_ref, *, tm):
    e = ek_ref[...]

    def body(j, acc):
        return acc + jnp.where(e == j, ps_ref[j], 0)

    res = lax.fori_loop(0, N_EXPERTS, body, jnp.zeros(e.shape, I32)) + rk_ref[...]
    for a in range(e.shape[1] // tm):
        for k in range(TOP_K):
            d_ref[a:a + 1, k * tm:(k + 1) * tm] = res[k:k + 1, a * tm:(a + 1) * tm]


def _dest(pstart, ek, rk, tm, tw):
    t = ek.shape[1]
    grid_spec = pltpu.PrefetchScalarGridSpec(
        num_scalar_prefetch=1,
        grid=(t // tw,),
        in_specs=[pl.BlockSpec((TOP_K, tw), lambda i, ps: (0, i)),
                  pl.BlockSpec((TOP_K, tw), lambda i, ps: (0, i))],
        out_specs=pl.BlockSpec((tw // tm, TOP_K * tm), lambda i, ps: (i, 0)),
    )
    return pl.pallas_call(
        functools.partial(_dest_kernel, tm=tm),
        grid_spec=grid_spec,
        out_shape=jax.ShapeDtypeStruct((t // tm, TOP_K * tm), I32),
        compiler_params=_cparams("parallel"),
        name="dest",
    )(pstart, ek, rk)


def _layer(x, mem, rel_bias, w_in, b_in, conv_w, diff_lambda, subln_g, mlstm_norm_g, w_mem_kv,
           w_branch, w_out, ln1_g, ln1_b, w_router, router_bias, w_e_gate, w_e_up, w_e_down,
           w_s_gate, w_s_up, w_s_down, ln2_g, ln2_b, layer_idx, cfg):
    b, s, d = x.shape
    t = b * s
    x2 = x.reshape(t, d)

    g0 = (OFF_MLO + 8) * LANES
    w_main = jnp.concatenate([w_in[:, :g0], w_in[:, g0 + 2 * ML_HEADS:]], axis=1).astype(BF16)
    b_main = jnp.concatenate([b_in[:g0], b_in[g0 + 2 * ML_HEADS:]])[None, :]
    w_g = jnp.pad(w_in[:, g0:g0 + 2 * ML_HEADS], ((0, 0), (0, LANES - 2 * ML_HEADS))).astype(BF16)
    b_g = jnp.pad(b_in[g0:g0 + 2 * ML_HEADS], (0, LANES - 2 * ML_HEADS))[None, :]

    u2, gates2 = _proj_in(x2, w_main, b_main, w_g, b_g, cfg["proj_tm"], cfg["proj_tn"])
    u3 = u2.reshape(b, s, N_MAIN)
    gates3 = gates2.reshape(b, s, LANES)

    lam_init = 0.8 - 0.6 * math.exp(-0.3 * layer_idx)
    y_a = _diff_attention(u3, rel_bias, diff_lambda, subln_g[None, :], cfg["attn_tq"], lam_init)
    y_m = _mlstm(u3, gates3, conv_w, mlstm_norm_g[None, :], cfg["ml_chunk"])
    kv = _mm(mem.reshape(-1, d), w_mem_kv.astype(BF16), BF16, cfg["kv_tm"], "mem_kv")
    y_c = _mem_attention(u3, kv.reshape(b, -1, 2 * MA_HEADS * MA_HEAD_DIM), cfg["ma_tq"])

    x1, x1b, x1p = _merge(y_a.reshape(t, d), y_m.reshape(t, d), y_c.reshape(t, d), u2, x2,
                          w_branch.astype(BF16), w_out.astype(BF16), ln1_g[None, :],
                          ln1_b[None, :], cfg["merge_tm"])

    ek, rk, wt, cnt = _route(x1b, w_router.T.astype(BF16), router_bias.astype(F32)[:, None],
                             cfg["route_tm"])
    bm = cfg["expert_bm"]
    items, starts = _work_items(cnt[:, 0], t * TOP_K, bm)
    dest = _dest(starts, ek, rk, cfg["moe_tm"], cfg["dest_tw"]).reshape(-1)
    xs = _dispatch(dest, x1p, t * TOP_K, cfg["moe_tm"])
    ys = _experts(items, xs, w_e_gate, w_e_up, w_e_down, bm)
    tm = cfg["moe_tm"]
    yt = _sc_gather_rows(ys, dest).reshape(t // tm, TOP_K, tm, d // 2)
    out = _ffn_out(yt, x1b, x1, wt, w_s_gate.astype(BF16), w_s_up.astype(BF16),
                   w_s_down.astype(BF16), ln2_g[None, :], ln2_b[None, :], tm)
    return out.reshape(b, s, d)


def _config(b, s):
    t = b * s
    return {
        "proj_tm": min(1024, t), "proj_tn": 1024,
        "attn_tq": 256, "ml_chunk": 256, "kv_tm": 512, "ma_tq": min(512, s),
        "merge_tm": 256, "route_tm": 512, "expert_bm": 512, "moe_tm": 256, "dest_tw": 2048,
    }


def kernel(x, mem, rel_bias, w_in, b_in, conv_w, diff_lambda, subln_g, mlstm_norm_g, w_mem_kv,
           w_branch, w_out, ln1_g, ln1_b, w_router, router_bias, w_e_gate, w_e_up, w_e_down,
           w_s_gate, w_s_up, w_s_down, ln2_g, ln2_b):
    cfg = _config(x.shape[0], x.shape[1])
    for l in range(DEPTH):
        x = _layer(x, mem, rel_bias, w_in[l], b_in[l], conv_w[l], diff_lambda[l], subln_g[l],
                   mlstm_norm_g[l], w_mem_kv[l], w_branch[l], w_out[l], ln1_g[l], ln1_b[l],
                   w_router[l], router_bias[l], w_e_gate[l], w_e_up[l], w_e_down[l],
                   w_s_gate[l], w_s_up[l], w_s_down[l], ln2_g[l], ln2_b[l], l, cfg)
    return x
```

```python
import functools
import math

import numpy as np
import jax
import jax.numpy as jnp
from jax import lax
from jax.experimental import pallas as pl
from jax.experimental.pallas import tpu as pltpu
from jax.experimental.pallas import tpu_sc as plsc

F32 = jnp.float32
BF16 = jnp.bfloat16
U32 = jnp.uint32
I32 = jnp.int32

D_MODEL = 1024
DEPTH = 1
DA_HEAD_DIM = 64
DA_V_DIM = 128
DA_HEADS = 8
ML_HEADS = 4
ML_HEAD_DIM = 256
ML_CONV = 4
MA_HEADS = 4
MA_HEAD_DIM = 256
N_BRANCH = 3
REL_BUCKETS = 32
REL_MAX_DIST = 128
N_EXPERTS = 256
TOP_K = 8
N_GROUP = 8
TOPK_GROUP = 4
D_EXPERT = 256
ROUTED_SCALE = 2.5
ALPHA = (2.0 * DEPTH) ** 0.25

LANES = 128
NEG = -1e30
LOG2E = math.log2(math.e)
FOLD_ACCS = 4
VMEM_LIMIT = 56 * 1024 * 1024

OFF_DAQ, OFF_DAK, OFF_DAV = 0, 8, 16
OFF_MLQ, OFF_MLK, OFF_MLV, OFF_MLO, OFF_MAQ, OFF_GATE = 24, 32, 40, 48, 56, 64
N_MAIN = 88 * LANES


def _cparams(*sem):
    return pltpu.CompilerParams(dimension_semantics=sem, vmem_limit_bytes=VMEM_LIMIT)


def _layer_norm(z, g, b):
    mu = jnp.mean(z, axis=-1, keepdims=True)
    zc = z - mu
    var = jnp.mean(zc * zc, axis=-1, keepdims=True)
    return zc * lax.rsqrt(var + 1e-5) * g + b


def _proj_in_kernel(x_ref, w_ref, b_ref, wg_ref, bg_ref, u_ref, g_ref, xs_ref):
    @pl.when(pl.program_id(1) == 0)
    def _():
        xb = x_ref[...].astype(BF16)
        xs_ref[...] = xb
        g_ref[...] = jnp.dot(xb, wg_ref[...], preferred_element_type=F32) + bg_ref[...]

    acc = jnp.dot(xs_ref[...], w_ref[...], preferred_element_type=F32)
    u_ref[...] = (acc + b_ref[...]).astype(u_ref.dtype)


def _proj_in(x2, w_main, b_main, w_g, b_g, tm, tn):
    t, k = x2.shape
    n = w_main.shape[1]
    return pl.pallas_call(
        _proj_in_kernel,
        grid=(t // tm, n // tn),
        in_specs=[
            pl.BlockSpec((tm, k), lambda i, j: (i, 0)),
            pl.BlockSpec((k, tn), lambda i, j: (0, j)),
            pl.BlockSpec((1, tn), lambda i, j: (0, j)),
            pl.BlockSpec((k, LANES), lambda i, j: (0, 0)),
            pl.BlockSpec((1, LANES), lambda i, j: (0, 0)),
        ],
        out_specs=[
            pl.BlockSpec((tm, tn), lambda i, j: (i, j)),
            pl.BlockSpec((tm, LANES), lambda i, j: (i, 0)),
        ],
        out_shape=[jax.ShapeDtypeStruct((t, n), BF16), jax.ShapeDtypeStruct((t, LANES), F32)],
        scratch_shapes=[pltpu.VMEM((tm, k), BF16)],
        compiler_params=_cparams("parallel", "arbitrary"),
        name="proj_in",
    )(x2, w_main, b_main, w_g, b_g)


def _mm_kernel(x_ref, w_ref, o_ref):
    o_ref[...] = jnp.dot(x_ref[...].astype(BF16), w_ref[...],
                         preferred_element_type=F32).astype(o_ref.dtype)


def _mm(x2, w, out_dtype, tm, name):
    m, k = x2.shape
    n = w.shape[1]
    return pl.pallas_call(
        _mm_kernel,
        grid=(m // tm,),
        in_specs=[pl.BlockSpec((tm, k), lambda i: (i, 0)), pl.BlockSpec((k, n), lambda i: (0, 0))],
        out_specs=pl.BlockSpec((tm, n), lambda i: (i, 0)),
        out_shape=jax.ShapeDtypeStruct((m, n), out_dtype),
        compiler_params=_cparams("parallel"),
        name=name,
    )(x2, w)


def _fold8(x, op):
    n = x.shape[0] // 8
    accs = [x[8 * a:8 * a + 8, :] for a in range(min(FOLD_ACCS, n))]
    for a in range(FOLD_ACCS, n):
        accs[a % FOLD_ACCS] = op(accs[a % FOLD_ACCS], x[8 * a:8 * a + 8, :])
    while len(accs) > 1:
        accs = [op(accs[a], accs[a + 1]) for a in range(0, len(accs), 2)]
    return accs[0]


def _dattn_kernel(rb_ref, q_ref, k_ref, v_ref, bkt_ref, dl_ref, g_ref, o_ref,
                  st_ref, pt_ref, vt_ref, nb_ref, *, tq, seq, lam_init):
    h = pl.program_id(0)
    nq = seq // tq
    nt = (((1,), (1,)), ((), ()))

    @pl.when(pl.program_id(1) == 0)
    def _():
        far = rb_ref[REL_BUCKETS - 1, h]
        for t in range(2):
            bk = bkt_ref[t]
            tile = jnp.full((tq, tq), NEG, F32)
            for bb in range(REL_BUCKETS):
                tile = jnp.where(bk == bb, (rb_ref[bb, h] - far) * LOG2E, tile)
            nb_ref[t * tq:(t + 1) * tq, :] = tile

    for j in range(nq):
        vt_ref[:, j * tq:(j + 1) * tq] = v_ref[0, j * tq:(j + 1) * tq, :].astype(F32).T.astype(BF16)

    dl = dl_ref[...]
    lam = (jnp.exp(jnp.sum(dl[0:1] * dl[1:2], axis=1, keepdims=True))
           - jnp.exp(jnp.sum(dl[2:3] * dl[3:4], axis=1, keepdims=True)) + lam_init)
    lane = lax.broadcasted_iota(I32, (tq, LANES), 1)
    gain = g_ref[...] * (1.0 - lam_init)

    def logits_pass(i):
        qs = q_ref[0, i * tq:(i + 1) * tq, :] * jnp.asarray(DA_HEAD_DIM ** -0.5 * LOG2E, BF16)
        zero = jnp.zeros_like(qs)
        qm = (jnp.where(lane < DA_HEAD_DIM, qs, zero), jnp.where(lane >= DA_HEAD_DIM, qs, zero))
        mx = [None, None]

        def chunk(c):
            for half in range(2):
                s = lax.dot_general(k_ref[0, c * tq:(c + 1) * tq, :], qm[half], nt,
                                    preferred_element_type=F32)
                if c >= i - 1:
                    s = s + nb_ref[(c - i + 1) * tq:(c - i + 2) * tq, :]
                st_ref[2 * (i % 2) + half, c * tq:(c + 1) * tq, :] = s
                f = _fold8(s, jnp.maximum)
                mx[half] = f if mx[half] is None else jnp.maximum(mx[half], f)

        def result():
            return [jnp.max(mx[half], axis=0, keepdims=True) for half in range(2)]

        return [functools.partial(chunk, c) for c in range(i + 1)], result

    def exp_pass(i, m):
        l8 = [None, None]

        def chunk(c):
            for half in range(2):
                p = jnp.exp2(st_ref[2 * (i % 2) + half, c * tq:(c + 1) * tq, :] - m[half])
                pt_ref[2 * (i % 2) + half, c * tq:(c + 1) * tq, :] = p.astype(BF16)
                f = _fold8(p, jnp.add)
                l8[half] = f if l8[half] is None else l8[half] + f

        def result():
            return [jnp.sum(l8[half], axis=0, keepdims=True) for half in range(2)]

        return [functools.partial(chunk, c) for c in range(i + 1)], result

    def value_pass(i, l1, l2):
        kv = (i + 1) * tq
        acc = [jnp.dot(vt_ref[:, 0:kv], pt_ref[2 * (i % 2) + half, 0:kv, :],
                       preferred_element_type=F32) for half in range(2)]
        o = (acc[0] * (1.0 / l1) - acc[1] * (lam / l2)).T
        ms = jnp.mean(o * o, axis=1, keepdims=True)
        y = o * lax.rsqrt(ms + 1e-5) * gain
        o_ref[0, i * tq:(i + 1) * tq, :] = y.astype(o_ref.dtype)

    chunks, result = logits_pass(0)
    for run in chunks:
        run()
    m_next = result()
    for i in range(nq):
        exp_chunks, exp_result = exp_pass(i, m_next)
        next_chunks, next_result = logits_pass(i + 1) if i + 1 < nq else ([], None)
        for c in range(max(len(exp_chunks), len(next_chunks))):
            if c < len(next_chunks):
                next_chunks[c]()
            if c < len(exp_chunks):
                exp_chunks[c]()
        if next_result is not None:
            m_next = next_result()
        value_pass(i, *exp_result())


def _t5_bucket_np(dist):
    n = np.maximum(dist, 0)
    max_exact = REL_BUCKETS // 2
    large = max_exact + (np.log(np.maximum(n, 1).astype(np.float32) / max_exact)
                         / math.log(REL_MAX_DIST / max_exact)
                         * (REL_BUCKETS - max_exact)).astype(np.int32)
    large = np.minimum(large, REL_BUCKETS - 1)
    return np.where(n < max_exact, n, large)


def _near_buckets(tq):
    assert _t5_bucket_np(np.array([tq + 1]))[0] == REL_BUCKETS - 1
    c = np.arange(tq)[:, None]
    r = np.arange(tq)[None, :]
    left = _t5_bucket_np(r + tq - c)
    diag = np.where(r >= c, _t5_bucket_np(r - c), -1)
    return np.stack([left, diag], axis=0).astype(np.int32)


def _diff_attention(u3, rel_bias, diff_lambda, subln_g, tq, lam_init):
    b, s, _ = u3.shape
    kern = functools.partial(_dattn_kernel, tq=tq, seq=s, lam_init=lam_init)
    grid_spec = pltpu.PrefetchScalarGridSpec(
        num_scalar_prefetch=1,
        grid=(DA_HEADS, b),
        in_specs=[
            pl.BlockSpec((1, s, LANES), lambda h, bi, rb: (bi, 0, OFF_DAQ + h)),
            pl.BlockSpec((1, s, LANES), lambda h, bi, rb: (bi, 0, OFF_DAK + h)),
            pl.BlockSpec((1, s, LANES), lambda h, bi, rb: (bi, 0, OFF_DAV + h)),
            pl.BlockSpec((2, tq, tq), lambda h, bi, rb: (0, 0, 0)),
            pl.BlockSpec((4, DA_HEAD_DIM), lambda h, bi, rb: (0, 0)),
            pl.BlockSpec((1, DA_V_DIM), lambda h, bi, rb: (0, 0)),
        ],
        out_specs=pl.BlockSpec((1, s, LANES), lambda h, bi, rb: (bi, 0, h)),
        scratch_shapes=[pltpu.VMEM((4, s, tq), F32), pltpu.VMEM((4, s, tq), BF16),
                        pltpu.VMEM((DA_V_DIM, s), BF16),
                        pltpu.VMEM((2 * tq, tq), F32)],
    )
    return pl.pallas_call(
        kern,
        grid_spec=grid_spec,
        out_shape=jax.ShapeDtypeStruct((b, s, DA_HEADS * DA_V_DIM), BF16),
        compiler_params=_cparams("arbitrary", "arbitrary"),
        name="diff_attn",
    )(rel_bias.astype(F32), u3, u3, u3, jnp.asarray(_near_buckets(tq)), diff_lambda, subln_g)


def _mlstm_kernel(q_ref, k_ref, v_ref, og_ref, g_ref, cwq_ref, cwk_ref, ng_ref, y_ref,
                  ct_ref, gts_ref, *, chunk, seq):
    h = pl.program_id(1)
    nc = seq // chunk
    hd = ML_HEAD_DIM
    lane = lax.broadcasted_iota(jnp.int32, (chunk, LANES), 1)
    row = lax.broadcasted_iota(jnp.int32, (chunk, chunk), 0)
    col = lax.broadcasted_iota(jnp.int32, (chunk, chunk), 1)
    causal = row >= col
    tri = causal.astype(F32)
    cwq = cwq_ref[...]
    cwk = cwk_ref[...]
    ng = ng_ref[...]
    ct_ref[...] = jnp.zeros(ct_ref.shape, F32)

    def conv_silu(ref, cw, r0, c):
        cur = ref[0, pl.ds(r0, chunk), :].astype(F32)
        p0 = pl.multiple_of(jnp.maximum(r0 - 16, 0), 16)
        prev = ref[0, pl.ds(p0, 16), :].astype(F32)
        prev = jnp.where(c > 0, prev, 0.0)
        x = jnp.concatenate([prev, cur], axis=0)
        out = cw[ML_CONV - 1:ML_CONV] * cur
        for tap in range(ML_CONV - 1):
            shifted = pltpu.roll(x, ML_CONV - 1 - tap, 0)[16:]
            out = out + cw[tap:tap + 1] * shifted
        return out * jax.nn.sigmoid(out)

    def body(c, carry):
        m, n = carry
        r0 = pl.multiple_of(c * chunk, chunk)
        g = g_ref[0, pl.ds(r0, chunk), :]
        logf = jnp.minimum(g, 0.0) - jnp.log(1.0 + jnp.exp(-jnp.abs(g)))
        bc = jnp.dot(tri, logf, precision=lax.Precision.HIGHEST, preferred_element_type=F32)
        gts_ref[...] = jnp.where(lane < ML_HEADS, g, bc).T
        irow = gts_ref[pl.ds(h, 1), :]
        brow = gts_ref[pl.ds(ML_HEADS + h, 1), :]
        bcol = jnp.sum(jnp.where(lane == ML_HEADS + h, bc, 0.0), axis=1, keepdims=True)
        icol = jnp.sum(jnp.where(lane == h, g, 0.0), axis=1, keepdims=True)

        dmat = jnp.where(causal, bcol - brow + irow, NEG)
        inter = bcol + m
        m_row = jnp.maximum(inter, jnp.max(dmat, axis=1, keepdims=True))

        q = conv_silu(q_ref, cwq, r0, c)
        k = conv_silu(k_ref, cwk, r0, c) * (hd ** -0.5)
        qb = q.astype(BF16)
        kb = k.astype(BF16)
        vb = v_ref[0, pl.ds(r0, chunk), :]
        sqk = lax.dot_general(qb, kb, (((1,), (1,)), ((), ())), preferred_element_type=F32)
        w = jnp.exp(dmat - m_row) * sqk
        inter_w = jnp.exp(inter - m_row)
        ct = ct_ref[...]
        num = (inter_w * jnp.dot(qb, ct.astype(BF16), preferred_element_type=F32)
               + jnp.dot(w.astype(BF16), vb, preferred_element_type=F32))
        den = inter_w * jnp.sum(q * n, axis=1, keepdims=True) + jnp.sum(w, axis=1, keepdims=True)
        hout = num / jnp.maximum(jnp.abs(den), jnp.exp(-m_row))
        mu = jnp.mean(hout, axis=1, keepdims=True)
        hc = hout - mu
        var = jnp.mean(hc * hc, axis=1, keepdims=True)
        hn = hc * lax.rsqrt(var + 1e-5) * ng
        og = og_ref[0, pl.ds(r0, chunk), :].astype(F32)
        y_ref[0, pl.ds(r0, chunk), :] = (jax.nn.sigmoid(og) * hn).astype(y_ref.dtype)

        total = brow[:, chunk - 1:chunk]
        grow = total - brow + irow
        m_new = jnp.maximum(total + m, jnp.max(grow, axis=1, keepdims=True))
        decay = jnp.exp(total + m - m_new)
        ws = jnp.exp(total - bcol + icol - m_new)
        wsv = (ws * vb.astype(F32)).astype(BF16)
        ct_ref[...] = decay * ct + jnp.dot(k.T.astype(BF16), wsv, preferred_element_type=F32)
        n_new = decay * n + jnp.sum(ws * k, axis=0, keepdims=True)
        return m_new, n_new

    lax.fori_loop(0, nc, body, (jnp.zeros((1, 1), F32), jnp.zeros((1, hd), F32)))


def _mlstm(u3, gates3, conv_w, norm_g, chunk):
    b, s, _ = u3.shape
    hd = ML_HEAD_DIM
    nq = hd // LANES
    kern = functools.partial(_mlstm_kernel, chunk=chunk, seq=s)

    def ublock(off):
        return pl.BlockSpec((1, s, hd), lambda bi, h: (bi, 0, off // nq + h))

    return pl.pallas_call(
        kern,
        grid=(b, ML_HEADS),
        in_specs=[
            ublock(OFF_MLQ), ublock(OFF_MLK), ublock(OFF_MLV), ublock(OFF_MLO),
            pl.BlockSpec((1, s, LANES), lambda bi, h: (bi, 0, 0)),
            pl.BlockSpec((ML_CONV, hd), lambda bi, h: (0, h)),
            pl.BlockSpec((ML_CONV, hd), lambda bi, h: (0, ML_HEADS + h)),
            pl.BlockSpec((1, hd), lambda bi, h: (0, h)),
        ],
        out_specs=pl.BlockSpec((1, s, hd), lambda bi, h: (bi, 0, h)),
        out_shape=jax.ShapeDtypeStruct((b, s, ML_HEADS * hd), BF16),
        scratch_shapes=[pltpu.VMEM((hd, hd), F32), pltpu.VMEM((LANES, chunk), F32)],
        compiler_params=_cparams("parallel", "arbitrary"),
        name="mlstm",
    )(u3, u3, u3, u3, gates3, conv_w, conv_w, norm_g)


def _memattn_kernel(q_ref, k_ref, v_ref, o_ref, *, tq, seq):
    kb = k_ref[0]
    vb = v_ref[0]
    scale = jnp.asarray(MA_HEAD_DIM ** -0.5, BF16)
    for t in range(seq // tq):
        q = q_ref[0, t * tq:(t + 1) * tq, :] * scale
        s = lax.dot_general(q, kb, (((1,), (1,)), ((), ())), preferred_element_type=F32)
        p = jnp.exp(s - jnp.max(s, axis=1, keepdims=True))
        inv = 1.0 / jnp.sum(p, axis=1, keepdims=True)
        o = jnp.dot(p.astype(BF16), vb, preferred_element_type=F32) * inv
        o_ref[0, t * tq:(t + 1) * tq, :] = o.astype(o_ref.dtype)


def _mem_attention(u3, kv3, tq):
    b, s, _ = u3.shape
    mlen = kv3.shape[1]
    hd = MA_HEAD_DIM
    nq = hd // LANES
    kern = functools.partial(_memattn_kernel, tq=tq, seq=s)
    return pl.pallas_call(
        kern,
        grid=(b, MA_HEADS),
        in_specs=[
            pl.BlockSpec((1, s, hd), lambda bi, h: (bi, 0, OFF_MAQ // nq + h)),
            pl.BlockSpec((1, mlen, hd), lambda bi, h: (bi, 0, h)),
            pl.BlockSpec((1, mlen, hd), lambda bi, h: (bi, 0, MA_HEADS + h)),
        ],
        out_specs=pl.BlockSpec((1, s, hd), lambda bi, h: (bi, 0, h)),
        out_shape=jax.ShapeDtypeStruct((b, s, MA_HEADS * hd), BF16),
        compiler_params=_cparams("parallel", "parallel"),
        name="mem_attn",
    )(u3, kv3, kv3)


def _merge_kernel(ya_ref, ym_ref, yc_ref, g0_ref, g1_ref, g2_ref, x_ref, wb_ref, wo_ref,
                  lg_ref, lb_ref, x1_ref, x1b_ref, x1p_ref):
    acc = None
    for n, (y_ref, g_ref) in enumerate(((ya_ref, g0_ref), (ym_ref, g1_ref), (yc_ref, g2_ref))):
        pr = jnp.dot(y_ref[...], wb_ref[n], preferred_element_type=F32)
        t = jax.nn.sigmoid(g_ref[...].astype(F32)) * pr
        acc = t if acc is None else acc + t
    out = jnp.dot(acc.astype(BF16), wo_ref[...], preferred_element_type=F32)
    x1 = _layer_norm(ALPHA * x_ref[...] + out, lg_ref[...], lb_ref[...])
    x1_ref[...] = x1
    x1b = x1.astype(BF16)
    x1b_ref[...] = x1b
    x1p_ref[...] = _pack_halves(x1b)


def _pack_halves(vb):
    w = vb.shape[1] // 2
    hi = lax.bitcast_convert_type(vb[:, :w].astype(F32), U32)
    lo = lax.bitcast_convert_type(vb[:, w:].astype(F32), U32)
    return hi | (lo >> 16)


def _unpack_halves(u):
    hi = lax.bitcast_convert_type(u & jnp.uint32(0xFFFF0000), F32)
    lo = lax.bitcast_convert_type(u << 16, F32)
    return hi, lo


def _merge(ya, ym, yc, u2, x2, wb, wo, lg, lb, tm):
    t, d = x2.shape
    gb = OFF_GATE * LANES // d

    def rows(i):
        return (i, 0)

    return pl.pallas_call(
        _merge_kernel,
        grid=(t // tm,),
        in_specs=[
            pl.BlockSpec((tm, d), rows), pl.BlockSpec((tm, d), rows), pl.BlockSpec((tm, d), rows),
            pl.BlockSpec((tm, d), lambda i: (i, gb)),
            pl.BlockSpec((tm, d), lambda i: (i, gb + 1)),
            pl.BlockSpec((tm, d), lambda i: (i, gb + 2)),
            pl.BlockSpec((tm, d), rows),
            pl.BlockSpec((N_BRANCH, d, d), lambda i: (0, 0, 0)),
            pl.BlockSpec((d, d), lambda i: (0, 0)),
            pl.BlockSpec((1, d), lambda i: (0, 0)),
            pl.BlockSpec((1, d), lambda i: (0, 0)),
        ],
        out_specs=[pl.BlockSpec((tm, d), rows), pl.BlockSpec((tm, d), rows),
                   pl.BlockSpec((tm, d // 2), rows)],
        out_shape=[jax.ShapeDtypeStruct((t, d), F32), jax.ShapeDtypeStruct((t, d), BF16),
                   jax.ShapeDtypeStruct((t, d // 2), U32)],
        compiler_params=_cparams("parallel"),
        name="merge_ln1",
    )(ya, ym, yc, u2, u2, u2, x2, wb, wo, lg, lb)


def _expert_kernel(ib_ref, ie_ref, lo_ref, hi_ref, first_ref, nit_ref, x_ref, wg_ref, wu_ref,
                   wd_ref, o_ref, wgb_ref, wub_ref, wdb_ref):
    i = pl.program_id(0)

    @pl.when((i < nit_ref[0]) & ((i == 0) | (ie_ref[i] != ie_ref[jnp.maximum(i - 1, 0)])))
    def _():
        wgb_ref[...] = wg_ref[0].astype(BF16)
        wub_ref[...] = wu_ref[0].astype(BF16)
        wdb_ref[...] = wd_ref[0].astype(BF16)

    @pl.when(i < nit_ref[0])
    def _():
        xl, xr = _unpack_halves(x_ref[...])
        xl = xl.astype(BF16)
        xr = xr.astype(BF16)
        half = xl.shape[1]

        def up(w_ref):
            return (jnp.dot(xl, w_ref[:half, :], preferred_element_type=F32)
                    + jnp.dot(xr, w_ref[half:, :], preferred_element_type=F32))

        hg = up(wgb_ref)
        hu = up(wub_ref)
        act = (hg * jax.nn.sigmoid(hg) * hu).astype(BF16)
        y = _pack_halves(jnp.dot(act, wdb_ref[...], preferred_element_type=F32).astype(BF16))
        row = lax.broadcasted_iota(I32, y.shape, 0)
        mine = (row >= lo_ref[i]) & (row < hi_ref[i])

        @pl.when(first_ref[i] == 1)
        def _():
            o_ref[...] = jnp.where(mine, y, jnp.uint32(0))

        @pl.when(first_ref[i] == 0)
        def _():
            o_ref[...] = jnp.where(mine, y, o_ref[...])


def _work_items(counts, n_pairs, bm):
    assert n_pairs % bm == 0
    nblocks = n_pairs // bm
    ends = jnp.cumsum(counts)
    starts = ends - counts
    first_blk = starts // bm
    n_e = jnp.where(counts > 0, (ends - 1) // bm - first_blk + 1, 0)
    item_end = jnp.cumsum(n_e)
    item_start = item_end - n_e
    n_items = item_end[-1]
    i = jnp.arange(nblocks + N_EXPERTS)
    valid = i < n_items
    e = jnp.minimum(jnp.sum(item_end[None, :] <= jnp.minimum(i, n_items - 1)[:, None], axis=1),
                    N_EXPERTS - 1)
    blk = jnp.where(valid, first_blk[e] + i - item_start[e], nblocks - 1)
    lo = jnp.clip(starts[e] - blk * bm, 0, bm)
    hi = jnp.where(valid, jnp.clip(ends[e] - blk * bm, 0, bm), 0)
    first = jnp.concatenate([jnp.ones((1,), I32), (blk[1:] != blk[:-1]).astype(I32)])
    items = tuple(a.astype(I32) for a in (blk, e, lo, hi, first, n_items[None]))
    return items, starts.astype(I32)


def _experts(items, xs, w_gate, w_up, w_down, bm):
    n, dh = xs.shape
    d = 2 * dh
    de = w_gate.shape[2]

    def rows(i, ib, ie, lo, hi, first, nit):
        return (ib[i], 0)

    def expert(i, ib, ie, lo, hi, first, nit):
        return (ie[i], 0, 0)

    grid_spec = pltpu.PrefetchScalarGridSpec(
        num_scalar_prefetch=6,
        grid=(items[0].shape[0],),
        in_specs=[
            pl.BlockSpec((bm, dh), rows),
            pl.BlockSpec((1, d, de), expert),
            pl.BlockSpec((1, d, de), expert),
            pl.BlockSpec((1, de, d), expert),
        ],
        out_specs=pl.BlockSpec((bm, dh), rows),
        scratch_shapes=[pltpu.VMEM((d, de), BF16), pltpu.VMEM((d, de), BF16),
                        pltpu.VMEM((de, d), BF16)],
    )
    return pl.pallas_call(
        _expert_kernel,
        grid_spec=grid_spec,
        out_shape=jax.ShapeDtypeStruct((n, dh), U32),
        compiler_params=_cparams("arbitrary"),
        name="experts",
    )(*items, xs, w_gate, w_up, w_down)


SC_WINDOW = 128
SC_WORKERS = 32


def _sc_worker():
    return lax.axis_index("core") * (SC_WORKERS // 2) + lax.axis_index("subcore")


def _sc_scatter_rows(x, idx, tm):
    t, dh = x.shape
    n = idx.shape[0]
    assert n == t * TOP_K and tm % SC_WINDOW == 0
    windows = t // SC_WINDOW
    per = windows // SC_WORKERS
    assert windows % SC_WORKERS == 0
    wpt = tm // SC_WINDOW
    mesh = plsc.VectorSubcoreMesh(core_axis_name="core", subcore_axis_name="subcore")

    @pl.kernel(out_type=jax.ShapeDtypeStruct((n, dh), x.dtype), mesh=mesh,
               scratch_types=[pltpu.VMEM((SC_WINDOW,), I32), pltpu.VMEM((SC_WINDOW, dh), x.dtype)])
    def scatter(x_hbm, idx_hbm, out_hbm, idx_vmem, rows_vmem):
        worker = _sc_worker()

        @pl.loop(0, per)
        def _(j):
            w = worker * per + j
            tile = w // wpt
            off = (w - tile * wpt) * SC_WINDOW
            pltpu.sync_copy(x_hbm.at[pl.ds(w * SC_WINDOW, SC_WINDOW)], rows_vmem)

            @pl.loop(0, TOP_K)
            def _(k):
                base = (tile * TOP_K + k) * tm + off
                pltpu.sync_copy(idx_hbm.at[pl.ds(base, SC_WINDOW)], idx_vmem)
                pltpu.sync_copy(rows_vmem, out_hbm.at[idx_vmem])

    return scatter(x, idx)


def _sc_gather_rows(src, idx):
    n = idx.shape[0]
    dh = src.shape[1]
    per = n // SC_WORKERS
    assert per % SC_WINDOW == 0
    mesh = plsc.VectorSubcoreMesh(core_axis_name="core", subcore_axis_name="subcore")

    @pl.kernel(out_type=jax.ShapeDtypeStruct((n, dh), src.dtype), mesh=mesh,
               scratch_types=[pltpu.VMEM((SC_WINDOW,), I32), pltpu.VMEM((SC_WINDOW, dh), src.dtype)])
    def gather(src_hbm, idx_hbm, out_hbm, idx_vmem, rows_vmem):
        worker = _sc_worker()

        @pl.loop(0, per // SC_WINDOW)
        def _(j):
            base = worker * per + j * SC_WINDOW
            pltpu.sync_copy(idx_hbm.at[pl.ds(base, SC_WINDOW)], idx_vmem)
            pltpu.sync_copy(src_hbm.at[idx_vmem], rows_vmem)
            pltpu.sync_copy(rows_vmem, out_hbm.at[pl.ds(base, SC_WINDOW)])

    return gather(src, idx)


def _ffn_out_kernel(yt_ref, xb_ref, x1_ref, wt_ref, wg_ref, wu_ref, wd_ref, lg_ref, lb_ref, o_ref):
    xb = xb_ref[...]
    hg = jnp.dot(xb, wg_ref[...], preferred_element_type=F32)
    hu = jnp.dot(xb, wu_ref[...], preferred_element_type=F32)
    act = (hg * jax.nn.sigmoid(hg) * hu).astype(BF16)
    sh = jnp.dot(act, wd_ref[...], preferred_element_type=F32)
    wt = wt_ref[...]
    rl = None
    rr = None
    for k in range(TOP_K):
        hi, lo = _unpack_halves(yt_ref[0, k])
        wk = wt[:, k:k + 1]
        rl = wk * hi if rl is None else rl + wk * hi
        rr = wk * lo if rr is None else rr + wk * lo
    z = ALPHA * x1_ref[...] + sh + jnp.concatenate([rl, rr], axis=1)
    o_ref[...] = _layer_norm(z, lg_ref[...], lb_ref[...])


def _ffn_out(yt, x1b, x1, wt, wg, wu, wd, lg, lb, tm):
    t, d = x1.shape
    ds = wg.shape[1]
    dh = yt.shape[-1]
    assert yt.shape == (t // tm, TOP_K, tm, dh)

    def rows(i):
        return (i, 0)

    def whole(i):
        return (0, 0)

    return pl.pallas_call(
        _ffn_out_kernel,
        grid=(t // tm,),
        in_specs=[
            pl.BlockSpec((1, TOP_K, tm, dh), lambda i: (i, 0, 0, 0)),
            pl.BlockSpec((tm, d), rows), pl.BlockSpec((tm, d), rows), pl.BlockSpec((tm, LANES), rows),
            pl.BlockSpec((d, ds), whole), pl.BlockSpec((d, ds), whole), pl.BlockSpec((ds, d), whole),
            pl.BlockSpec((1, d), whole), pl.BlockSpec((1, d), whole),
        ],
        out_specs=pl.BlockSpec((tm, d), rows),
        out_shape=jax.ShapeDtypeStruct((t, d), F32),
        compiler_params=_cparams("parallel"),
        name="ffn_out_ln2",
    )(yt, x1b, x1, wt, wg, wu, wd, lg, lb)


def _route_kernel(xb_ref, wrt_ref, rb_ref, ek_ref, rk_ref, wt_ref, cnt_ref, upper_ref, run_ref, *,
                  tm):
    i = pl.program_id(0)
    gsz = N_EXPERTS // N_GROUP
    ninf = -jnp.inf

    @pl.when(i == 0)
    def _():
        r = lax.broadcasted_iota(I32, (tm, tm), 0)
        c = lax.broadcasted_iota(I32, (tm, tm), 1)
        upper_ref[...] = jnp.where(r < c, 1.0, 0.0).astype(BF16)
        run_ref[...] = jnp.zeros(run_ref.shape, F32)

    logits = lax.dot_general(wrt_ref[...], xb_ref[...], (((1,), (1,)), ((), ())),
                             preferred_element_type=F32)
    scores = jax.nn.sigmoid(logits)
    choice = scores + rb_ref[...]

    ridx = lax.broadcasted_iota(I32, (gsz, tm), 0)
    gscore = []
    for g in range(N_GROUP):
        blk = choice[g * gsz:(g + 1) * gsz, :]
        m1 = jnp.max(blk, axis=0, keepdims=True)
        i1 = jnp.min(jnp.where(blk == m1, ridx, gsz), axis=0, keepdims=True)
        m2 = jnp.max(jnp.where(ridx == i1, ninf, blk), axis=0, keepdims=True)
        gscore.append(m1 + m2)
    masked = []
    for g in range(N_GROUP):
        beaten = jnp.zeros((1, tm), I32)
        for g2 in range(N_GROUP):
            if g2 == g:
                continue
            wins = (gscore[g2] >= gscore[g]) if g2 < g else (gscore[g2] > gscore[g])
            beaten = beaten + jnp.where(wins, 1, 0)
        masked.append(jnp.where(beaten < TOPK_GROUP, choice[g * gsz:(g + 1) * gsz, :], ninf))
    v = jnp.concatenate(masked, axis=0)

    eidx = lax.broadcasted_iota(I32, (N_EXPERTS, tm), 0)
    sel = jnp.zeros((N_EXPERTS, tm), F32)
    e_rows = []
    s_rows = []
    for k in range(TOP_K):
        m = jnp.max(v, axis=0, keepdims=True)
        ik = jnp.min(jnp.where(v == m, eidx, N_EXPERTS), axis=0, keepdims=True)
        hit = eidx == ik
        e_rows.append(ik)
        s_rows.append(jnp.sum(jnp.where(hit, scores, 0.0), axis=0, keepdims=True))
        v = jnp.where(hit, ninf, v)
        sel = jnp.where(hit, 1.0, sel)

    prefix = jnp.dot(sel.astype(BF16), upper_ref[...], preferred_element_type=F32)
    pos = prefix + run_ref[...]
    for k in range(TOP_K):
        rk = jnp.sum(jnp.where(eidx == e_rows[k], pos, 0.0), axis=0, keepdims=True)
        ek_ref[k:k + 1, :] = e_rows[k]
        rk_ref[k:k + 1, :] = rk.astype(I32)
    run_ref[...] = run_ref[...] + jnp.sum(sel, axis=1, keepdims=True)
    cnt_ref[...] = jnp.broadcast_to(run_ref[...], cnt_ref.shape).astype(I32)

    ssum = s_rows[0]
    for k in range(1, TOP_K):
        ssum = ssum + s_rows[k]
    w_rows = [s / (ssum + 1e-20) * ROUTED_SCALE for s in s_rows]
    w_rows.append(jnp.zeros((LANES - TOP_K, tm), F32))
    wt_ref[...] = jnp.concatenate(w_rows, axis=0).T


def _route(x1b, wrt, rbias, tm):
    t, d = x1b.shape
    return pl.pallas_call(
        functools.partial(_route_kernel, tm=tm),
        grid=(t // tm,),
        in_specs=[pl.BlockSpec((tm, d), lambda i: (i, 0)),
                  pl.BlockSpec((N_EXPERTS, d), lambda i: (0, 0)),
                  pl.BlockSpec((N_EXPERTS, 1), lambda i: (0, 0))],
        out_specs=[pl.BlockSpec((TOP_K, tm), lambda i: (0, i)),
                   pl.BlockSpec((TOP_K, tm), lambda i: (0, i)),
                   pl.BlockSpec((tm, LANES), lambda i: (i, 0)),
                   pl.BlockSpec((N_EXPERTS, LANES), lambda i: (0, 0))],
        out_shape=[jax.ShapeDtypeStruct((TOP_K, t), I32), jax.ShapeDtypeStruct((TOP_K, t), I32),
                   jax.ShapeDtypeStruct((t, LANES), F32),
                   jax.ShapeDtypeStruct((N_EXPERTS, LANES), I32)],
        scratch_shapes=[pltpu.VMEM((tm, tm), BF16), pltpu.VMEM((N_EXPERTS, 1), F32)],
        compiler_params=_cparams("arbitrary"),
        name="route",
    )(x1b, wrt, rbias)


def _dest_kernel(ps_ref, ek_ref, rk_ref, d_ref, *, tm):
    e = ek_ref[...]

    def body(j, acc):
        return acc + jnp.where(e == j, ps_ref[j], 0)

    res = lax.fori_loop(0, N_EXPERTS, body, jnp.zeros(e.shape, I32)) + rk_ref[...]
    for a in range(e.shape[1] // tm):
        for k in range(TOP_K):
            d_ref[a:a + 1, k * tm:(k + 1) * tm] = res[k:k + 1, a * tm:(a + 1) * tm]


def _dest(pstart, ek, rk, tm, tw):
    t = ek.shape[1]
    grid_spec = pltpu.PrefetchScalarGridSpec(
        num_scalar_prefetch=1,
        grid=(t // tw,),
        in_specs=[pl.BlockSpec((TOP_K, tw), lambda i, ps: (0, i)),
                  pl.BlockSpec((TOP_K, tw), lambda i, ps: (0, i))],
        out_specs=pl.BlockSpec((tw // tm, TOP_K * tm), lambda i, ps: (i, 0)),
    )
    return pl.pallas_call(
        functools.partial(_dest_kernel, tm=tm),
        grid_spec=grid_spec,
        out_shape=jax.ShapeDtypeStruct((t // tm, TOP_K * tm), I32),
        compiler_params=_cparams("parallel"),
        name="dest",
    )(pstart, ek, rk)


def _layer(x, mem, rel_bias, w_in, b_in, conv_w, diff_lambda, subln_g, mlstm_norm_g, w_mem_kv,
           w_branch, w_out, ln1_g, ln1_b, w_router, router_bias, w_e_gate, w_e_up, w_e_down,
           w_s_gate, w_s_up, w_s_down, ln2_g, ln2_b, layer_idx, cfg):
    b, s, d = x.shape
    t = b * s
    x2 = x.reshape(t, d)

    g0 = (OFF_MLO + 8) * LANES
    w_main = jnp.concatenate([w_in[:, :g0], w_in[:, g0 + 2 * ML_HEADS:]], axis=1).astype(BF16)
    b_main = jnp.concatenate([b_in[:g0], b_in[g0 + 2 * ML_HEADS:]])[None, :]
    w_g = jnp.pad(w_in[:, g0:g0 + 2 * ML_HEADS], ((0, 0), (0, LANES - 2 * ML_HEADS))).astype(BF16)
    b_g = jnp.pad(b_in[g0:g0 + 2 * ML_HEADS], (0, LANES - 2 * ML_HEADS))[None, :]

    u2, gates2 = _proj_in(x2, w_main, b_main, w_g, b_g, cfg["proj_tm"], cfg["proj_tn"])
    u3 = u2.reshape(b, s, N_MAIN)
    gates3 = gates2.reshape(b, s, LANES)

    lam_init = 0.8 - 0.6 * math.exp(-0.3 * layer_idx)
    y_a = _diff_attention(u3, rel_bias, diff_lambda, subln_g[None, :], cfg["attn_tq"], lam_init)
    y_m = _mlstm(u3, gates3, conv_w, mlstm_norm_g[None, :], cfg["ml_chunk"])
    kv = _mm(mem.reshape(-1, d), w_mem_kv.astype(BF16), BF16, cfg["kv_tm"], "mem_kv")
    y_c = _mem_attention(u3, kv.reshape(b, -1, 2 * MA_HEADS * MA_HEAD_DIM), cfg["ma_tq"])

    x1, x1b, x1p = _merge(y_a.reshape(t, d), y_m.reshape(t, d), y_c.reshape(t, d), u2, x2,
                          w_branch.astype(BF16), w_out.astype(BF16), ln1_g[None, :],
                          ln1_b[None, :], cfg["merge_tm"])

    ek, rk, wt, cnt = _route(x1b, w_router.T.astype(BF16), router_bias.astype(F32)[:, None],
                             cfg["route_tm"])
    bm = cfg["expert_bm"]
    items, starts = _work_items(cnt[:, 0], t * TOP_K, bm)
    dest = _dest(starts, ek, rk, cfg["moe_tm"], cfg["dest_tw"]).reshape(-1)
    xs = _sc_scatter_rows(x1p, dest, cfg["moe_tm"])
    ys = _experts(items, xs, w_e_gate, w_e_up, w_e_down, bm)
    tm = cfg["moe_tm"]
    yt = _sc_gather_rows(ys, dest).reshape(t // tm, TOP_K, tm, d // 2)
    out = _ffn_out(yt, x1b, x1, wt, w_s_gate.astype(BF16), w_s_up.astype(BF16),
                   w_s_down.astype(BF16), ln2_g[None, :], ln2_b[None, :], tm)
    return out.reshape(b, s, d)


def _config(b, s):
    t = b * s
    return {
        "proj_tm": min(1024, t), "proj_tn": 1024,
        "attn_tq": 256, "ml_chunk": 256, "kv_tm": 512, "ma_tq": min(512, s),
        "merge_tm": 256, "route_tm": 512, "expert_bm": 512, "moe_tm": 256, "dest_tw": 2048,
    }


def kernel(x, mem, rel_bias, w_in, b_in, conv_w, diff_lambda, subln_g, mlstm_norm_g, w_mem_kv,
           w_branch, w_out, ln1_g, ln1_b, w_router, router_bias, w_e_gate, w_e_up, w_e_down,
           w_s_gate, w_s_up, w_s_down, ln2_g, ln2_b):
    cfg = _config(x.shape[0], x.shape[1])
    for l in range(DEPTH):
        x = _layer(x, mem, rel_bias, w_in[l], b_in[l], conv_w[l], diff_lambda[l], subln_g[l],
                   mlstm_norm_g[l], w_mem_kv[l], w_branch[l], w_out[l], ln1_g[l], ln1_b[l],
                   w_router[l], router_bias[l], w_e_gate[l], w_e_up[l], w_e_down[l],
                   w_s_gate[l], w_s_up[l], w_s_down[l], ln2_g[l], ln2_b[l], l, cfg)
    return x
```

```python
import functools
import math

import numpy as np
import jax
import jax.numpy as jnp
from jax import lax
from jax.experimental import pallas as pl
from jax.experimental.pallas import tpu as pltpu
from jax.experimental.pallas import tpu_sc as plsc

F32 = jnp.float32
BF16 = jnp.bfloat16
U32 = jnp.uint32
I32 = jnp.int32

D_MODEL = 1024
DEPTH = 1
DA_HEAD_DIM = 64
DA_V_DIM = 128
DA_HEADS = 8
ML_HEADS = 4
ML_HEAD_DIM = 256
ML_CONV = 4
MA_HEADS = 4
MA_HEAD_DIM = 256
N_BRANCH = 3
REL_BUCKETS = 32
REL_MAX_DIST = 128
N_EXPERTS = 256
TOP_K = 8
N_GROUP = 8
TOPK_GROUP = 4
D_EXPERT = 256
ROUTED_SCALE = 2.5
ALPHA = (2.0 * DEPTH) ** 0.25

LANES = 128
NEG = -1e30
LOG2E = math.log2(math.e)
FOLD_ACCS = 4
VMEM_LIMIT = 56 * 1024 * 1024

OFF_DAQ, OFF_DAK, OFF_DAV = 0, 8, 16
OFF_MLQ, OFF_MLK, OFF_MLV, OFF_MLO, OFF_MAQ, OFF_GATE = 24, 32, 40, 48, 56, 64
N_MAIN = 88 * LANES


def _cparams(*sem):
    return pltpu.CompilerParams(dimension_semantics=sem, vmem_limit_bytes=VMEM_LIMIT)


def _layer_norm(z, g, b):
    mu = jnp.mean(z, axis=-1, keepdims=True)
    zc = z - mu
    var = jnp.mean(zc * zc, axis=-1, keepdims=True)
    return zc * lax.rsqrt(var + 1e-5) * g + b


def _proj_in_kernel(x_ref, w_ref, b_ref, wg_ref, bg_ref, u_ref, g_ref, xs_ref):
    @pl.when(pl.program_id(1) == 0)
    def _():
        xb = x_ref[...].astype(BF16)
        xs_ref[...] = xb
        g_ref[...] = jnp.dot(xb, wg_ref[...], preferred_element_type=F32) + bg_ref[...]

    acc = jnp.dot(xs_ref[...], w_ref[...], preferred_element_type=F32)
    u_ref[...] = (acc + b_ref[...]).astype(u_ref.dtype)


def _proj_in(x2, w_main, b_main, w_g, b_g, tm, tn):
    t, k = x2.shape
    n = w_main.shape[1]
    return pl.pallas_call(
        _proj_in_kernel,
        grid=(t // tm, n // tn),
        in_specs=[
            pl.BlockSpec((tm, k), lambda i, j: (i, 0)),
            pl.BlockSpec((k, tn), lambda i, j: (0, j)),
            pl.BlockSpec((1, tn), lambda i, j: (0, j)),
            pl.BlockSpec((k, LANES), lambda i, j: (0, 0)),
            pl.BlockSpec((1, LANES), lambda i, j: (0, 0)),
        ],
        out_specs=[
            pl.BlockSpec((tm, tn), lambda i, j: (i, j)),
            pl.BlockSpec((tm, LANES), lambda i, j: (i, 0)),
        ],
        out_shape=[jax.ShapeDtypeStruct((t, n), BF16), jax.ShapeDtypeStruct((t, LANES), F32)],
        scratch_shapes=[pltpu.VMEM((tm, k), BF16)],
        compiler_params=_cparams("parallel", "arbitrary"),
        name="proj_in",
    )(x2, w_main, b_main, w_g, b_g)


def _mm_kernel(x_ref, w_ref, o_ref):
    o_ref[...] = jnp.dot(x_ref[...].astype(BF16), w_ref[...],
                         preferred_element_type=F32).astype(o_ref.dtype)


def _mm(x2, w, out_dtype, tm, name):
    m, k = x2.shape
    n = w.shape[1]
    return pl.pallas_call(
        _mm_kernel,
        grid=(m // tm,),
        in_specs=[pl.BlockSpec((tm, k), lambda i: (i, 0)), pl.BlockSpec((k, n), lambda i: (0, 0))],
        out_specs=pl.BlockSpec((tm, n), lambda i: (i, 0)),
        out_shape=jax.ShapeDtypeStruct((m, n), out_dtype),
        compiler_params=_cparams("parallel"),
        name=name,
    )(x2, w)


def _fold8(x, op):
    n = x.shape[0] // 8
    accs = [x[8 * a:8 * a + 8, :] for a in range(min(FOLD_ACCS, n))]
    for a in range(FOLD_ACCS, n):
        accs[a % FOLD_ACCS] = op(accs[a % FOLD_ACCS], x[8 * a:8 * a + 8, :])
    while len(accs) > 1:
        accs = [op(accs[a], accs[a + 1]) for a in range(0, len(accs), 2)]
    return accs[0]


def _dattn_kernel(rb_ref, q_ref, k_ref, v_ref, bkt_ref, dl_ref, g_ref, o_ref,
                  st_ref, pt_ref, vt_ref, nb_ref, *, tq, seq, lam_init):
    h = pl.program_id(0)
    nq = seq // tq
    nt = (((1,), (1,)), ((), ()))

    @pl.when(pl.program_id(1) == 0)
    def _():
        far = rb_ref[REL_BUCKETS - 1, h]
        for t in range(2):
            bk = bkt_ref[t]
            tile = jnp.full((tq, tq), NEG, F32)
            for bb in range(REL_BUCKETS):
                tile = jnp.where(bk == bb, (rb_ref[bb, h] - far) * LOG2E, tile)
            nb_ref[t * tq:(t + 1) * tq, :] = tile

    for j in range(nq):
        vt_ref[:, j * tq:(j + 1) * tq] = v_ref[0, j * tq:(j + 1) * tq, :].astype(F32).T.astype(BF16)

    dl = dl_ref[...]
    lam = (jnp.exp(jnp.sum(dl[0:1] * dl[1:2], axis=1, keepdims=True))
           - jnp.exp(jnp.sum(dl[2:3] * dl[3:4], axis=1, keepdims=True)) + lam_init)
    lane = lax.broadcasted_iota(I32, (tq, LANES), 1)
    gain = g_ref[...] * (1.0 - lam_init)

    def logits_pass(i):
        qs = q_ref[0, i * tq:(i + 1) * tq, :] * jnp.asarray(DA_HEAD_DIM ** -0.5 * LOG2E, BF16)
        zero = jnp.zeros_like(qs)
        qm = (jnp.where(lane < DA_HEAD_DIM, qs, zero), jnp.where(lane >= DA_HEAD_DIM, qs, zero))
        mx = [None, None]

        def chunk(c):
            for half in range(2):
                s = lax.dot_general(k_ref[0, c * tq:(c + 1) * tq, :], qm[half], nt,
                                    preferred_element_type=F32)
                if c >= i - 1:
                    s = s + nb_ref[(c - i + 1) * tq:(c - i + 2) * tq, :]
                st_ref[2 * (i % 2) + half, c * tq:(c + 1) * tq, :] = s
                f = _fold8(s, jnp.maximum)
                mx[half] = f if mx[half] is None else jnp.maximum(mx[half], f)

        def result():
            return [jnp.max(mx[half], axis=0, keepdims=True) for half in range(2)]

        return [functools.partial(chunk, c) for c in range(i + 1)], result

    def exp_pass(i, m):
        l8 = [None, None]

        def chunk(c):
            for half in range(2):
                p = jnp.exp2(st_ref[2 * (i % 2) + half, c * tq:(c + 1) * tq, :] - m[half])
                pt_ref[2 * (i % 2) + half, c * tq:(c + 1) * tq, :] = p.astype(BF16)
                f = _fold8(p, jnp.add)
                l8[half] = f if l8[half] is None else l8[half] + f

        def result():
            return [jnp.sum(l8[half], axis=0, keepdims=True) for half in range(2)]

        return [functools.partial(chunk, c) for c in range(i + 1)], result

    def value_pass(i, l1, l2):
        kv = (i + 1) * tq
        acc = [jnp.dot(vt_ref[:, 0:kv], pt_ref[2 * (i % 2) + half, 0:kv, :],
                       preferred_element_type=F32) for half in range(2)]
        o = (acc[0] * (1.0 / l1) - acc[1] * (lam / l2)).T
        ms = jnp.mean(o * o, axis=1, keepdims=True)
        y = o * lax.rsqrt(ms + 1e-5) * gain
        o_ref[0, i * tq:(i + 1) * tq, :] = y.astype(o_ref.dtype)

    chunks, result = logits_pass(0)
    for run in chunks:
        run()
    m_next = result()
    for i in range(nq):
        exp_chunks, exp_result = exp_pass(i, m_next)
        next_chunks, next_result = logits_pass(i + 1) if i + 1 < nq else ([], None)
        for c in range(max(len(exp_chunks), len(next_chunks))):
            if c < len(next_chunks):
                next_chunks[c]()
            if c < len(exp_chunks):
                exp_chunks[c]()
        if next_result is not None:
            m_next = next_result()
        value_pass(i, *exp_result())


def _t5_bucket_np(dist):
    n = np.maximum(dist, 0)
    max_exact = REL_BUCKETS // 2
    large = max_exact + (np.log(np.maximum(n, 1).astype(np.float32) / max_exact)
                         / math.log(REL_MAX_DIST / max_exact)
                         * (REL_BUCKETS - max_exact)).astype(np.int32)
    large = np.minimum(large, REL_BUCKETS - 1)
    return np.where(n < max_exact, n, large)


def _near_buckets(tq):
    assert _t5_bucket_np(np.array([tq + 1]))[0] == REL_BUCKETS - 1
    c = np.arange(tq)[:, None]
    r = np.arange(tq)[None, :]
    left = _t5_bucket_np(r + tq - c)
    diag = np.where(r >= c, _t5_bucket_np(r - c), -1)
    return np.stack([left, diag], axis=0).astype(np.int32)


def _diff_attention(u3, rel_bias, diff_lambda, subln_g, tq, lam_init):
    b, s, _ = u3.shape
    kern = functools.partial(_dattn_kernel, tq=tq, seq=s, lam_init=lam_init)
    grid_spec = pltpu.PrefetchScalarGridSpec(
        num_scalar_prefetch=1,
        grid=(DA_HEADS, b),
        in_specs=[
            pl.BlockSpec((1, s, LANES), lambda h, bi, rb: (bi, 0, OFF_DAQ + h)),
            pl.BlockSpec((1, s, LANES), lambda h, bi, rb: (bi, 0, OFF_DAK + h)),
            pl.BlockSpec((1, s, LANES), lambda h, bi, rb: (bi, 0, OFF_DAV + h)),
            pl.BlockSpec((2, tq, tq), lambda h, bi, rb: (0, 0, 0)),
            pl.BlockSpec((4, DA_HEAD_DIM), lambda h, bi, rb: (0, 0)),
            pl.BlockSpec((1, DA_V_DIM), lambda h, bi, rb: (0, 0)),
        ],
        out_specs=pl.BlockSpec((1, s, LANES), lambda h, bi, rb: (bi, 0, h)),
        scratch_shapes=[pltpu.VMEM((4, s, tq), F32), pltpu.VMEM((4, s, tq), BF16),
                        pltpu.VMEM((DA_V_DIM, s), BF16),
                        pltpu.VMEM((2 * tq, tq), F32)],
    )
    return pl.pallas_call(
        kern,
        grid_spec=grid_spec,
        out_shape=jax.ShapeDtypeStruct((b, s, DA_HEADS * DA_V_DIM), BF16),
        compiler_params=_cparams("arbitrary", "arbitrary"),
        name="diff_attn",
    )(rel_bias.astype(F32), u3, u3, u3, jnp.asarray(_near_buckets(tq)), diff_lambda, subln_g)


def _mlstm_kernel(q_ref, k_ref, v_ref, og_ref, g_ref, cwq_ref, cwk_ref, ng_ref, y_ref,
                  ct_ref, gts_ref, *, chunk, seq):
    h = pl.program_id(1)
    nc = seq // chunk
    hd = ML_HEAD_DIM
    lane = lax.broadcasted_iota(jnp.int32, (chunk, LANES), 1)
    row = lax.broadcasted_iota(jnp.int32, (chunk, chunk), 0)
    col = lax.broadcasted_iota(jnp.int32, (chunk, chunk), 1)
    causal = row >= col
    tri = causal.astype(F32)
    cwq = cwq_ref[...]
    cwk = cwk_ref[...]
    ng = ng_ref[...]
    ct_ref[...] = jnp.zeros(ct_ref.shape, F32)

    def conv_silu(ref, cw, r0, c):
        cur = ref[0, pl.ds(r0, chunk), :].astype(F32)
        p0 = pl.multiple_of(jnp.maximum(r0 - 16, 0), 16)
        prev = ref[0, pl.ds(p0, 16), :].astype(F32)
        prev = jnp.where(c > 0, prev, 0.0)
        x = jnp.concatenate([prev, cur], axis=0)
        out = cw[ML_CONV - 1:ML_CONV] * cur
        for tap in range(ML_CONV - 1):
            shifted = pltpu.roll(x, ML_CONV - 1 - tap, 0)[16:]
            out = out + cw[tap:tap + 1] * shifted
        return out * jax.nn.sigmoid(out)

    def body(c, carry):
        m, n = carry
        r0 = pl.multiple_of(c * chunk, chunk)
        g = g_ref[0, pl.ds(r0, chunk), :]
        logf = jnp.minimum(g, 0.0) - jnp.log(1.0 + jnp.exp(-jnp.abs(g)))
        bc = jnp.dot(tri, logf, precision=lax.Precision.HIGHEST, preferred_element_type=F32)
        gts_ref[...] = jnp.where(lane < ML_HEADS, g, bc).T
        irow = gts_ref[pl.ds(h, 1), :]
        brow = gts_ref[pl.ds(ML_HEADS + h, 1), :]
        bcol = jnp.sum(jnp.where(lane == ML_HEADS + h, bc, 0.0), axis=1, keepdims=True)
        icol = jnp.sum(jnp.where(lane == h, g, 0.0), axis=1, keepdims=True)

        dmat = jnp.where(causal, bcol - brow + irow, NEG)
        inter = bcol + m
        m_row = jnp.maximum(inter, jnp.max(dmat, axis=1, keepdims=True))

        q = conv_silu(q_ref, cwq, r0, c)
        k = conv_silu(k_ref, cwk, r0, c) * (hd ** -0.5)
        qb = q.astype(BF16)
        kb = k.astype(BF16)
        vb = v_ref[0, pl.ds(r0, chunk), :]
        sqk = lax.dot_general(qb, kb, (((1,), (1,)), ((), ())), preferred_element_type=F32)
        w = jnp.exp(dmat - m_row) * sqk
        inter_w = jnp.exp(inter - m_row)
        ct = ct_ref[...]
        num = (inter_w * jnp.dot(qb, ct.astype(BF16), preferred_element_type=F32)
               + jnp.dot(w.astype(BF16), vb, preferred_element_type=F32))
        den = inter_w * jnp.sum(q * n, axis=1, keepdims=True) + jnp.sum(w, axis=1, keepdims=True)
        hout = num / jnp.maximum(jnp.abs(den), jnp.exp(-m_row))
        mu = jnp.mean(hout, axis=1, keepdims=True)
        hc = hout - mu
        var = jnp.mean(hc * hc, axis=1, keepdims=True)
        hn = hc * lax.rsqrt(var + 1e-5) * ng
        og = og_ref[0, pl.ds(r0, chunk), :].astype(F32)
        y_ref[0, pl.ds(r0, chunk), :] = (jax.nn.sigmoid(og) * hn).astype(y_ref.dtype)

        total = brow[:, chunk - 1:chunk]
        grow = total - brow + irow
        m_new = jnp.maximum(total + m, jnp.max(grow, axis=1, keepdims=True))
        decay = jnp.exp(total + m - m_new)
        ws = jnp.exp(total - bcol + icol - m_new)
        wsv = (ws * vb.astype(F32)).astype(BF16)
        ct_ref[...] = decay * ct + jnp.dot(k.T.astype(BF16), wsv, preferred_element_type=F32)
        n_new = decay * n + jnp.sum(ws * k, axis=0, keepdims=True)
        return m_new, n_new

    lax.fori_loop(0, nc, body, (jnp.zeros((1, 1), F32), jnp.zeros((1, hd), F32)))


def _mlstm(u3, gates3, conv_w, norm_g, chunk):
    b, s, _ = u3.shape
    hd = ML_HEAD_DIM
    nq = hd // LANES
    kern = functools.partial(_mlstm_kernel, chunk=chunk, seq=s)

    def ublock(off):
        return pl.BlockSpec((1, s, hd), lambda bi, h: (bi, 0, off // nq + h))

    return pl.pallas_call(
        kern,
        grid=(b, ML_HEADS),
        in_specs=[
            ublock(OFF_MLQ), ublock(OFF_MLK), ublock(OFF_MLV), ublock(OFF_MLO),
            pl.BlockSpec((1, s, LANES), lambda bi, h: (bi, 0, 0)),
            pl.BlockSpec((ML_CONV, hd), lambda bi, h: (0, h)),
            pl.BlockSpec((ML_CONV, hd), lambda bi, h: (0, ML_HEADS + h)),
            pl.BlockSpec((1, hd), lambda bi, h: (0, h)),
        ],
        out_specs=pl.BlockSpec((1, s, hd), lambda bi, h: (bi, 0, h)),
        out_shape=jax.ShapeDtypeStruct((b, s, ML_HEADS * hd), BF16),
        scratch_shapes=[pltpu.VMEM((hd, hd), F32), pltpu.VMEM((LANES, chunk), F32)],
        compiler_params=_cparams("parallel", "arbitrary"),
        name="mlstm",
    )(u3, u3, u3, u3, gates3, conv_w, conv_w, norm_g)


def _memattn_kernel(q_ref, k_ref, v_ref, o_ref, *, tq, seq):
    kb = k_ref[0]
    vb = v_ref[0]
    scale = jnp.asarray(MA_HEAD_DIM ** -0.5, BF16)
    for t in range(seq // tq):
        q = q_ref[0, t * tq:(t + 1) * tq, :] * scale
        s = lax.dot_general(q, kb, (((1,), (1,)), ((), ())), preferred_element_type=F32)
        p = jnp.exp(s - jnp.max(s, axis=1, keepdims=True))
        inv = 1.0 / jnp.sum(p, axis=1, keepdims=True)
        o = jnp.dot(p.astype(BF16), vb, preferred_element_type=F32) * inv
        o_ref[0, t * tq:(t + 1) * tq, :] = o.astype(o_ref.dtype)


def _mem_attention(u3, kv3, tq):
    b, s, _ = u3.shape
    mlen = kv3.shape[1]
    hd = MA_HEAD_DIM
    nq = hd // LANES
    kern = functools.partial(_memattn_kernel, tq=tq, seq=s)
    return pl.pallas_call(
        kern,
        grid=(b, MA_HEADS),
        in_specs=[
            pl.BlockSpec((1, s, hd), lambda bi, h: (bi, 0, OFF_MAQ // nq + h)),
            pl.BlockSpec((1, mlen, hd), lambda bi, h: (bi, 0, h)),
            pl.BlockSpec((1, mlen, hd), lambda bi, h: (bi, 0, MA_HEADS + h)),
        ],
        out_specs=pl.BlockSpec((1, s, hd), lambda bi, h: (bi, 0, h)),
        out_shape=jax.ShapeDtypeStruct((b, s, MA_HEADS * hd), BF16),
        compiler_params=_cparams("parallel", "parallel"),
        name="mem_attn",
    )(u3, kv3, kv3)


def _merge_kernel(ya_ref, ym_ref, yc_ref, g0_ref, g1_ref, g2_ref, x_ref, wb_ref, wo_ref,
                  lg_ref, lb_ref, x1_ref, x1b_ref, x1p_ref):
    acc = None
    for n, (y_ref, g_ref) in enumerate(((ya_ref, g0_ref), (ym_ref, g1_ref), (yc_ref, g2_ref))):
        pr = jnp.dot(y_ref[...], wb_ref[n], preferred_element_type=F32)
        t = jax.nn.sigmoid(g_ref[...].astype(F32)) * pr
        acc = t if acc is None else acc + t
    out = jnp.dot(acc.astype(BF16), wo_ref[...], preferred_element_type=F32)
    x1 = _layer_norm(ALPHA * x_ref[...] + out, lg_ref[...], lb_ref[...])
    x1_ref[...] = x1
    x1b = x1.astype(BF16)
    x1b_ref[...] = x1b
    x1p_ref[...] = _pack_halves(x1b)


def _pack_halves(vb):
    w = vb.shape[1] // 2
    hi = lax.bitcast_convert_type(vb[:, :w].astype(F32), U32)
    lo = lax.bitcast_convert_type(vb[:, w:].astype(F32), U32)
    return hi | (lo >> 16)


def _unpack_halves(u):
    hi = lax.bitcast_convert_type(u & jnp.uint32(0xFFFF0000), F32)
    lo = lax.bitcast_convert_type(u << 16, F32)
    return hi, lo


def _merge(ya, ym, yc, u2, x2, wb, wo, lg, lb, tm):
    t, d = x2.shape
    gb = OFF_GATE * LANES // d

    def rows(i):
        return (i, 0)

    return pl.pallas_call(
        _merge_kernel,
        grid=(t // tm,),
        in_specs=[
            pl.BlockSpec((tm, d), rows), pl.BlockSpec((tm, d), rows), pl.BlockSpec((tm, d), rows),
            pl.BlockSpec((tm, d), lambda i: (i, gb)),
            pl.BlockSpec((tm, d), lambda i: (i, gb + 1)),
            pl.BlockSpec((tm, d), lambda i: (i, gb + 2)),
            pl.BlockSpec((tm, d), rows),
            pl.BlockSpec((N_BRANCH, d, d), lambda i: (0, 0, 0)),
            pl.BlockSpec((d, d), lambda i: (0, 0)),
            pl.BlockSpec((1, d), lambda i: (0, 0)),
            pl.BlockSpec((1, d), lambda i: (0, 0)),
        ],
        out_specs=[pl.BlockSpec((tm, d), rows), pl.BlockSpec((tm, d), rows),
                   pl.BlockSpec((tm, d // 2), rows)],
        out_shape=[jax.ShapeDtypeStruct((t, d), F32), jax.ShapeDtypeStruct((t, d), BF16),
                   jax.ShapeDtypeStruct((t, d // 2), U32)],
        compiler_params=_cparams("parallel"),
        name="merge_ln1",
    )(ya, ym, yc, u2, u2, u2, x2, wb, wo, lg, lb)


def _expert_kernel(ib_ref, ie_ref, lo_ref, hi_ref, first_ref, nit_ref, new_ref, slot_ref, next_ref,
                   x_ref, wg_hbm, wu_hbm, wd_hbm, o_ref, wgf_ref, wuf_ref, wdf_ref,
                   wgb_ref, wub_ref, wdb_ref, wsem):
    i = pl.program_id(0)

    def weight_copies(e, s):
        return (pltpu.make_async_copy(wg_hbm.at[e], wgf_ref.at[s], wsem.at[s]),
                pltpu.make_async_copy(wu_hbm.at[e], wuf_ref.at[s], wsem.at[s]),
                pltpu.make_async_copy(wd_hbm.at[e], wdf_ref.at[s], wsem.at[s]))

    @pl.when(i == 0)
    def _():
        for cp in weight_copies(ie_ref[0], 0):
            cp.start()

    @pl.when(new_ref[i] == 1)
    def _():
        s = slot_ref[i]
        for cp in weight_copies(ie_ref[i], s):
            cp.wait()

        @pl.when(next_ref[i] >= 0)
        def _():
            for cp in weight_copies(next_ref[i], 1 - s):
                cp.start()

        wgb_ref[...] = wgf_ref[s].astype(BF16)
        wub_ref[...] = wuf_ref[s].astype(BF16)
        wdb_ref[...] = wdf_ref[s].astype(BF16)

    @pl.when(i < nit_ref[0])
    def _():
        xl, xr = _unpack_halves(x_ref[...])
        xl = xl.astype(BF16)
        xr = xr.astype(BF16)
        half = xl.shape[1]

        def up(w_ref):
            return (jnp.dot(xl, w_ref[:half, :], preferred_element_type=F32)
                    + jnp.dot(xr, w_ref[half:, :], preferred_element_type=F32))

        hg = up(wgb_ref)
        hu = up(wub_ref)
        act = (hg * jax.nn.sigmoid(hg) * hu).astype(BF16)
        y = _pack_halves(jnp.dot(act, wdb_ref[...], preferred_element_type=F32).astype(BF16))
        row = lax.broadcasted_iota(I32, y.shape, 0)
        mine = (row >= lo_ref[i]) & (row < hi_ref[i])

        @pl.when(first_ref[i] == 1)
        def _():
            o_ref[...] = jnp.where(mine, y, jnp.uint32(0))

        @pl.when(first_ref[i] == 0)
        def _():
            o_ref[...] = jnp.where(mine, y, o_ref[...])


def _work_items(counts, n_pairs, bm):
    assert n_pairs % bm == 0
    nblocks = n_pairs // bm
    ends = jnp.cumsum(counts)
    starts = ends - counts
    first_blk = starts // bm
    n_e = jnp.where(counts > 0, (ends - 1) // bm - first_blk + 1, 0)
    item_end = jnp.cumsum(n_e)
    item_start = item_end - n_e
    n_items = item_end[-1]
    i = jnp.arange(nblocks + N_EXPERTS)
    valid = i < n_items
    e = jnp.minimum(jnp.sum(item_end[None, :] <= jnp.minimum(i, n_items - 1)[:, None], axis=1),
                    N_EXPERTS - 1)
    blk = jnp.where(valid, first_blk[e] + i - item_start[e], nblocks - 1)
    lo = jnp.clip(starts[e] - blk * bm, 0, bm)
    hi = jnp.where(valid, jnp.clip(ends[e] - blk * bm, 0, bm), 0)
    first = jnp.concatenate([jnp.ones((1,), I32), (blk[1:] != blk[:-1]).astype(I32)])
    new = jnp.where(valid, jnp.concatenate([jnp.ones((1,), bool), e[1:] != e[:-1]]), False)
    slot = (jnp.cumsum(new) - 1) % 2
    ids = jnp.arange(N_EXPERTS)
    later = jnp.where((counts > 0)[None, :] & (ids[None, :] > ids[:, None]), ids[None, :], N_EXPERTS)
    next_e = jnp.min(later, axis=1)
    nxt = jnp.where(next_e[e] < N_EXPERTS, next_e[e], -1)
    items = tuple(a.astype(I32) for a in (blk, e, lo, hi, first, n_items[None], new, slot, nxt))
    return items, starts.astype(I32)


def _experts(items, xs, w_gate, w_up, w_down, bm):
    n, dh = xs.shape
    d = 2 * dh
    de = w_gate.shape[2]

    def rows(i, ib, *_):
        return (ib[i], 0)

    grid_spec = pltpu.PrefetchScalarGridSpec(
        num_scalar_prefetch=len(items),
        grid=(items[0].shape[0],),
        in_specs=[
            pl.BlockSpec((bm, dh), rows),
            pl.BlockSpec(memory_space=pl.ANY),
            pl.BlockSpec(memory_space=pl.ANY),
            pl.BlockSpec(memory_space=pl.ANY),
        ],
        out_specs=pl.BlockSpec((bm, dh), rows),
        scratch_shapes=[pltpu.VMEM((2, d, de), F32), pltpu.VMEM((2, d, de), F32),
                        pltpu.VMEM((2, de, d), F32),
                        pltpu.VMEM((d, de), BF16), pltpu.VMEM((d, de), BF16),
                        pltpu.VMEM((de, d), BF16), pltpu.SemaphoreType.DMA((2,))],
    )
    return pl.pallas_call(
        _expert_kernel,
        grid_spec=grid_spec,
        out_shape=jax.ShapeDtypeStruct((n, dh), U32),
        compiler_params=_cparams("arbitrary"),
        name="experts",
    )(*items, xs, w_gate, w_up, w_down)


SC_WINDOW = 128
SC_WORKERS = 32


def _sc_worker():
    return lax.axis_index("core") * (SC_WORKERS // 2) + lax.axis_index("subcore")


def _sc_scatter_rows(x, idx, tm):
    t, dh = x.shape
    n = idx.shape[0]
    assert n == t * TOP_K and tm % SC_WINDOW == 0
    windows = t // SC_WINDOW
    per = windows // SC_WORKERS
    assert windows % SC_WORKERS == 0
    wpt = tm // SC_WINDOW
    mesh = plsc.VectorSubcoreMesh(core_axis_name="core", subcore_axis_name="subcore")

    @pl.kernel(out_type=jax.ShapeDtypeStruct((n, dh), x.dtype), mesh=mesh,
               scratch_types=[pltpu.VMEM((SC_WINDOW,), I32), pltpu.VMEM((SC_WINDOW, dh), x.dtype)])
    def scatter(x_hbm, idx_hbm, out_hbm, idx_vmem, rows_vmem):
        worker = _sc_worker()

        @pl.loop(0, per)
        def _(j):
            w = worker * per + j
            tile = w // wpt
            off = (w - tile * wpt) * SC_WINDOW
            pltpu.sync_copy(x_hbm.at[pl.ds(w * SC_WINDOW, SC_WINDOW)], rows_vmem)

            @pl.loop(0, TOP_K)
            def _(k):
                base = (tile * TOP_K + k) * tm + off
                pltpu.sync_copy(idx_hbm.at[pl.ds(base, SC_WINDOW)], idx_vmem)
                pltpu.sync_copy(rows_vmem, out_hbm.at[idx_vmem])

    return scatter(x, idx)


def _sc_gather_rows(src, idx):
    n = idx.shape[0]
    dh = src.shape[1]
    per = n // SC_WORKERS
    assert per % SC_WINDOW == 0
    mesh = plsc.VectorSubcoreMesh(core_axis_name="core", subcore_axis_name="subcore")

    @pl.kernel(out_type=jax.ShapeDtypeStruct((n, dh), src.dtype), mesh=mesh,
               scratch_types=[pltpu.VMEM((SC_WINDOW,), I32), pltpu.VMEM((SC_WINDOW, dh), src.dtype)])
    def gather(src_hbm, idx_hbm, out_hbm, idx_vmem, rows_vmem):
        worker = _sc_worker()

        @pl.loop(0, per // SC_WINDOW)
        def _(j):
            base = worker * per + j * SC_WINDOW
            pltpu.sync_copy(idx_hbm.at[pl.ds(base, SC_WINDOW)], idx_vmem)
            pltpu.sync_copy(src_hbm.at[idx_vmem], rows_vmem)
            pltpu.sync_copy(rows_vmem, out_hbm.at[pl.ds(base, SC_WINDOW)])

    return gather(src, idx)


def _shared_ffn_kernel(xb_ref, wg_ref, wu_ref, wd_ref, o_ref):
    xb = xb_ref[...]
    hg = jnp.dot(xb, wg_ref[...], preferred_element_type=F32)
    hu = jnp.dot(xb, wu_ref[...], preferred_element_type=F32)
    act = (hg * jax.nn.sigmoid(hg) * hu).astype(BF16)
    o_ref[...] = jnp.dot(act, wd_ref[...], preferred_element_type=F32)


def _shared_ffn(x1b, wg, wu, wd, tm):
    t, d = x1b.shape
    ds = wg.shape[1]
    return pl.pallas_call(
        _shared_ffn_kernel,
        grid=(t // tm,),
        in_specs=[pl.BlockSpec((tm, d), lambda i: (i, 0)), pl.BlockSpec((d, ds), lambda i: (0, 0)),
                  pl.BlockSpec((d, ds), lambda i: (0, 0)), pl.BlockSpec((ds, d), lambda i: (0, 0))],
        out_specs=pl.BlockSpec((tm, d), lambda i: (i, 0)),
        out_shape=jax.ShapeDtypeStruct((t, d), F32),
        compiler_params=_cparams("parallel"),
        name="shared_ffn",
    )(x1b, wg, wu, wd)


def _ffn_out_kernel(yt_ref, sh_ref, x1_ref, wt_ref, lg_ref, lb_ref, o_ref):
    sh = sh_ref[...]
    wt = wt_ref[...]
    rl = None
    rr = None
    for k in range(TOP_K):
        hi, lo = _unpack_halves(yt_ref[0, k])
        wk = wt[:, k:k + 1]
        rl = wk * hi if rl is None else rl + wk * hi
        rr = wk * lo if rr is None else rr + wk * lo
    z = ALPHA * x1_ref[...] + sh + jnp.concatenate([rl, rr], axis=1)
    o_ref[...] = _layer_norm(z, lg_ref[...], lb_ref[...])


def _ffn_out(yt, sh, x1, wt, lg, lb, tm):
    t, d = x1.shape
    dh = yt.shape[-1]
    assert yt.shape == (t // tm, TOP_K, tm, dh)

    def rows(i):
        return (i, 0)

    def whole(i):
        return (0, 0)

    return pl.pallas_call(
        _ffn_out_kernel,
        grid=(t // tm,),
        in_specs=[
            pl.BlockSpec((1, TOP_K, tm, dh), lambda i: (i, 0, 0, 0)),
            pl.BlockSpec((tm, d), rows), pl.BlockSpec((tm, d), rows), pl.BlockSpec((tm, LANES), rows),
            pl.BlockSpec((1, d), whole), pl.BlockSpec((1, d), whole),
        ],
        out_specs=pl.BlockSpec((tm, d), rows),
        out_shape=jax.ShapeDtypeStruct((t, d), F32),
        compiler_params=_cparams("parallel"),
        name="ffn_out_ln2",
    )(yt, sh, x1, wt, lg, lb)


def _route_kernel(xb_ref, wrt_ref, rb_ref, ek_ref, rk_ref, wt_ref, cnt_ref, upper_ref, run_ref, *,
                  tm):
    i = pl.program_id(0)
    gsz = N_EXPERTS // N_GROUP
    ninf = -jnp.inf

    @pl.when(i == 0)
    def _():
        r = lax.broadcasted_iota(I32, (tm, tm), 0)
        c = lax.broadcasted_iota(I32, (tm, tm), 1)
        upper_ref[...] = jnp.where(r < c, 1.0, 0.0).astype(BF16)
        run_ref[...] = jnp.zeros(run_ref.shape, F32)

    logits = lax.dot_general(wrt_ref[...], xb_ref[...], (((1,), (1,)), ((), ())),
                             preferred_element_type=F32)
    scores = jax.nn.sigmoid(logits)
    choice = scores + rb_ref[...]

    ridx = lax.broadcasted_iota(I32, (gsz, tm), 0)
    gscore = []
    for g in range(N_GROUP):
        blk = choice[g * gsz:(g + 1) * gsz, :]
        m1 = jnp.max(blk, axis=0, keepdims=True)
        i1 = jnp.min(jnp.where(blk == m1, ridx, gsz), axis=0, keepdims=True)
        m2 = jnp.max(jnp.where(ridx == i1, ninf, blk), axis=0, keepdims=True)
        gscore.append(m1 + m2)
    masked = []
    for g in range(N_GROUP):
        beaten = jnp.zeros((1, tm), I32)
        for g2 in range(N_GROUP):
            if g2 == g:
                continue
            wins = (gscore[g2] >= gscore[g]) if g2 < g else (gscore[g2] > gscore[g])
            beaten = beaten + jnp.where(wins, 1, 0)
        masked.append(jnp.where(beaten < TOPK_GROUP, choice[g * gsz:(g + 1) * gsz, :], ninf))
    v = jnp.concatenate(masked, axis=0)

    eidx = lax.broadcasted_iota(I32, (N_EXPERTS, tm), 0)
    sel = jnp.zeros((N_EXPERTS, tm), F32)
    e_rows = []
    s_rows = []
    for k in range(TOP_K):
        m = jnp.max(v, axis=0, keepdims=True)
        ik = jnp.min(jnp.where(v == m, eidx, N_EXPERTS), axis=0, keepdims=True)
        hit = eidx == ik
        e_rows.append(ik)
        s_rows.append(jnp.sum(jnp.where(hit, scores, 0.0), axis=0, keepdims=True))
        v = jnp.where(hit, ninf, v)
        sel = jnp.where(hit, 1.0, sel)

    prefix = jnp.dot(sel.astype(BF16), upper_ref[...], preferred_element_type=F32)
    pos = prefix + run_ref[...]
    for k in range(TOP_K):
        rk = jnp.sum(jnp.where(eidx == e_rows[k], pos, 0.0), axis=0, keepdims=True)
        ek_ref[k:k + 1, :] = e_rows[k]
        rk_ref[k:k + 1, :] = rk.astype(I32)
    run_ref[...] = run_ref[...] + jnp.sum(sel, axis=1, keepdims=True)
    cnt_ref[...] = jnp.broadcast_to(run_ref[...], cnt_ref.shape).astype(I32)

    ssum = s_rows[0]
    for k in range(1, TOP_K):
        ssum = ssum + s_rows[k]
    w_rows = [s / (ssum + 1e-20) * ROUTED_SCALE for s in s_rows]
    w_rows.append(jnp.zeros((LANES - TOP_K, tm), F32))
    wt_ref[...] = jnp.concatenate(w_rows, axis=0).T


def _route(x1b, wrt, rbias, tm):
    t, d = x1b.shape
    return pl.pallas_call(
        functools.partial(_route_kernel, tm=tm),
        grid=(t // tm,),
        in_specs=[pl.BlockSpec((tm, d), lambda i: (i, 0)),
                  pl.BlockSpec((N_EXPERTS, d), lambda i: (0, 0)),
                  pl.BlockSpec((N_EXPERTS, 1), lambda i: (0, 0))],
        out_specs=[pl.BlockSpec((TOP_K, tm), lambda i: (0, i)),
                   pl.BlockSpec((TOP_K, tm), lambda i: (0, i)),
                   pl.BlockSpec((tm, LANES), lambda i: (i, 0)),
                   pl.BlockSpec((N_EXPERTS, LANES), lambda i: (0, 0))],
        out_shape=[jax.ShapeDtypeStruct((TOP_K, t), I32), jax.ShapeDtypeStruct((TOP_K, t), I32),
                   jax.ShapeDtypeStruct((t, LANES), F32),
                   jax.ShapeDtypeStruct((N_EXPERTS, LANES), I32)],
        scratch_shapes=[pltpu.VMEM((tm, tm), BF16), pltpu.VMEM((N_EXPERTS, 1), F32)],
        compiler_params=_cparams("arbitrary"),
        name="route",
    )(x1b, wrt, rbias)


def _dest_kernel(ps_ref, ek_ref, rk_ref, d_ref, *, tm):
    e = ek_ref[...]

    def body(j, acc):
        return acc + jnp.where(e == j, ps_ref[j], 0)

    res = lax.fori_loop(0, N_EXPERTS, body, jnp.zeros(e.shape, I32)) + rk_ref[...]
    for a in range(e.shape[1] // tm):
        for k in range(TOP_K):
            d_ref[a:a + 1, k * tm:(k + 1) * tm] = res[k:k + 1, a * tm:(a + 1) * tm]


def _dest(pstart, ek, rk, tm, tw):
    t = ek.shape[1]
    grid_spec = pltpu.PrefetchScalarGridSpec(
        num_scalar_prefetch=1,
        grid=(t // tw,),
        in_specs=[pl.BlockSpec((TOP_K, tw), lambda i, ps: (0, i)),
                  pl.BlockSpec((TOP_K, tw), lambda i, ps: (0, i))],
        out_specs=pl.BlockSpec((tw // tm, TOP_K * tm), lambda i, ps: (i, 0)),
    )
    return pl.pallas_call(
        functools.partial(_dest_kernel, tm=tm),
        grid_spec=grid_spec,
        out_shape=jax.ShapeDtypeStruct((t // tm, TOP_K * tm), I32),
        compiler_params=_cparams("parallel"),
        name="dest",
    )(pstart, ek, rk)


def _layer(x, mem, rel_bias, w_in, b_in, conv_w, diff_lambda, subln_g, mlstm_norm_g, w_mem_kv,
           w_branch, w_out, ln1_g, ln1_b, w_router, router_bias, w_e_gate, w_e_up, w_e_down,
           w_s_gate, w_s_up, w_s_down, ln2_g, ln2_b, layer_idx, cfg):
    b, s, d = x.shape
    t = b * s
    x2 = x.reshape(t, d)

    g0 = (OFF_MLO + 8) * LANES
    w_main = jnp.concatenate([w_in[:, :g0], w_in[:, g0 + 2 * ML_HEADS:]], axis=1).astype(BF16)
    b_main = jnp.concatenate([b_in[:g0], b_in[g0 + 2 * ML_HEADS:]])[None, :]
    w_g = jnp.pad(w_in[:, g0:g0 + 2 * ML_HEADS], ((0, 0), (0, LANES - 2 * ML_HEADS))).astype(BF16)
    b_g = jnp.pad(b_in[g0:g0 + 2 * ML_HEADS], (0, LANES - 2 * ML_HEADS))[None, :]

    u2, gates2 = _proj_in(x2, w_main, b_main, w_g, b_g, cfg["proj_tm"], cfg["proj_tn"])
    u3 = u2.reshape(b, s, N_MAIN)
    gates3 = gates2.reshape(b, s, LANES)

    lam_init = 0.8 - 0.6 * math.exp(-0.3 * layer_idx)
    y_a = _diff_attention(u3, rel_bias, diff_lambda, subln_g[None, :], cfg["attn_tq"], lam_init)
    y_m = _mlstm(u3, gates3, conv_w, mlstm_norm_g[None, :], cfg["ml_chunk"])
    kv = _mm(mem.reshape(-1, d), w_mem_kv.astype(BF16), BF16, cfg["kv_tm"], "mem_kv")
    y_c = _mem_attention(u3, kv.reshape(b, -1, 2 * MA_HEADS * MA_HEAD_DIM), cfg["ma_tq"])

    x1, x1b, x1p = _merge(y_a.reshape(t, d), y_m.reshape(t, d), y_c.reshape(t, d), u2, x2,
                          w_branch.astype(BF16), w_out.astype(BF16), ln1_g[None, :],
                          ln1_b[None, :], cfg["merge_tm"])

    ek, rk, wt, cnt = _route(x1b, w_router.T.astype(BF16), router_bias.astype(F32)[:, None],
                             cfg["route_tm"])
    bm = cfg["expert_bm"]
    items, starts = _work_items(cnt[:, 0], t * TOP_K, bm)
    dest = _dest(starts, ek, rk, cfg["moe_tm"], cfg["dest_tw"]).reshape(-1)
    xs = _sc_scatter_rows(x1p, dest, cfg["moe_tm"])
    sh = _shared_ffn(x1b, w_s_gate.astype(BF16), w_s_up.astype(BF16), w_s_down.astype(BF16),
                     cfg["shared_tm"])
    ys = _experts(items, xs, w_e_gate, w_e_up, w_e_down, bm)
    tm = cfg["moe_tm"]
    yt = _sc_gather_rows(ys, dest).reshape(t // tm, TOP_K, tm, d // 2)
    out = _ffn_out(yt, sh, x1, wt, ln2_g[None, :], ln2_b[None, :], tm)
    return out.reshape(b, s, d)


def _config(b, s):
    t = b * s
    return {
        "proj_tm": min(1024, t), "proj_tn": 1024,
        "attn_tq": 256, "ml_chunk": 256, "kv_tm": 512, "ma_tq": min(512, s),
        "merge_tm": 256, "route_tm": 512, "expert_bm": 512, "moe_tm": 256, "dest_tw": 2048,
        "shared_tm": 512,
    }


def kernel(x, mem, rel_bias, w_in, b_in, conv_w, diff_lambda, subln_g, mlstm_norm_g, w_mem_kv,
           w_branch, w_out, ln1_g, ln1_b, w_router, router_bias, w_e_gate, w_e_up, w_e_down,
           w_s_gate, w_s_up, w_s_down, ln2_g, ln2_b):
    cfg = _config(x.shape[0], x.shape[1])
    for l in range(DEPTH):
        x = _layer(x, mem, rel_bias, w_in[l], b_in[l], conv_w[l], diff_lambda[l], subln_g[l],
                   mlstm_norm_g[l], w_mem_kv[l], w_branch[l], w_out[l], ln1_g[l], ln1_b[l],
                   w_router[l], router_bias[l], w_e_gate[l], w_e_up[l], w_e_down[l],
                   w_s_gate[l], w_s_up[l], w_s_down[l], ln2_g[l], ln2_b[l], l, cfg)
    return x
```

```python
import functools
import math

import numpy as np
import jax
import jax.numpy as jnp
from jax import lax
from jax.experimental import pallas as pl
from jax.experimental.pallas import tpu as pltpu
from jax.experimental.pallas import tpu_sc as plsc

F32 = jnp.float32
BF16 = jnp.bfloat16
U32 = jnp.uint32
I32 = jnp.int32

D_MODEL = 1024
DEPTH = 1
DA_HEAD_DIM = 64
DA_V_DIM = 128
DA_HEADS = 8
ML_HEADS = 4
ML_HEAD_DIM = 256
ML_CONV = 4
ML_GROUP = 2
MA_HEADS = 4
MA_HEAD_DIM = 256
N_BRANCH = 3
REL_BUCKETS = 32
REL_MAX_DIST = 128
N_EXPERTS = 256
TOP_K = 8
N_GROUP = 8
TOPK_GROUP = 4
D_EXPERT = 256
ROUTED_SCALE = 2.5
ALPHA = (2.0 * DEPTH) ** 0.25

LANES = 128
NEG = -1e30
LOG2E = math.log2(math.e)
FOLD_ACCS = 4
VMEM_LIMIT = 56 * 1024 * 1024

OFF_DAQ, OFF_DAK, OFF_DAV = 0, 8, 16
OFF_MLQ, OFF_MLK, OFF_MLV, OFF_MLO, OFF_MAQ, OFF_GATE = 24, 32, 40, 48, 56, 64
N_MAIN = 88 * LANES


def _cparams(*sem):
    return pltpu.CompilerParams(dimension_semantics=sem, vmem_limit_bytes=VMEM_LIMIT)


def _sigmoid(x):
    return 0.5 + 0.5 * jnp.tanh(0.5 * x)


def _layer_norm(z, g, b):
    mu = jnp.mean(z, axis=-1, keepdims=True)
    zc = z - mu
    var = jnp.mean(zc * zc, axis=-1, keepdims=True)
    return zc * lax.rsqrt(var + 1e-5) * g + b


def _proj_in_kernel(x_ref, w_ref, b_ref, wg_ref, bg_ref, u_ref, g_ref, xs_ref):
    @pl.when(pl.program_id(1) == 0)
    def _():
        xb = x_ref[...].astype(BF16)
        xs_ref[...] = xb
        g_ref[...] = jnp.dot(xb, wg_ref[...], preferred_element_type=F32) + bg_ref[...]

    acc = jnp.dot(xs_ref[...], w_ref[...], preferred_element_type=F32)
    u_ref[...] = (acc + b_ref[...]).astype(u_ref.dtype)


def _proj_in(x2, w_main, b_main, w_g, b_g, tm, tn):
    t, k = x2.shape
    n = w_main.shape[1]
    return pl.pallas_call(
        _proj_in_kernel,
        grid=(t // tm, n // tn),
        in_specs=[
            pl.BlockSpec((tm, k), lambda i, j: (i, 0)),
            pl.BlockSpec((k, tn), lambda i, j: (0, j)),
            pl.BlockSpec((1, tn), lambda i, j: (0, j)),
            pl.BlockSpec((k, LANES), lambda i, j: (0, 0)),
            pl.BlockSpec((1, LANES), lambda i, j: (0, 0)),
        ],
        out_specs=[
            pl.BlockSpec((tm, tn), lambda i, j: (i, j)),
            pl.BlockSpec((tm, LANES), lambda i, j: (i, 0)),
        ],
        out_shape=[jax.ShapeDtypeStruct((t, n), BF16), jax.ShapeDtypeStruct((t, LANES), F32)],
        scratch_shapes=[pltpu.VMEM((tm, k), BF16)],
        compiler_params=_cparams("parallel", "arbitrary"),
        name="proj_in",
    )(x2, w_main, b_main, w_g, b_g)


def _mm_kernel(x_ref, w_ref, o_ref):
    o_ref[...] = jnp.dot(x_ref[...].astype(BF16), w_ref[...],
                         preferred_element_type=F32).astype(o_ref.dtype)


def _mm(x2, w, out_dtype, tm, name):
    m, k = x2.shape
    n = w.shape[1]
    return pl.pallas_call(
        _mm_kernel,
        grid=(m // tm,),
        in_specs=[pl.BlockSpec((tm, k), lambda i: (i, 0)), pl.BlockSpec((k, n), lambda i: (0, 0))],
        out_specs=pl.BlockSpec((tm, n), lambda i: (i, 0)),
        out_shape=jax.ShapeDtypeStruct((m, n), out_dtype),
        compiler_params=_cparams("parallel"),
        name=name,
    )(x2, w)


def _fold8(x, op):
    n = x.shape[0] // 8
    accs = [x[8 * a:8 * a + 8, :] for a in range(min(FOLD_ACCS, n))]
    for a in range(FOLD_ACCS, n):
        accs[a % FOLD_ACCS] = op(accs[a % FOLD_ACCS], x[8 * a:8 * a + 8, :])
    while len(accs) > 1:
        accs = [op(accs[a], accs[a + 1]) for a in range(0, len(accs), 2)]
    return accs[0]


def _dattn_kernel(rb_ref, q_ref, k_ref, v_ref, bkt_ref, dl_ref, g_ref, o_ref,
                  st_ref, pt_ref, vt_ref, nb_ref, *, tq, seq, lam_init):
    h = pl.program_id(0)
    nq = seq // tq
    nt = (((1,), (1,)), ((), ()))

    @pl.when(pl.program_id(1) == 0)
    def _():
        far = rb_ref[REL_BUCKETS - 1, h]
        for t in range(2):
            bk = bkt_ref[t]
            tile = jnp.full((tq, tq), NEG, F32)
            for bb in range(REL_BUCKETS):
                tile = jnp.where(bk == bb, (rb_ref[bb, h] - far) * LOG2E, tile)
            nb_ref[t * tq:(t + 1) * tq, :] = tile

    for j in range(nq):
        vt_ref[:, j * tq:(j + 1) * tq] = v_ref[0, j * tq:(j + 1) * tq, :].astype(F32).T.astype(BF16)

    dl = dl_ref[...]
    lam = (jnp.exp(jnp.sum(dl[0:1] * dl[1:2], axis=1, keepdims=True))
           - jnp.exp(jnp.sum(dl[2:3] * dl[3:4], axis=1, keepdims=True)) + lam_init)
    lane = lax.broadcasted_iota(I32, (tq, LANES), 1)
    gain = g_ref[...] * (1.0 - lam_init)

    def logits_pass(i):
        qs = q_ref[0, i * tq:(i + 1) * tq, :] * jnp.asarray(DA_HEAD_DIM ** -0.5 * LOG2E, BF16)
        zero = jnp.zeros_like(qs)
        qm = (jnp.where(lane < DA_HEAD_DIM, qs, zero), jnp.where(lane >= DA_HEAD_DIM, qs, zero))
        mx = [None, None]

        def chunk(c):
            for half in range(2):
                s = lax.dot_general(k_ref[0, c * tq:(c + 1) * tq, :], qm[half], nt,
                                    preferred_element_type=F32)
                if c >= i - 1:
                    s = s + nb_ref[(c - i + 1) * tq:(c - i + 2) * tq, :]
                st_ref[2 * (i % 2) + half, c * tq:(c + 1) * tq, :] = s
                f = _fold8(s, jnp.maximum)
                mx[half] = f if mx[half] is None else jnp.maximum(mx[half], f)

        def result():
            return [jnp.max(mx[half], axis=0, keepdims=True) for half in range(2)]

        return [functools.partial(chunk, c) for c in range(i + 1)], result

    def exp_pass(i, m):
        l8 = [None, None]

        def chunk(c):
            for half in range(2):
                p = jnp.exp2(st_ref[2 * (i % 2) + half, c * tq:(c + 1) * tq, :] - m[half])
                pt_ref[2 * (i % 2) + half, c * tq:(c + 1) * tq, :] = p.astype(BF16)
                f = _fold8(p, jnp.add)
                l8[half] = f if l8[half] is None else l8[half] + f

        def result():
            return [jnp.sum(l8[half], axis=0, keepdims=True) for half in range(2)]

        return [functools.partial(chunk, c) for c in range(i + 1)], result

    def value_pass(i, l1, l2):
        kv = (i + 1) * tq
        acc = [jnp.dot(vt_ref[:, 0:kv], pt_ref[2 * (i % 2) + half, 0:kv, :],
                       preferred_element_type=F32) for half in range(2)]
        o = (acc[0] * (1.0 / l1) - acc[1] * (lam / l2)).T
        ms = jnp.mean(o * o, axis=1, keepdims=True)
        y = o * lax.rsqrt(ms + 1e-5) * gain
        o_ref[0, i * tq:(i + 1) * tq, :] = y.astype(o_ref.dtype)

    chunks, result = logits_pass(0)
    for run in chunks:
        run()
    m_next = result()
    for i in range(nq):
        exp_chunks, exp_result = exp_pass(i, m_next)
        next_chunks, next_result = logits_pass(i + 1) if i + 1 < nq else ([], None)
        for c in range(max(len(exp_chunks), len(next_chunks))):
            if c < len(next_chunks):
                next_chunks[c]()
            if c < len(exp_chunks):
                exp_chunks[c]()
        if next_result is not None:
            m_next = next_result()
        value_pass(i, *exp_result())


def _t5_bucket_np(dist):
    n = np.maximum(dist, 0)
    max_exact = REL_BUCKETS // 2
    large = max_exact + (np.log(np.maximum(n, 1).astype(np.float32) / max_exact)
                         / math.log(REL_MAX_DIST / max_exact)
                         * (REL_BUCKETS - max_exact)).astype(np.int32)
    large = np.minimum(large, REL_BUCKETS - 1)
    return np.where(n < max_exact, n, large)


def _near_buckets(tq):
    assert _t5_bucket_np(np.array([tq + 1]))[0] == REL_BUCKETS - 1
    c = np.arange(tq)[:, None]
    r = np.arange(tq)[None, :]
    left = _t5_bucket_np(r + tq - c)
    diag = np.where(r >= c, _t5_bucket_np(r - c), -1)
    return np.stack([left, diag], axis=0).astype(np.int32)


def _diff_attention(u3, rel_bias, diff_lambda, subln_g, tq, lam_init):
    b, s, _ = u3.shape
    kern = functools.partial(_dattn_kernel, tq=tq, seq=s, lam_init=lam_init)
    grid_spec = pltpu.PrefetchScalarGridSpec(
        num_scalar_prefetch=1,
        grid=(DA_HEADS, b),
        in_specs=[
            pl.BlockSpec((1, s, LANES), lambda h, bi, rb: (bi, 0, OFF_DAQ + h)),
            pl.BlockSpec((1, s, LANES), lambda h, bi, rb: (bi, 0, OFF_DAK + h)),
            pl.BlockSpec((1, s, LANES), lambda h, bi, rb: (bi, 0, OFF_DAV + h)),
            pl.BlockSpec((2, tq, tq), lambda h, bi, rb: (0, 0, 0)),
            pl.BlockSpec((4, DA_HEAD_DIM), lambda h, bi, rb: (0, 0)),
            pl.BlockSpec((1, DA_V_DIM), lambda h, bi, rb: (0, 0)),
        ],
        out_specs=pl.BlockSpec((1, s, LANES), lambda h, bi, rb: (bi, 0, h)),
        scratch_shapes=[pltpu.VMEM((4, s, tq), F32), pltpu.VMEM((4, s, tq), BF16),
                        pltpu.VMEM((DA_V_DIM, s), BF16),
                        pltpu.VMEM((2 * tq, tq), F32)],
    )
    return pl.pallas_call(
        kern,
        grid_spec=grid_spec,
        out_shape=jax.ShapeDtypeStruct((b, s, DA_HEADS * DA_V_DIM), BF16),
        compiler_params=_cparams("arbitrary", "arbitrary"),
        name="diff_attn",
    )(rel_bias.astype(F32), u3, u3, u3, jnp.asarray(_near_buckets(tq)), diff_lambda, subln_g)


def _mlstm_kernel(q_ref, k_ref, v_ref, og_ref, g_ref, cwq_ref, cwk_ref, ng_ref, y_ref,
                  ct_ref, gts_ref, *, chunk, seq):
    first_head = pl.program_id(1) * ML_GROUP
    nc = seq // chunk
    hd = ML_HEAD_DIM
    lane = lax.broadcasted_iota(jnp.int32, (chunk, LANES), 1)
    row = lax.broadcasted_iota(jnp.int32, (chunk, chunk), 0)
    col = lax.broadcasted_iota(jnp.int32, (chunk, chunk), 1)
    causal = row >= col
    tri = causal.astype(F32)
    ct_ref[...] = jnp.zeros(ct_ref.shape, F32)

    def conv_silu(ref, cw_ref, r0, c, cols):
        cw = cw_ref[:, cols]
        cur = ref[0, pl.ds(r0, chunk), cols].astype(F32)
        p0 = pl.multiple_of(jnp.maximum(r0 - 16, 0), 16)
        prev = ref[0, pl.ds(p0, 16), cols].astype(F32)
        prev = jnp.where(c > 0, prev, 0.0)
        x = jnp.concatenate([prev, cur], axis=0)
        out = cw[ML_CONV - 1:ML_CONV] * cur
        for tap in range(ML_CONV - 1):
            shifted = pltpu.roll(x, ML_CONV - 1 - tap, 0)[16:]
            out = out + cw[tap:tap + 1] * shifted
        return out * _sigmoid(out)

    def head_step(j, c, r0, g, bc, m, n):
        h = first_head + j
        cols = slice(j * hd, (j + 1) * hd)
        irow = gts_ref[pl.ds(h, 1), :]
        brow = gts_ref[pl.ds(ML_HEADS + h, 1), :]
        bcol = jnp.sum(jnp.where(lane == ML_HEADS + h, bc, 0.0), axis=1, keepdims=True)
        icol = jnp.sum(jnp.where(lane == h, g, 0.0), axis=1, keepdims=True)

        dmat = jnp.where(causal, bcol - brow + irow, NEG)
        inter = bcol + m
        m_row = jnp.maximum(inter, jnp.max(dmat, axis=1, keepdims=True))

        q = conv_silu(q_ref, cwq_ref, r0, c, cols)
        k = conv_silu(k_ref, cwk_ref, r0, c, cols) * (hd ** -0.5)
        qb = q.astype(BF16)
        kb = k.astype(BF16)
        vb = v_ref[0, pl.ds(r0, chunk), cols]
        sqk = lax.dot_general(qb, kb, (((1,), (1,)), ((), ())), preferred_element_type=F32)
        w = jnp.exp(dmat - m_row) * sqk
        inter_w = jnp.exp(inter - m_row)
        ct = ct_ref[j]
        num = (inter_w * jnp.dot(qb, ct.astype(BF16), preferred_element_type=F32)
               + jnp.dot(w.astype(BF16), vb, preferred_element_type=F32))
        den = inter_w * jnp.sum(q * n, axis=1, keepdims=True) + jnp.sum(w, axis=1, keepdims=True)
        hout = num / jnp.maximum(jnp.abs(den), jnp.exp(-m_row))
        mu = jnp.mean(hout, axis=1, keepdims=True)
        hc = hout - mu
        var = jnp.mean(hc * hc, axis=1, keepdims=True)
        hn = hc * lax.rsqrt(var + 1e-5) * ng_ref[:, cols]
        og = og_ref[0, pl.ds(r0, chunk), cols].astype(F32)
        y_ref[0, pl.ds(r0, chunk), cols] = (_sigmoid(og) * hn).astype(y_ref.dtype)

        total = brow[:, chunk - 1:chunk]
        grow = total - brow + irow
        m_new = jnp.maximum(total + m, jnp.max(grow, axis=1, keepdims=True))
        decay = jnp.exp(total + m - m_new)
        ws = jnp.exp(total - bcol + icol - m_new)
        wsv = (ws * vb.astype(F32)).astype(BF16)
        ct_ref[j] = decay * ct + jnp.dot(k.T.astype(BF16), wsv, preferred_element_type=F32)
        n_new = decay * n + jnp.sum(ws * k, axis=0, keepdims=True)
        return m_new, n_new

    def body(c, carry):
        r0 = pl.multiple_of(c * chunk, chunk)
        g = g_ref[0, pl.ds(r0, chunk), :]
        logf = jnp.minimum(g, 0.0) - jnp.log(1.0 + jnp.exp(-jnp.abs(g)))
        bc = jnp.dot(tri, logf, precision=lax.Precision.HIGHEST, preferred_element_type=F32)
        gts_ref[...] = jnp.where(lane < ML_HEADS, g, bc).T
        return tuple(head_step(j, c, r0, g, bc, *carry[j]) for j in range(ML_GROUP))

    init = tuple((jnp.zeros((1, 1), F32), jnp.zeros((1, hd), F32)) for _ in range(ML_GROUP))
    lax.fori_loop(0, nc, body, init)


def _mlstm(u3, gates3, conv_w, norm_g, chunk):
    b, s, _ = u3.shape
    gw = ML_GROUP * ML_HEAD_DIM
    nq = gw // LANES
    ngroups = ML_HEADS // ML_GROUP
    kern = functools.partial(_mlstm_kernel, chunk=chunk, seq=s)

    def ublock(off):
        return pl.BlockSpec((1, s, gw), lambda bi, h: (bi, 0, off // nq + h))

    return pl.pallas_call(
        kern,
        grid=(b, ngroups),
        in_specs=[
            ublock(OFF_MLQ), ublock(OFF_MLK), ublock(OFF_MLV), ublock(OFF_MLO),
            pl.BlockSpec((1, s, LANES), lambda bi, h: (bi, 0, 0)),
            pl.BlockSpec((ML_CONV, gw), lambda bi, h: (0, h)),
            pl.BlockSpec((ML_CONV, gw), lambda bi, h: (0, ngroups + h)),
            pl.BlockSpec((1, gw), lambda bi, h: (0, h)),
        ],
        out_specs=pl.BlockSpec((1, s, gw), lambda bi, h: (bi, 0, h)),
        out_shape=jax.ShapeDtypeStruct((b, s, ML_HEADS * ML_HEAD_DIM), BF16),
        scratch_shapes=[pltpu.VMEM((ML_GROUP, ML_HEAD_DIM, ML_HEAD_DIM), F32),
                        pltpu.VMEM((LANES, chunk), F32)],
        compiler_params=_cparams("parallel", "arbitrary"),
        name="mlstm",
    )(u3, u3, u3, u3, gates3, conv_w, conv_w, norm_g)


def _memattn_kernel(q_ref, k_ref, v_ref, o_ref, *, tq, seq):
    kb = k_ref[0]
    vb = v_ref[0]
    scale = jnp.asarray(MA_HEAD_DIM ** -0.5, BF16)
    for t in range(seq // tq):
        q = q_ref[0, t * tq:(t + 1) * tq, :] * scale
        s = lax.dot_general(q, kb, (((1,), (1,)), ((), ())), preferred_element_type=F32)
        p = jnp.exp(s - jnp.max(s, axis=1, keepdims=True))
        inv = 1.0 / jnp.sum(p, axis=1, keepdims=True)
        o = jnp.dot(p.astype(BF16), vb, preferred_element_type=F32) * inv
        o_ref[0, t * tq:(t + 1) * tq, :] = o.astype(o_ref.dtype)


def _mem_attention(u3, kv3, tq):
    b, s, _ = u3.shape
    mlen = kv3.shape[1]
    hd = MA_HEAD_DIM
    nq = hd // LANES
    kern = functools.partial(_memattn_kernel, tq=tq, seq=s)
    return pl.pallas_call(
        kern,
        grid=(b, MA_HEADS),
        in_specs=[
            pl.BlockSpec((1, s, hd), lambda bi, h: (bi, 0, OFF_MAQ // nq + h)),
            pl.BlockSpec((1, mlen, hd), lambda bi, h: (bi, 0, h)),
            pl.BlockSpec((1, mlen, hd), lambda bi, h: (bi, 0, MA_HEADS + h)),
        ],
        out_specs=pl.BlockSpec((1, s, hd), lambda bi, h: (bi, 0, h)),
        out_shape=jax.ShapeDtypeStruct((b, s, MA_HEADS * hd), BF16),
        compiler_params=_cparams("parallel", "parallel"),
        name="mem_attn",
    )(u3, kv3, kv3)


def _merge_kernel(ya_ref, ym_ref, yc_ref, g0_ref, g1_ref, g2_ref, x_ref, wb_ref, wo_ref,
                  lg_ref, lb_ref, x1_ref, x1b_ref, x1p_ref):
    acc = None
    for n, (y_ref, g_ref) in enumerate(((ya_ref, g0_ref), (ym_ref, g1_ref), (yc_ref, g2_ref))):
        pr = jnp.dot(y_ref[...], wb_ref[n], preferred_element_type=F32)
        t = jax.nn.sigmoid(g_ref[...].astype(F32)) * pr
        acc = t if acc is None else acc + t
    out = jnp.dot(acc.astype(BF16), wo_ref[...], preferred_element_type=F32)
    x1 = _layer_norm(ALPHA * x_ref[...] + out, lg_ref[...], lb_ref[...])
    x1_ref[...] = x1
    x1b = x1.astype(BF16)
    x1b_ref[...] = x1b
    x1p_ref[...] = _pack_halves(x1b)


def _pack_halves(vb):
    w = vb.shape[1] // 2
    hi = lax.bitcast_convert_type(vb[:, :w].astype(F32), U32)
    lo = lax.bitcast_convert_type(vb[:, w:].astype(F32), U32)
    return hi | (lo >> 16)


def _unpack_halves(u):
    hi = lax.bitcast_convert_type(u & jnp.uint32(0xFFFF0000), F32)
    lo = lax.bitcast_convert_type(u << 16, F32)
    return hi, lo


def _merge(ya, ym, yc, u2, x2, wb, wo, lg, lb, tm):
    t, d = x2.shape
    gb = OFF_GATE * LANES // d

    def rows(i):
        return (i, 0)

    return pl.pallas_call(
        _merge_kernel,
        grid=(t // tm,),
        in_specs=[
            pl.BlockSpec((tm, d), rows), pl.BlockSpec((tm, d), rows), pl.BlockSpec((tm, d), rows),
            pl.BlockSpec((tm, d), lambda i: (i, gb)),
            pl.BlockSpec((tm, d), lambda i: (i, gb + 1)),
            pl.BlockSpec((tm, d), lambda i: (i, gb + 2)),
            pl.BlockSpec((tm, d), rows),
            pl.BlockSpec((N_BRANCH, d, d), lambda i: (0, 0, 0)),
            pl.BlockSpec((d, d), lambda i: (0, 0)),
            pl.BlockSpec((1, d), lambda i: (0, 0)),
            pl.BlockSpec((1, d), lambda i: (0, 0)),
        ],
        out_specs=[pl.BlockSpec((tm, d), rows), pl.BlockSpec((tm, d), rows),
                   pl.BlockSpec((tm, d // 2), rows)],
        out_shape=[jax.ShapeDtypeStruct((t, d), F32), jax.ShapeDtypeStruct((t, d), BF16),
                   jax.ShapeDtypeStruct((t, d // 2), U32)],
        compiler_params=_cparams("parallel"),
        name="merge_ln1",
    )(ya, ym, yc, u2, u2, u2, x2, wb, wo, lg, lb)


def _expert_kernel(ib_ref, ie_ref, lo_ref, hi_ref, first_ref, nit_ref, new_ref, slot_ref, next_ref,
                   x_ref, wg_hbm, wu_hbm, wd_hbm, o_ref, wgf_ref, wuf_ref, wdf_ref,
                   wgb_ref, wub_ref, wdb_ref, wsem):
    i = pl.program_id(0)

    def weight_copies(e, s):
        return (pltpu.make_async_copy(wg_hbm.at[e], wgf_ref.at[s], wsem.at[s]),
                pltpu.make_async_copy(wu_hbm.at[e], wuf_ref.at[s], wsem.at[s]),
                pltpu.make_async_copy(wd_hbm.at[e], wdf_ref.at[s], wsem.at[s]))

    @pl.when(i == 0)
    def _():
        for cp in weight_copies(ie_ref[0], 0):
            cp.start()

    @pl.when(new_ref[i] == 1)
    def _():
        s = slot_ref[i]
        for cp in weight_copies(ie_ref[i], s):
            cp.wait()

        @pl.when(next_ref[i] >= 0)
        def _():
            for cp in weight_copies(next_ref[i], 1 - s):
                cp.start()

        wgb_ref[...] = wgf_ref[s].astype(BF16)
        wub_ref[...] = wuf_ref[s].astype(BF16)
        wdb_ref[...] = wdf_ref[s].astype(BF16)

    @pl.when(i < nit_ref[0])
    def _():
        xl, xr = _unpack_halves(x_ref[...])
        xl = xl.astype(BF16)
        xr = xr.astype(BF16)
        half = xl.shape[1]

        def up(w_ref):
            return (jnp.dot(xl, w_ref[:half, :], preferred_element_type=F32)
                    + jnp.dot(xr, w_ref[half:, :], preferred_element_type=F32))

        hg = up(wgb_ref)
        hu = up(wub_ref)
        act = (hg * jax.nn.sigmoid(hg) * hu).astype(BF16)
        y = _pack_halves(jnp.dot(act, wdb_ref[...], preferred_element_type=F32).astype(BF16))
        row = lax.broadcasted_iota(I32, y.shape, 0)
        mine = (row >= lo_ref[i]) & (row < hi_ref[i])

        @pl.when(first_ref[i] == 1)
        def _():
            o_ref[...] = jnp.where(mine, y, jnp.uint32(0))

        @pl.when(first_ref[i] == 0)
        def _():
            o_ref[...] = jnp.where(mine, y, o_ref[...])


def _work_items(counts, n_pairs, bm):
    assert n_pairs % bm == 0
    nblocks = n_pairs // bm
    ends = jnp.cumsum(counts)
    starts = ends - counts
    first_blk = starts // bm
    n_e = jnp.where(counts > 0, (ends - 1) // bm - first_blk + 1, 0)
    item_end = jnp.cumsum(n_e)
    item_start = item_end - n_e
    n_items = item_end[-1]
    i = jnp.arange(nblocks + N_EXPERTS)
    valid = i < n_items
    e = jnp.minimum(jnp.sum(item_end[None, :] <= jnp.minimum(i, n_items - 1)[:, None], axis=1),
                    N_EXPERTS - 1)
    blk = jnp.where(valid, first_blk[e] + i - item_start[e], nblocks - 1)
    lo = jnp.clip(starts[e] - blk * bm, 0, bm)
    hi = jnp.where(valid, jnp.clip(ends[e] - blk * bm, 0, bm), 0)
    first = jnp.concatenate([jnp.ones((1,), I32), (blk[1:] != blk[:-1]).astype(I32)])
    new = jnp.where(valid, jnp.concatenate([jnp.ones((1,), bool), e[1:] != e[:-1]]), False)
    slot = (jnp.cumsum(new) - 1) % 2
    ids = jnp.arange(N_EXPERTS)
    later = jnp.where((counts > 0)[None, :] & (ids[None, :] > ids[:, None]), ids[None, :], N_EXPERTS)
    next_e = jnp.min(later, axis=1)
    nxt = jnp.where(next_e[e] < N_EXPERTS, next_e[e], -1)
    items = tuple(a.astype(I32) for a in (blk, e, lo, hi, first, n_items[None], new, slot, nxt))
    return items, starts.astype(I32)


def _experts(items, xs, w_gate, w_up, w_down, bm):
    n, dh = xs.shape
    d = 2 * dh
    de = w_gate.shape[2]

    def rows(i, ib, *_):
        return (ib[i], 0)

    grid_spec = pltpu.PrefetchScalarGridSpec(
        num_scalar_prefetch=len(items),
        grid=(items[0].shape[0],),
        in_specs=[
            pl.BlockSpec((bm, dh), rows),
            pl.BlockSpec(memory_space=pl.ANY),
            pl.BlockSpec(memory_space=pl.ANY),
            pl.BlockSpec(memory_space=pl.ANY),
        ],
        out_specs=pl.BlockSpec((bm, dh), rows),
        scratch_shapes=[pltpu.VMEM((2, d, de), F32), pltpu.VMEM((2, d, de), F32),
                        pltpu.VMEM((2, de, d), F32),
                        pltpu.VMEM((d, de), BF16), pltpu.VMEM((d, de), BF16),
                        pltpu.VMEM((de, d), BF16), pltpu.SemaphoreType.DMA((2,))],
    )
    return pl.pallas_call(
        _expert_kernel,
        grid_spec=grid_spec,
        out_shape=jax.ShapeDtypeStruct((n, dh), U32),
        compiler_params=_cparams("arbitrary"),
        name="experts",
    )(*items, xs, w_gate, w_up, w_down)


SC_WINDOW = 128
SC_WORKERS = 32


def _sc_worker():
    return lax.axis_index("core") * (SC_WORKERS // 2) + lax.axis_index("subcore")


def _sc_scatter_rows(x, idx, tm):
    t, dh = x.shape
    n = idx.shape[0]
    assert n == t * TOP_K and tm % SC_WINDOW == 0
    windows = t // SC_WINDOW
    per = windows // SC_WORKERS
    assert windows % SC_WORKERS == 0
    wpt = tm // SC_WINDOW
    mesh = plsc.VectorSubcoreMesh(core_axis_name="core", subcore_axis_name="subcore")

    @pl.kernel(out_type=jax.ShapeDtypeStruct((n, dh), x.dtype), mesh=mesh,
               scratch_types=[pltpu.VMEM((SC_WINDOW,), I32), pltpu.VMEM((SC_WINDOW, dh), x.dtype)])
    def scatter(x_hbm, idx_hbm, out_hbm, idx_vmem, rows_vmem):
        worker = _sc_worker()

        @pl.loop(0, per)
        def _(j):
            w = worker * per + j
            tile = w // wpt
            off = (w - tile * wpt) * SC_WINDOW
            pltpu.sync_copy(x_hbm.at[pl.ds(w * SC_WINDOW, SC_WINDOW)], rows_vmem)

            @pl.loop(0, TOP_K)
            def _(k):
                base = (tile * TOP_K + k) * tm + off
                pltpu.sync_copy(idx_hbm.at[pl.ds(base, SC_WINDOW)], idx_vmem)
                pltpu.sync_copy(rows_vmem, out_hbm.at[idx_vmem])

    return scatter(x, idx)


def _sc_gather_rows(src, idx):
    n = idx.shape[0]
    dh = src.shape[1]
    per = n // SC_WORKERS
    assert per % SC_WINDOW == 0
    mesh = plsc.VectorSubcoreMesh(core_axis_name="core", subcore_axis_name="subcore")

    @pl.kernel(out_type=jax.ShapeDtypeStruct((n, dh), src.dtype), mesh=mesh,
               scratch_types=[pltpu.VMEM((SC_WINDOW,), I32), pltpu.VMEM((SC_WINDOW, dh), src.dtype)])
    def gather(src_hbm, idx_hbm, out_hbm, idx_vmem, rows_vmem):
        worker = _sc_worker()

        @pl.loop(0, per // SC_WINDOW)
        def _(j):
            base = worker * per + j * SC_WINDOW
            pltpu.sync_copy(idx_hbm.at[pl.ds(base, SC_WINDOW)], idx_vmem)
            pltpu.sync_copy(src_hbm.at[idx_vmem], rows_vmem)
            pltpu.sync_copy(rows_vmem, out_hbm.at[pl.ds(base, SC_WINDOW)])

    return gather(src, idx)


def _ffn_out_kernel(yt_ref, xb_ref, x1_ref, wt_ref, wg_ref, wu_ref, wd_ref, lg_ref, lb_ref, o_ref):
    xb = xb_ref[...]
    hg = jnp.dot(xb, wg_ref[...], preferred_element_type=F32)
    hu = jnp.dot(xb, wu_ref[...], preferred_element_type=F32)
    act = (hg * jax.nn.sigmoid(hg) * hu).astype(BF16)
    sh = jnp.dot(act, wd_ref[...], preferred_element_type=F32)
    wt = wt_ref[...]
    rl = None
    rr = None
    for k in range(TOP_K):
        hi, lo = _unpack_halves(yt_ref[0, k])
        wk = wt[:, k:k + 1]
        rl = wk * hi if rl is None else rl + wk * hi
        rr = wk * lo if rr is None else rr + wk * lo
    z = ALPHA * x1_ref[...] + sh + jnp.concatenate([rl, rr], axis=1)
    o_ref[...] = _layer_norm(z, lg_ref[...], lb_ref[...])


def _ffn_out(yt, x1b, x1, wt, wg, wu, wd, lg, lb, tm):
    t, d = x1.shape
    ds = wg.shape[1]
    dh = yt.shape[-1]
    assert yt.shape == (t // tm, TOP_K, tm, dh)

    def rows(i):
        return (i, 0)

    def whole(i):
        return (0, 0)

    return pl.pallas_call(
        _ffn_out_kernel,
        grid=(t // tm,),
        in_specs=[
            pl.BlockSpec((1, TOP_K, tm, dh), lambda i: (i, 0, 0, 0)),
            pl.BlockSpec((tm, d), rows), pl.BlockSpec((tm, d), rows), pl.BlockSpec((tm, LANES), rows),
            pl.BlockSpec((d, ds), whole), pl.BlockSpec((d, ds), whole), pl.BlockSpec((ds, d), whole),
            pl.BlockSpec((1, d), whole), pl.BlockSpec((1, d), whole),
        ],
        out_specs=pl.BlockSpec((tm, d), rows),
        out_shape=jax.ShapeDtypeStruct((t, d), F32),
        compiler_params=_cparams("parallel"),
        name="ffn_out_ln2",
    )(yt, x1b, x1, wt, wg, wu, wd, lg, lb)


def _route_kernel(xb_ref, wrt_ref, rb_ref, ek_ref, rk_ref, wt_ref, cnt_ref, upper_ref, run_ref, *,
                  tm):
    i = pl.program_id(0)
    gsz = N_EXPERTS // N_GROUP
    ninf = -jnp.inf

    @pl.when(i == 0)
    def _():
        r = lax.broadcasted_iota(I32, (tm, tm), 0)
        c = lax.broadcasted_iota(I32, (tm, tm), 1)
        upper_ref[...] = jnp.where(r < c, 1.0, 0.0).astype(BF16)
        run_ref[...] = jnp.zeros(run_ref.shape, F32)

    logits = lax.dot_general(wrt_ref[...], xb_ref[...], (((1,), (1,)), ((), ())),
                             preferred_element_type=F32)
    scores = jax.nn.sigmoid(logits)
    choice = scores + rb_ref[...]

    ridx = lax.broadcasted_iota(I32, (gsz, tm), 0)
    gscore = []
    for g in range(N_GROUP):
        blk = choice[g * gsz:(g + 1) * gsz, :]
        m1 = jnp.max(blk, axis=0, keepdims=True)
        i1 = jnp.min(jnp.where(blk == m1, ridx, gsz), axis=0, keepdims=True)
        m2 = jnp.max(jnp.where(ridx == i1, ninf, blk), axis=0, keepdims=True)
        gscore.append(m1 + m2)
    masked = []
    for g in range(N_GROUP):
        beaten = jnp.zeros((1, tm), I32)
        for g2 in range(N_GROUP):
            if g2 == g:
                continue
            wins = (gscore[g2] >= gscore[g]) if g2 < g else (gscore[g2] > gscore[g])
            beaten = beaten + jnp.where(wins, 1, 0)
        masked.append(jnp.where(beaten < TOPK_GROUP, choice[g * gsz:(g + 1) * gsz, :], ninf))
    v = jnp.concatenate(masked, axis=0)

    eidx = lax.broadcasted_iota(I32, (N_EXPERTS, tm), 0)
    sel = jnp.zeros((N_EXPERTS, tm), F32)
    e_rows = []
    s_rows = []
    for k in range(TOP_K):
        m = jnp.max(v, axis=0, keepdims=True)
        ik = jnp.min(jnp.where(v == m, eidx, N_EXPERTS), axis=0, keepdims=True)
        hit = eidx == ik
        e_rows.append(ik)
        s_rows.append(jnp.sum(jnp.where(hit, scores, 0.0), axis=0, keepdims=True))
        v = jnp.where(hit, ninf, v)
        sel = jnp.where(hit, 1.0, sel)

    prefix = jnp.dot(sel.astype(BF16), upper_ref[...], preferred_element_type=F32)
    pos = prefix + run_ref[...]
    for k in range(TOP_K):
        rk = jnp.sum(jnp.where(eidx == e_rows[k], pos, 0.0), axis=0, keepdims=True)
        ek_ref[k:k + 1, :] = e_rows[k]
        rk_ref[k:k + 1, :] = rk.astype(I32)
    run_ref[...] = run_ref[...] + jnp.sum(sel, axis=1, keepdims=True)
    cnt_ref[...] = jnp.broadcast_to(run_ref[...], cnt_ref.shape).astype(I32)

    ssum = s_rows[0]
    for k in range(1, TOP_K):
        ssum = ssum + s_rows[k]
    w_rows = [s / (ssum + 1e-20) * ROUTED_SCALE for s in s_rows]
    w_rows.append(jnp.zeros((LANES - TOP_K, tm), F32))
    wt_ref[...] = jnp.concatenate(w_rows, axis=0).T


def _route(x1b, wrt, rbias, tm):
    t, d = x1b.shape
    return pl.pallas_call(
        functools.partial(_route_kernel, tm=tm),
        grid=(t // tm,),
        in_specs=[pl.BlockSpec((tm, d), lambda i: (i, 0)),
                  pl.BlockSpec((N_EXPERTS, d), lambda i: (0, 0)),
                  pl.BlockSpec((N_EXPERTS, 1), lambda i: (0, 0))],
        out_specs=[pl.BlockSpec((TOP_K, tm), lambda i: (0, i)),
                   pl.BlockSpec((TOP_K, tm), lambda i: (0, i)),
                   pl.BlockSpec((tm, LANES), lambda i: (i, 0)),
                   pl.BlockSpec((N_EXPERTS, LANES), lambda i: (0, 0))],
        out_shape=[jax.ShapeDtypeStruct((TOP_K, t), I32), jax.ShapeDtypeStruct((TOP_K, t), I32),
                   jax.ShapeDtypeStruct((t, LANES), F32),
                   jax.ShapeDtypeStruct((N_EXPERTS, LANES), I32)],
        scratch_shapes=[pltpu.VMEM((tm, tm), BF16), pltpu.VMEM((N_EXPERTS, 1), F32)],
        compiler_params=_cparams("arbitrary"),
        name="route",
    )(x1b, wrt, rbias)


def _dest_kernel(ps_ref, ek_ref, rk_ref, d_ref, *, tm):
    e = ek_ref[...]

    def body(j, acc):
        return acc + jnp.where(e == j, ps_ref[j], 0)

    res = lax.fori_loop(0, N_EXPERTS, body, jnp.zeros(e.shape, I32)) + rk_ref[...]
    for a in range(e.shape[1] // tm):
        for k in range(TOP_K):
            d_ref[a:a + 1, k * tm:(k + 1) * tm] = res[k:k + 1, a * tm:(a + 1) * tm]


def _dest(pstart, ek, rk, tm, tw):
    t = ek.shape[1]
    grid_spec = pltpu.PrefetchScalarGridSpec(
        num_scalar_prefetch=1,
        grid=(t // tw,),
        in_specs=[pl.BlockSpec((TOP_K, tw), lambda i, ps: (0, i)),
                  pl.BlockSpec((TOP_K, tw), lambda i, ps: (0, i))],
        out_specs=pl.BlockSpec((tw // tm, TOP_K * tm), lambda i, ps: (i, 0)),
    )
    return pl.pallas_call(
        functools.partial(_dest_kernel, tm=tm),
        grid_spec=grid_spec,
        out_shape=jax.ShapeDtypeStruct((t // tm, TOP_K * tm), I32),
        compiler_params=_cparams("parallel"),
        name="dest",
    )(pstart, ek, rk)


def _layer(x, mem, rel_bias, w_in, b_in, conv_w, diff_lambda, subln_g, mlstm_norm_g, w_mem_kv,
           w_branch, w_out, ln1_g, ln1_b, w_router, router_bias, w_e_gate, w_e_up, w_e_down,
           w_s_gate, w_s_up, w_s_down, ln2_g, ln2_b, layer_idx, cfg):
    b, s, d = x.shape
    t = b * s
    x2 = x.reshape(t, d)

    g0 = (OFF_MLO + 8) * LANES
    w_main = jnp.concatenate([w_in[:, :g0], w_in[:, g0 + 2 * ML_HEADS:]], axis=1).astype(BF16)
    b_main = jnp.concatenate([b_in[:g0], b_in[g0 + 2 * ML_HEADS:]])[None, :]
    w_g = jnp.pad(w_in[:, g0:g0 + 2 * ML_HEADS], ((0, 0), (0, LANES - 2 * ML_HEADS))).astype(BF16)
    b_g = jnp.pad(b_in[g0:g0 + 2 * ML_HEADS], (0, LANES - 2 * ML_HEADS))[None, :]

    u2, gates2 = _proj_in(x2, w_main, b_main, w_g, b_g, cfg["proj_tm"], cfg["proj_tn"])
    u3 = u2.reshape(b, s, N_MAIN)
    gates3 = gates2.reshape(b, s, LANES)

    lam_init = 0.8 - 0.6 * math.exp(-0.3 * layer_idx)
    y_a = _diff_attention(u3, rel_bias, diff_lambda, subln_g[None, :], cfg["attn_tq"], lam_init)
    y_m = _mlstm(u3, gates3, conv_w, mlstm_norm_g[None, :], cfg["ml_chunk"])
    kv = _mm(mem.reshape(-1, d), w_mem_kv.astype(BF16), BF16, cfg["kv_tm"], "mem_kv")
    y_c = _mem_attention(u3, kv.reshape(b, -1, 2 * MA_HEADS * MA_HEAD_DIM), cfg["ma_tq"])

    x1, x1b, x1p = _merge(y_a.reshape(t, d), y_m.reshape(t, d), y_c.reshape(t, d), u2, x2,
                          w_branch.astype(BF16), w_out.astype(BF16), ln1_g[None, :],
                          ln1_b[None, :], cfg["merge_tm"])

    ek, rk, wt, cnt = _route(x1b, w_router.T.astype(BF16), router_bias.astype(F32)[:, None],
                             cfg["route_tm"])
    bm = cfg["expert_bm"]
    items, starts = _work_items(cnt[:, 0], t * TOP_K, bm)
    dest = _dest(starts, ek, rk, cfg["moe_tm"], cfg["dest_tw"]).reshape(-1)
    xs = _sc_scatter_rows(x1p, dest, cfg["moe_tm"])
    ys = _experts(items, xs, w_e_gate, w_e_up, w_e_down, bm)
    tm = cfg["moe_tm"]
    yt = _sc_gather_rows(ys, dest).reshape(t // tm, TOP_K, tm, d // 2)
    out = _ffn_out(yt, x1b, x1, wt, w_s_gate.astype(BF16), w_s_up.astype(BF16),
                   w_s_down.astype(BF16), ln2_g[None, :], ln2_b[None, :], tm)
    return out.reshape(b, s, d)


def _config(b, s):
    t = b * s
    return {
        "proj_tm": min(1024, t), "proj_tn": 1024,
        "attn_tq": 256, "ml_chunk": 256, "kv_tm": 512, "ma_tq": min(512, s),
        "merge_tm": 256, "route_tm": 512, "expert_bm": 512, "moe_tm": 256, "dest_tw": 2048,
    }


def kernel(x, mem, rel_bias, w_in, b_in, conv_w, diff_lambda, subln_g, mlstm_norm_g, w_mem_kv,
           w_branch, w_out, ln1_g, ln1_b, w_router, router_bias, w_e_gate, w_e_up, w_e_down,
           w_s_gate, w_s_up, w_s_down, ln2_g, ln2_b):
    cfg = _config(x.shape[0], x.shape[1])
    for l in range(DEPTH):
        x = _layer(x, mem, rel_bias, w_in[l], b_in[l], conv_w[l], diff_lambda[l], subln_g[l],
                   mlstm_norm_g[l], w_mem_kv[l], w_branch[l], w_out[l], ln1_g[l], ln1_b[l],
                   w_router[l], router_bias[l], w_e_gate[l], w_e_up[l], w_e_down[l],
                   w_s_gate[l], w_s_up[l], w_s_down[l], ln2_g[l], ln2_b[l], l, cfg)
    return x
```

```python
import functools
import math

import numpy as np
import jax
import jax.numpy as jnp
from jax import lax
from jax.experimental import pallas as pl
from jax.experimental.pallas import tpu as pltpu
from jax.experimental.pallas import tpu_sc as plsc

F32 = jnp.float32
BF16 = jnp.bfloat16
U32 = jnp.uint32
I32 = jnp.int32

D_MODEL = 1024
DEPTH = 1
DA_HEAD_DIM = 64
DA_V_DIM = 128
DA_HEADS = 8
ML_HEADS = 4
ML_HEAD_DIM = 256
ML_CONV = 4
ML_GROUP = 2
MA_HEADS = 4
MA_HEAD_DIM = 256
N_BRANCH = 3
REL_BUCKETS = 32
REL_MAX_DIST = 128
N_EXPERTS = 256
TOP_K = 8
N_GROUP = 8
TOPK_GROUP = 4
D_EXPERT = 256
ROUTED_SCALE = 2.5
ALPHA = (2.0 * DEPTH) ** 0.25

LANES = 128
NEG = -1e30
LOG2E = math.log2(math.e)
FOLD_ACCS = 4
VMEM_LIMIT = 56 * 1024 * 1024

OFF_DAQ, OFF_DAK, OFF_DAV = 0, 8, 16
OFF_MLQ, OFF_MLK, OFF_MLV, OFF_MLO, OFF_MAQ, OFF_GATE = 24, 32, 40, 48, 56, 64
N_MAIN = 88 * LANES


def _cparams(*sem):
    return pltpu.CompilerParams(dimension_semantics=sem, vmem_limit_bytes=VMEM_LIMIT)


def _sigmoid(x):
    return 0.5 + 0.5 * jnp.tanh(0.5 * x)


def _layer_norm(z, g, b):
    mu = jnp.mean(z, axis=-1, keepdims=True)
    zc = z - mu
    var = jnp.mean(zc * zc, axis=-1, keepdims=True)
    return zc * lax.rsqrt(var + 1e-5) * g + b


def _proj_in_kernel(x_ref, w_ref, b_ref, wg_ref, bg_ref, u_ref, g_ref, xs_ref):
    @pl.when(pl.program_id(1) == 0)
    def _():
        xb = x_ref[...].astype(BF16)
        xs_ref[...] = xb
        g_ref[...] = jnp.dot(xb, wg_ref[...], preferred_element_type=F32) + bg_ref[...]

    acc = jnp.dot(xs_ref[...], w_ref[...], preferred_element_type=F32)
    u_ref[...] = (acc + b_ref[...]).astype(u_ref.dtype)


def _proj_in(x2, w_main, b_main, w_g, b_g, tm, tn):
    t, k = x2.shape
    n = w_main.shape[1]
    return pl.pallas_call(
        _proj_in_kernel,
        grid=(t // tm, n // tn),
        in_specs=[
            pl.BlockSpec((tm, k), lambda i, j: (i, 0)),
            pl.BlockSpec((k, tn), lambda i, j: (0, j)),
            pl.BlockSpec((1, tn), lambda i, j: (0, j)),
            pl.BlockSpec((k, LANES), lambda i, j: (0, 0)),
            pl.BlockSpec((1, LANES), lambda i, j: (0, 0)),
        ],
        out_specs=[
            pl.BlockSpec((tm, tn), lambda i, j: (i, j)),
            pl.BlockSpec((tm, LANES), lambda i, j: (i, 0)),
        ],
        out_shape=[jax.ShapeDtypeStruct((t, n), BF16), jax.ShapeDtypeStruct((t, LANES), F32)],
        scratch_shapes=[pltpu.VMEM((tm, k), BF16)],
        compiler_params=_cparams("parallel", "arbitrary"),
        name="proj_in",
    )(x2, w_main, b_main, w_g, b_g)


def _mm_kernel(x_ref, w_ref, o_ref):
    o_ref[...] = jnp.dot(x_ref[...].astype(BF16), w_ref[...],
                         preferred_element_type=F32).astype(o_ref.dtype)


def _mm(x2, w, out_dtype, tm, name):
    m, k = x2.shape
    n = w.shape[1]
    return pl.pallas_call(
        _mm_kernel,
        grid=(m // tm,),
        in_specs=[pl.BlockSpec((tm, k), lambda i: (i, 0)), pl.BlockSpec((k, n), lambda i: (0, 0))],
        out_specs=pl.BlockSpec((tm, n), lambda i: (i, 0)),
        out_shape=jax.ShapeDtypeStruct((m, n), out_dtype),
        compiler_params=_cparams("parallel"),
        name=name,
    )(x2, w)


def _fold8(x, op):
    n = x.shape[0] // 8
    accs = [x[8 * a:8 * a + 8, :] for a in range(min(FOLD_ACCS, n))]
    for a in range(FOLD_ACCS, n):
        accs[a % FOLD_ACCS] = op(accs[a % FOLD_ACCS], x[8 * a:8 * a + 8, :])
    while len(accs) > 1:
        accs = [op(accs[a], accs[a + 1]) for a in range(0, len(accs), 2)]
    return accs[0]


def _dattn_kernel(rb_ref, q_ref, k_ref, v_ref, bkt_ref, dl_ref, g_ref, o_ref,
                  st_ref, pt_ref, vt_ref, nb_ref, *, tq, seq, lam_init):
    h = pl.program_id(0)
    nq = seq // tq
    nt = (((1,), (1,)), ((), ()))

    @pl.when(pl.program_id(1) == 0)
    def _():
        far = rb_ref[REL_BUCKETS - 1, h]
        for t in range(2):
            bk = bkt_ref[t]
            tile = jnp.full((tq, tq), NEG, F32)
            for bb in range(REL_BUCKETS):
                tile = jnp.where(bk == bb, (rb_ref[bb, h] - far) * LOG2E, tile)
            nb_ref[t * tq:(t + 1) * tq, :] = tile

    for j in range(nq):
        vt_ref[:, j * tq:(j + 1) * tq] = v_ref[0, j * tq:(j + 1) * tq, :].astype(F32).T.astype(BF16)

    dl = dl_ref[...]
    lam = (jnp.exp(jnp.sum(dl[0:1] * dl[1:2], axis=1, keepdims=True))
           - jnp.exp(jnp.sum(dl[2:3] * dl[3:4], axis=1, keepdims=True)) + lam_init)
    lane = lax.broadcasted_iota(I32, (tq, LANES), 1)
    gain = g_ref[...] * (1.0 - lam_init)

    def logits_pass(i):
        qs = q_ref[0, i * tq:(i + 1) * tq, :] * jnp.asarray(DA_HEAD_DIM ** -0.5 * LOG2E, BF16)
        zero = jnp.zeros_like(qs)
        qm = (jnp.where(lane < DA_HEAD_DIM, qs, zero), jnp.where(lane >= DA_HEAD_DIM, qs, zero))
        mx = [None, None]

        def chunk(c):
            for half in range(2):
                s = lax.dot_general(k_ref[0, c * tq:(c + 1) * tq, :], qm[half], nt,
                                    preferred_element_type=F32)
                if c >= i - 1:
                    s = s + nb_ref[(c - i + 1) * tq:(c - i + 2) * tq, :]
                st_ref[2 * (i % 2) + half, c * tq:(c + 1) * tq, :] = s
                f = _fold8(s, jnp.maximum)
                mx[half] = f if mx[half] is None else jnp.maximum(mx[half], f)

        def result():
            return [jnp.max(mx[half], axis=0, keepdims=True) for half in range(2)]

        return [functools.partial(chunk, c) for c in range(i + 1)], result

    def exp_pass(i, m):
        l8 = [None, None]

        def chunk(c):
            for half in range(2):
                p = jnp.exp2(st_ref[2 * (i % 2) + half, c * tq:(c + 1) * tq, :] - m[half])
                pt_ref[2 * (i % 2) + half, c * tq:(c + 1) * tq, :] = p.astype(BF16)
                f = _fold8(p, jnp.add)
                l8[half] = f if l8[half] is None else l8[half] + f

        def result():
            return [jnp.sum(l8[half], axis=0, keepdims=True) for half in range(2)]

        return [functools.partial(chunk, c) for c in range(i + 1)], result

    def value_pass(i, l1, l2):
        kv = (i + 1) * tq
        acc = [jnp.dot(vt_ref[:, 0:kv], pt_ref[2 * (i % 2) + half, 0:kv, :],
                       preferred_element_type=F32) for half in range(2)]
        o = (acc[0] * (1.0 / l1) - acc[1] * (lam / l2)).T
        ms = jnp.mean(o * o, axis=1, keepdims=True)
        y = o * lax.rsqrt(ms + 1e-5) * gain
        o_ref[0, i * tq:(i + 1) * tq, :] = y.astype(o_ref.dtype)

    chunks, result = logits_pass(0)
    for run in chunks:
        run()
    m_next = result()
    for i in range(nq):
        exp_chunks, exp_result = exp_pass(i, m_next)
        next_chunks, next_result = logits_pass(i + 1) if i + 1 < nq else ([], None)
        for c in range(max(len(exp_chunks), len(next_chunks))):
            if c < len(next_chunks):
                next_chunks[c]()
            if c < len(exp_chunks):
                exp_chunks[c]()
        if next_result is not None:
            m_next = next_result()
        value_pass(i, *exp_result())


def _t5_bucket_np(dist):
    n = np.maximum(dist, 0)
    max_exact = REL_BUCKETS // 2
    large = max_exact + (np.log(np.maximum(n, 1).astype(np.float32) / max_exact)
                         / math.log(REL_MAX_DIST / max_exact)
                         * (REL_BUCKETS - max_exact)).astype(np.int32)
    large = np.minimum(large, REL_BUCKETS - 1)
    return np.where(n < max_exact, n, large)


def _near_buckets(tq):
    assert _t5_bucket_np(np.array([tq + 1]))[0] == REL_BUCKETS - 1
    c = np.arange(tq)[:, None]
    r = np.arange(tq)[None, :]
    left = _t5_bucket_np(r + tq - c)
    diag = np.where(r >= c, _t5_bucket_np(r - c), -1)
    return np.stack([left, diag], axis=0).astype(np.int32)


def _diff_attention(u3, rel_bias, diff_lambda, subln_g, tq, lam_init):
    b, s, _ = u3.shape
    kern = functools.partial(_dattn_kernel, tq=tq, seq=s, lam_init=lam_init)
    grid_spec = pltpu.PrefetchScalarGridSpec(
        num_scalar_prefetch=1,
        grid=(DA_HEADS, b),
        in_specs=[
            pl.BlockSpec((1, s, LANES), lambda h, bi, rb: (bi, 0, OFF_DAQ + h)),
            pl.BlockSpec((1, s, LANES), lambda h, bi, rb: (bi, 0, OFF_DAK + h)),
            pl.BlockSpec((1, s, LANES), lambda h, bi, rb: (bi, 0, OFF_DAV + h)),
            pl.BlockSpec((2, tq, tq), lambda h, bi, rb: (0, 0, 0)),
            pl.BlockSpec((4, DA_HEAD_DIM), lambda h, bi, rb: (0, 0)),
            pl.BlockSpec((1, DA_V_DIM), lambda h, bi, rb: (0, 0)),
        ],
        out_specs=pl.BlockSpec((1, s, LANES), lambda h, bi, rb: (bi, 0, h)),
        scratch_shapes=[pltpu.VMEM((4, s, tq), F32), pltpu.VMEM((4, s, tq), BF16),
                        pltpu.VMEM((DA_V_DIM, s), BF16),
                        pltpu.VMEM((2 * tq, tq), F32)],
    )
    return pl.pallas_call(
        kern,
        grid_spec=grid_spec,
        out_shape=jax.ShapeDtypeStruct((b, s, DA_HEADS * DA_V_DIM), BF16),
        compiler_params=_cparams("arbitrary", "arbitrary"),
        name="diff_attn",
    )(rel_bias.astype(F32), u3, u3, u3, jnp.asarray(_near_buckets(tq)), diff_lambda, subln_g)


def _mlstm_kernel(q_ref, k_ref, v_ref, og_ref, g_ref, cwq_ref, cwk_ref, ng_ref, y_ref,
                  ct_ref, gts_ref, *, chunk, seq):
    first_head = pl.program_id(1) * ML_GROUP
    nc = seq // chunk
    hd = ML_HEAD_DIM
    lane = lax.broadcasted_iota(jnp.int32, (chunk, LANES), 1)
    row = lax.broadcasted_iota(jnp.int32, (chunk, chunk), 0)
    col = lax.broadcasted_iota(jnp.int32, (chunk, chunk), 1)
    causal = row >= col
    tri = causal.astype(F32)
    ct_ref[...] = jnp.zeros(ct_ref.shape, F32)

    def conv_silu(ref, cw_ref, r0, c, cols):
        cw = cw_ref[:, cols]
        cur = ref[0, pl.ds(r0, chunk), cols].astype(F32)
        p0 = pl.multiple_of(jnp.maximum(r0 - 16, 0), 16)
        prev = ref[0, pl.ds(p0, 16), cols].astype(F32)
        prev = jnp.where(c > 0, prev, 0.0)
        x = jnp.concatenate([prev, cur], axis=0)
        out = cw[ML_CONV - 1:ML_CONV] * cur
        for tap in range(ML_CONV - 1):
            shifted = pltpu.roll(x, ML_CONV - 1 - tap, 0)[16:]
            out = out + cw[tap:tap + 1] * shifted
        return out * _sigmoid(out)

    def head_step(j, c, r0, g, bc, m, n):
        h = first_head + j
        cols = slice(j * hd, (j + 1) * hd)
        irow = gts_ref[pl.ds(h, 1), :]
        brow = gts_ref[pl.ds(ML_HEADS + h, 1), :]
        bcol = jnp.sum(jnp.where(lane == ML_HEADS + h, bc, 0.0), axis=1, keepdims=True)
        icol = jnp.sum(jnp.where(lane == h, g, 0.0), axis=1, keepdims=True)

        dmat = jnp.where(causal, bcol - brow + irow, NEG)
        inter = bcol + m
        m_row = jnp.maximum(inter, jnp.max(dmat, axis=1, keepdims=True))

        q = conv_silu(q_ref, cwq_ref, r0, c, cols)
        k = conv_silu(k_ref, cwk_ref, r0, c, cols) * (hd ** -0.5)
        qb = q.astype(BF16)
        kb = k.astype(BF16)
        vb = v_ref[0, pl.ds(r0, chunk), cols]
        sqk = lax.dot_general(qb, kb, (((1,), (1,)), ((), ())), preferred_element_type=F32)
        w = jnp.exp(dmat - m_row) * sqk
        inter_w = jnp.exp(inter - m_row)
        ct = ct_ref[j]
        num = (inter_w * jnp.dot(qb, ct.astype(BF16), preferred_element_type=F32)
               + jnp.dot(w.astype(BF16), vb, preferred_element_type=F32))
        den = inter_w * jnp.sum(q * n, axis=1, keepdims=True) + jnp.sum(w, axis=1, keepdims=True)
        hout = num / jnp.maximum(jnp.abs(den), jnp.exp(-m_row))
        mu = jnp.mean(hout, axis=1, keepdims=True)
        hc = hout - mu
        var = jnp.mean(hc * hc, axis=1, keepdims=True)
        hn = hc * lax.rsqrt(var + 1e-5) * ng_ref[:, cols]
        og = og_ref[0, pl.ds(r0, chunk), cols].astype(F32)
        y_ref[0, pl.ds(r0, chunk), cols] = (_sigmoid(og) * hn).astype(y_ref.dtype)

        total = brow[:, chunk - 1:chunk]
        grow = total - brow + irow
        m_new = jnp.maximum(total + m, jnp.max(grow, axis=1, keepdims=True))
        decay = jnp.exp(total + m - m_new)
        ws = jnp.exp(total - bcol + icol - m_new)
        wsv = (ws * vb.astype(F32)).astype(BF16)
        ct_ref[j] = decay * ct + jnp.dot(k.T.astype(BF16), wsv, preferred_element_type=F32)
        n_new = decay * n + jnp.sum(ws * k, axis=0, keepdims=True)
        return m_new, n_new

    def body(c, carry):
        r0 = pl.multiple_of(c * chunk, chunk)
        g = g_ref[0, pl.ds(r0, chunk), :]
        logf = jnp.minimum(g, 0.0) - jnp.log(1.0 + jnp.exp(-jnp.abs(g)))
        bc = jnp.dot(tri, logf, precision=lax.Precision.HIGHEST, preferred_element_type=F32)
        gts_ref[...] = jnp.where(lane < ML_HEADS, g, bc).T
        return tuple(head_step(j, c, r0, g, bc, *carry[j]) for j in range(ML_GROUP))

    init = tuple((jnp.zeros((1, 1), F32), jnp.zeros((1, hd), F32)) for _ in range(ML_GROUP))
    lax.fori_loop(0, nc, body, init)


def _mlstm(u3, gates3, conv_w, norm_g, chunk):
    b, s, _ = u3.shape
    gw = ML_GROUP * ML_HEAD_DIM
    nq = gw // LANES
    ngroups = ML_HEADS // ML_GROUP
    kern = functools.partial(_mlstm_kernel, chunk=chunk, seq=s)

    def ublock(off):
        return pl.BlockSpec((1, s, gw), lambda bi, h: (bi, 0, off // nq + h))

    return pl.pallas_call(
        kern,
        grid=(b, ngroups),
        in_specs=[
            ublock(OFF_MLQ), ublock(OFF_MLK), ublock(OFF_MLV), ublock(OFF_MLO),
            pl.BlockSpec((1, s, LANES), lambda bi, h: (bi, 0, 0)),
            pl.BlockSpec((ML_CONV, gw), lambda bi, h: (0, h)),
            pl.BlockSpec((ML_CONV, gw), lambda bi, h: (0, ngroups + h)),
            pl.BlockSpec((1, gw), lambda bi, h: (0, h)),
        ],
        out_specs=pl.BlockSpec((1, s, gw), lambda bi, h: (bi, 0, h)),
        out_shape=jax.ShapeDtypeStruct((b, s, ML_HEADS * ML_HEAD_DIM), BF16),
        scratch_shapes=[pltpu.VMEM((ML_GROUP, ML_HEAD_DIM, ML_HEAD_DIM), F32),
                        pltpu.VMEM((LANES, chunk), F32)],
        compiler_params=_cparams("parallel", "arbitrary"),
        name="mlstm",
    )(u3, u3, u3, u3, gates3, conv_w, conv_w, norm_g)


def _memattn_kernel(q_ref, k_ref, v_ref, o_ref, *, tq, seq):
    kb = k_ref[0]
    vb = v_ref[0]
    scale = jnp.asarray(MA_HEAD_DIM ** -0.5, BF16)
    for t in range(seq // tq):
        q = q_ref[0, t * tq:(t + 1) * tq, :] * scale
        s = lax.dot_general(q, kb, (((1,), (1,)), ((), ())), preferred_element_type=F32)
        p = jnp.exp(s - jnp.max(s, axis=1, keepdims=True))
        inv = 1.0 / jnp.sum(p, axis=1, keepdims=True)
        o = jnp.dot(p.astype(BF16), vb, preferred_element_type=F32) * inv
        o_ref[0, t * tq:(t + 1) * tq, :] = o.astype(o_ref.dtype)


def _mem_attention(u3, kv3, tq):
    b, s, _ = u3.shape
    mlen = kv3.shape[1]
    hd = MA_HEAD_DIM
    nq = hd // LANES
    kern = functools.partial(_memattn_kernel, tq=tq, seq=s)
    return pl.pallas_call(
        kern,
        grid=(b, MA_HEADS),
        in_specs=[
            pl.BlockSpec((1, s, hd), lambda bi, h: (bi, 0, OFF_MAQ // nq + h)),
            pl.BlockSpec((1, mlen, hd), lambda bi, h: (bi, 0, h)),
            pl.BlockSpec((1, mlen, hd), lambda bi, h: (bi, 0, MA_HEADS + h)),
        ],
        out_specs=pl.BlockSpec((1, s, hd), lambda bi, h: (bi, 0, h)),
        out_shape=jax.ShapeDtypeStruct((b, s, MA_HEADS * hd), BF16),
        compiler_params=_cparams("parallel", "parallel"),
        name="mem_attn",
    )(u3, kv3, kv3)


def _merge_kernel(ya_ref, ym_ref, yc_ref, g0_ref, g1_ref, g2_ref, x_ref, wb_ref, wo_ref,
                  lg_ref, lb_ref, x1_ref, x1b_ref, x1p_ref):
    acc = None
    for n, (y_ref, g_ref) in enumerate(((ya_ref, g0_ref), (ym_ref, g1_ref), (yc_ref, g2_ref))):
        pr = jnp.dot(y_ref[...], wb_ref[n], preferred_element_type=F32)
        t = jax.nn.sigmoid(g_ref[...].astype(F32)) * pr
        acc = t if acc is None else acc + t
    out = jnp.dot(acc.astype(BF16), wo_ref[...], preferred_element_type=F32)
    x1 = _layer_norm(ALPHA * x_ref[...] + out, lg_ref[...], lb_ref[...])
    x1_ref[...] = x1
    x1b = x1.astype(BF16)
    x1b_ref[...] = x1b
    x1p_ref[...] = _pack_halves(x1b)


def _pack_halves(vb):
    w = vb.shape[1] // 2
    hi = lax.bitcast_convert_type(vb[:, :w].astype(F32), U32)
    lo = lax.bitcast_convert_type(vb[:, w:].astype(F32), U32)
    return hi | (lo >> 16)


def _unpack_halves(u):
    hi = lax.bitcast_convert_type(u & jnp.uint32(0xFFFF0000), F32)
    lo = lax.bitcast_convert_type(u << 16, F32)
    return hi, lo


def _merge(ya, ym, yc, u2, x2, wb, wo, lg, lb, tm):
    t, d = x2.shape
    gb = OFF_GATE * LANES // d

    def rows(i):
        return (i, 0)

    return pl.pallas_call(
        _merge_kernel,
        grid=(t // tm,),
        in_specs=[
            pl.BlockSpec((tm, d), rows), pl.BlockSpec((tm, d), rows), pl.BlockSpec((tm, d), rows),
            pl.BlockSpec((tm, d), lambda i: (i, gb)),
            pl.BlockSpec((tm, d), lambda i: (i, gb + 1)),
            pl.BlockSpec((tm, d), lambda i: (i, gb + 2)),
            pl.BlockSpec((tm, d), rows),
            pl.BlockSpec((N_BRANCH, d, d), lambda i: (0, 0, 0)),
            pl.BlockSpec((d, d), lambda i: (0, 0)),
            pl.BlockSpec((1, d), lambda i: (0, 0)),
            pl.BlockSpec((1, d), lambda i: (0, 0)),
        ],
        out_specs=[pl.BlockSpec((tm, d), rows), pl.BlockSpec((tm, d), rows),
                   pl.BlockSpec((tm, d // 2), rows)],
        out_shape=[jax.ShapeDtypeStruct((t, d), F32), jax.ShapeDtypeStruct((t, d), BF16),
                   jax.ShapeDtypeStruct((t, d // 2), U32)],
        compiler_params=_cparams("parallel"),
        name="merge_ln1",
    )(ya, ym, yc, u2, u2, u2, x2, wb, wo, lg, lb)


def _expert_kernel(ib_ref, ie_ref, lo_ref, hi_ref, first_ref, nit_ref, new_ref, slot_ref, next_ref,
                   x_ref, wg_hbm, wu_hbm, wd_hbm, after_hbm, o_ref, wgf_ref, wuf_ref, wdf_ref,
                   wgb_ref, wub_ref, wdb_ref, wsem):
    i = pl.program_id(0)

    def weight_copies(e, s):
        return (pltpu.make_async_copy(wg_hbm.at[e], wgf_ref.at[s], wsem.at[s]),
                pltpu.make_async_copy(wu_hbm.at[e], wuf_ref.at[s], wsem.at[s]),
                pltpu.make_async_copy(wd_hbm.at[e], wdf_ref.at[s], wsem.at[s]))

    @pl.when(i == 0)
    def _():
        for cp in weight_copies(ie_ref[0], 0):
            cp.start()

    @pl.when(new_ref[i] == 1)
    def _():
        s = slot_ref[i]
        for cp in weight_copies(ie_ref[i], s):
            cp.wait()

        @pl.when(next_ref[i] >= 0)
        def _():
            for cp in weight_copies(next_ref[i], 1 - s):
                cp.start()

        wgb_ref[...] = wgf_ref[s].astype(BF16)
        wub_ref[...] = wuf_ref[s].astype(BF16)
        wdb_ref[...] = wdf_ref[s].astype(BF16)

    @pl.when(i < nit_ref[0])
    def _():
        xl, xr = _unpack_halves(x_ref[...])
        xl = xl.astype(BF16)
        xr = xr.astype(BF16)
        half = xl.shape[1]

        def up(w_ref):
            return (jnp.dot(xl, w_ref[:half, :], preferred_element_type=F32)
                    + jnp.dot(xr, w_ref[half:, :], preferred_element_type=F32))

        hg = up(wgb_ref)
        hu = up(wub_ref)
        act = (hg * jax.nn.sigmoid(hg) * hu).astype(BF16)
        y = _pack_halves(jnp.dot(act, wdb_ref[...], preferred_element_type=F32).astype(BF16))
        row = lax.broadcasted_iota(I32, y.shape, 0)
        mine = (row >= lo_ref[i]) & (row < hi_ref[i])

        @pl.when(first_ref[i] == 1)
        def _():
            o_ref[...] = jnp.where(mine, y, jnp.uint32(0))

        @pl.when(first_ref[i] == 0)
        def _():
            o_ref[...] = jnp.where(mine, y, o_ref[...])


def _work_items(counts, n_pairs, bm):
    assert n_pairs % bm == 0
    nblocks = n_pairs // bm
    ends = jnp.cumsum(counts)
    starts = ends - counts
    first_blk = starts // bm
    n_e = jnp.where(counts > 0, (ends - 1) // bm - first_blk + 1, 0)
    item_end = jnp.cumsum(n_e)
    item_start = item_end - n_e
    n_items = item_end[-1]
    i = jnp.arange(nblocks + N_EXPERTS)
    valid = i < n_items
    e = jnp.minimum(jnp.sum(item_end[None, :] <= jnp.minimum(i, n_items - 1)[:, None], axis=1),
                    N_EXPERTS - 1)
    blk = jnp.where(valid, first_blk[e] + i - item_start[e], nblocks - 1)
    lo = jnp.clip(starts[e] - blk * bm, 0, bm)
    hi = jnp.where(valid, jnp.clip(ends[e] - blk * bm, 0, bm), 0)
    first = jnp.concatenate([jnp.ones((1,), I32), (blk[1:] != blk[:-1]).astype(I32)])
    new = jnp.where(valid, jnp.concatenate([jnp.ones((1,), bool), e[1:] != e[:-1]]), False)
    slot = (jnp.cumsum(new) - 1) % 2
    ids = jnp.arange(N_EXPERTS)
    later = jnp.where((counts > 0)[None, :] & (ids[None, :] > ids[:, None]), ids[None, :], N_EXPERTS)
    next_e = jnp.min(later, axis=1)
    nxt = jnp.where(next_e[e] < N_EXPERTS, next_e[e], -1)
    items = tuple(a.astype(I32) for a in (blk, e, lo, hi, first, n_items[None], new, slot, nxt))
    return items, starts.astype(I32)


def _experts(items, xs, w_gate, w_up, w_down, bm, after):
    n, dh = xs.shape
    d = 2 * dh
    de = w_gate.shape[2]

    def rows(i, ib, *_):
        return (ib[i], 0)

    grid_spec = pltpu.PrefetchScalarGridSpec(
        num_scalar_prefetch=len(items),
        grid=(items[0].shape[0],),
        in_specs=[
            pl.BlockSpec((bm, dh), rows),
            pl.BlockSpec(memory_space=pl.ANY),
            pl.BlockSpec(memory_space=pl.ANY),
            pl.BlockSpec(memory_space=pl.ANY),
            pl.BlockSpec(memory_space=pl.ANY),
        ],
        out_specs=pl.BlockSpec((bm, dh), rows),
        scratch_shapes=[pltpu.VMEM((2, d, de), F32), pltpu.VMEM((2, d, de), F32),
                        pltpu.VMEM((2, de, d), F32),
                        pltpu.VMEM((d, de), BF16), pltpu.VMEM((d, de), BF16),
                        pltpu.VMEM((de, d), BF16), pltpu.SemaphoreType.DMA((2,))],
    )
    return pl.pallas_call(
        _expert_kernel,
        grid_spec=grid_spec,
        out_shape=jax.ShapeDtypeStruct((n, dh), U32),
        compiler_params=_cparams("arbitrary"),
        name="experts",
    )(*items, xs, w_gate, w_up, w_down, after)


SC_WINDOW = 128
SC_WORKERS = 32


def _sc_worker():
    return lax.axis_index("core") * (SC_WORKERS // 2) + lax.axis_index("subcore")


def _sc_scatter_rows(x, idx, tm):
    t, dh = x.shape
    n = idx.shape[0]
    assert n == t * TOP_K and tm % SC_WINDOW == 0
    windows = t // SC_WINDOW
    per = windows // SC_WORKERS
    assert windows % SC_WORKERS == 0
    wpt = tm // SC_WINDOW
    mesh = plsc.VectorSubcoreMesh(core_axis_name="core", subcore_axis_name="subcore")

    @pl.kernel(out_type=jax.ShapeDtypeStruct((n, dh), x.dtype), mesh=mesh,
               scratch_types=[pltpu.VMEM((SC_WINDOW,), I32), pltpu.VMEM((SC_WINDOW, dh), x.dtype)])
    def scatter(x_hbm, idx_hbm, out_hbm, idx_vmem, rows_vmem):
        worker = _sc_worker()

        @pl.loop(0, per)
        def _(j):
            w = worker * per + j
            tile = w // wpt
            off = (w - tile * wpt) * SC_WINDOW
            pltpu.sync_copy(x_hbm.at[pl.ds(w * SC_WINDOW, SC_WINDOW)], rows_vmem)

            @pl.loop(0, TOP_K)
            def _(k):
                base = (tile * TOP_K + k) * tm + off
                pltpu.sync_copy(idx_hbm.at[pl.ds(base, SC_WINDOW)], idx_vmem)
                pltpu.sync_copy(rows_vmem, out_hbm.at[idx_vmem])

    return scatter(x, idx)


def _sc_gather_rows(src, idx):
    n = idx.shape[0]
    dh = src.shape[1]
    per = n // SC_WORKERS
    assert per % SC_WINDOW == 0
    mesh = plsc.VectorSubcoreMesh(core_axis_name="core", subcore_axis_name="subcore")

    @pl.kernel(out_type=jax.ShapeDtypeStruct((n, dh), src.dtype), mesh=mesh,
               scratch_types=[pltpu.VMEM((SC_WINDOW,), I32), pltpu.VMEM((SC_WINDOW, dh), src.dtype)])
    def gather(src_hbm, idx_hbm, out_hbm, idx_vmem, rows_vmem):
        worker = _sc_worker()

        @pl.loop(0, per // SC_WINDOW)
        def _(j):
            base = worker * per + j * SC_WINDOW
            pltpu.sync_copy(idx_hbm.at[pl.ds(base, SC_WINDOW)], idx_vmem)
            pltpu.sync_copy(src_hbm.at[idx_vmem], rows_vmem)
            pltpu.sync_copy(rows_vmem, out_hbm.at[pl.ds(base, SC_WINDOW)])

    return gather(src, idx)


def _shared_ffn_kernel(xb_ref, wg_ref, wu_ref, wd_ref, o_ref):
    xb = xb_ref[...]
    hg = jnp.dot(xb, wg_ref[...], preferred_element_type=F32)
    hu = jnp.dot(xb, wu_ref[...], preferred_element_type=F32)
    act = (hg * jax.nn.sigmoid(hg) * hu).astype(BF16)
    o_ref[...] = jnp.dot(act, wd_ref[...], preferred_element_type=F32)


def _shared_ffn(x1b, wg, wu, wd, tm):
    t, d = x1b.shape
    ds = wg.shape[1]
    return pl.pallas_call(
        _shared_ffn_kernel,
        grid=(t // tm,),
        in_specs=[pl.BlockSpec((tm, d), lambda i: (i, 0)), pl.BlockSpec((d, ds), lambda i: (0, 0)),
                  pl.BlockSpec((d, ds), lambda i: (0, 0)), pl.BlockSpec((ds, d), lambda i: (0, 0))],
        out_specs=pl.BlockSpec((tm, d), lambda i: (i, 0)),
        out_shape=jax.ShapeDtypeStruct((t, d), F32),
        compiler_params=_cparams("parallel"),
        name="shared_ffn",
    )(x1b, wg, wu, wd)


def _ffn_out_kernel(yt_ref, sh_ref, x1_ref, wt_ref, lg_ref, lb_ref, *rest):
    o_ref = rest[-1]
    wt = wt_ref[...]
    rl = None
    rr = None
    for k in range(TOP_K):
        hi, lo = _unpack_halves(yt_ref[0, k])
        wk = wt[:, k:k + 1]
        rl = wk * hi if rl is None else rl + wk * hi
        rr = wk * lo if rr is None else rr + wk * lo
    z = ALPHA * x1_ref[...] + sh_ref[...] + jnp.concatenate([rl, rr], axis=1)
    o_ref[...] = _layer_norm(z, lg_ref[...], lb_ref[...])


def _ffn_out(yt, sh, x1, wt, lg, lb, tm, tile0, prev):
    t, d = x1.shape
    dh = yt.shape[-1]
    assert yt.shape[1:] == (TOP_K, tm, dh)

    def rows(i):
        return (i + tile0, 0)

    def whole(i):
        return (0, 0)

    in_specs = [
        pl.BlockSpec((1, TOP_K, tm, dh), lambda i: (i, 0, 0, 0)),
        pl.BlockSpec((tm, d), rows), pl.BlockSpec((tm, d), rows), pl.BlockSpec((tm, LANES), rows),
        pl.BlockSpec((1, d), whole), pl.BlockSpec((1, d), whole),
    ]
    args = [yt, sh, x1, wt, lg, lb]
    aliases = {}
    if prev is not None:
        in_specs.append(pl.BlockSpec(memory_space=pl.ANY))
        args.append(prev)
        aliases = {len(args) - 1: 0}
    return pl.pallas_call(
        _ffn_out_kernel,
        grid=(yt.shape[0],),
        in_specs=in_specs,
        out_specs=pl.BlockSpec((tm, d), rows),
        out_shape=jax.ShapeDtypeStruct((t, d), F32),
        input_output_aliases=aliases,
        compiler_params=_cparams("parallel"),
        name="ffn_out_ln2",
    )(*args)


def _route_kernel(xb_ref, wrt_ref, rb_ref, ek_ref, rk_ref, wt_ref, cnt_ref, upper_ref, run_ref, *,
                  tm):
    i = pl.program_id(0)
    gsz = N_EXPERTS // N_GROUP
    ninf = -jnp.inf

    @pl.when(i == 0)
    def _():
        r = lax.broadcasted_iota(I32, (tm, tm), 0)
        c = lax.broadcasted_iota(I32, (tm, tm), 1)
        upper_ref[...] = jnp.where(r < c, 1.0, 0.0).astype(BF16)
        run_ref[...] = jnp.zeros(run_ref.shape, F32)

    logits = lax.dot_general(wrt_ref[...], xb_ref[...], (((1,), (1,)), ((), ())),
                             preferred_element_type=F32)
    scores = jax.nn.sigmoid(logits)
    choice = scores + rb_ref[...]

    ridx = lax.broadcasted_iota(I32, (gsz, tm), 0)
    gscore = []
    for g in range(N_GROUP):
        blk = choice[g * gsz:(g + 1) * gsz, :]
        m1 = jnp.max(blk, axis=0, keepdims=True)
        i1 = jnp.min(jnp.where(blk == m1, ridx, gsz), axis=0, keepdims=True)
        m2 = jnp.max(jnp.where(ridx == i1, ninf, blk), axis=0, keepdims=True)
        gscore.append(m1 + m2)
    masked = []
    for g in range(N_GROUP):
        beaten = jnp.zeros((1, tm), I32)
        for g2 in range(N_GROUP):
            if g2 == g:
                continue
            wins = (gscore[g2] >= gscore[g]) if g2 < g else (gscore[g2] > gscore[g])
            beaten = beaten + jnp.where(wins, 1, 0)
        masked.append(jnp.where(beaten < TOPK_GROUP, choice[g * gsz:(g + 1) * gsz, :], ninf))
    v = jnp.concatenate(masked, axis=0)

    eidx = lax.broadcasted_iota(I32, (N_EXPERTS, tm), 0)
    sel = jnp.zeros((N_EXPERTS, tm), F32)
    e_rows = []
    s_rows = []
    for k in range(TOP_K):
        m = jnp.max(v, axis=0, keepdims=True)
        ik = jnp.min(jnp.where(v == m, eidx, N_EXPERTS), axis=0, keepdims=True)
        hit = eidx == ik
        e_rows.append(ik)
        s_rows.append(jnp.sum(jnp.where(hit, scores, 0.0), axis=0, keepdims=True))
        v = jnp.where(hit, ninf, v)
        sel = jnp.where(hit, 1.0, sel)

    prefix = jnp.dot(sel.astype(BF16), upper_ref[...], preferred_element_type=F32)
    pos = prefix + run_ref[...]
    for k in range(TOP_K):
        rk = jnp.sum(jnp.where(eidx == e_rows[k], pos, 0.0), axis=0, keepdims=True)
        ek_ref[k:k + 1, :] = e_rows[k]
        rk_ref[k:k + 1, :] = rk.astype(I32)
    run_ref[...] = run_ref[...] + jnp.sum(sel, axis=1, keepdims=True)
    cnt_ref[...] = jnp.broadcast_to(run_ref[...], cnt_ref.shape).astype(I32)

    ssum = s_rows[0]
    for k in range(1, TOP_K):
        ssum = ssum + s_rows[k]
    w_rows = [s / (ssum + 1e-20) * ROUTED_SCALE for s in s_rows]
    w_rows.append(jnp.zeros((LANES - TOP_K, tm), F32))
    wt_ref[...] = jnp.concatenate(w_rows, axis=0).T


def _route(x1b, wrt, rbias, tm):
    t, d = x1b.shape
    return pl.pallas_call(
        functools.partial(_route_kernel, tm=tm),
        grid=(t // tm,),
        in_specs=[pl.BlockSpec((tm, d), lambda i: (i, 0)),
                  pl.BlockSpec((N_EXPERTS, d), lambda i: (0, 0)),
                  pl.BlockSpec((N_EXPERTS, 1), lambda i: (0, 0))],
        out_specs=[pl.BlockSpec((TOP_K, tm), lambda i: (0, i)),
                   pl.BlockSpec((TOP_K, tm), lambda i: (0, i)),
                   pl.BlockSpec((tm, LANES), lambda i: (i, 0)),
                   pl.BlockSpec((N_EXPERTS, LANES), lambda i: (0, 0))],
        out_shape=[jax.ShapeDtypeStruct((TOP_K, t), I32), jax.ShapeDtypeStruct((TOP_K, t), I32),
                   jax.ShapeDtypeStruct((t, LANES), F32),
                   jax.ShapeDtypeStruct((N_EXPERTS, LANES), I32)],
        scratch_shapes=[pltpu.VMEM((tm, tm), BF16), pltpu.VMEM((N_EXPERTS, 1), F32)],
        compiler_params=_cparams("arbitrary"),
        name="route",
    )(x1b, wrt, rbias)


def _dest_kernel(ps_ref, ek_ref, rk_ref, d_ref, *, tm):
    e = ek_ref[...]

    def body(j, acc):
        return acc + jnp.where(e == j, ps_ref[j], 0)

    res = lax.fori_loop(0, N_EXPERTS, body, jnp.zeros(e.shape, I32)) + rk_ref[...]
    for a in range(e.shape[1] // tm):
        for k in range(TOP_K):
            d_ref[a:a + 1, k * tm:(k + 1) * tm] = res[k:k + 1, a * tm:(a + 1) * tm]


def _dest(pstart, ek, rk, tm, tw):
    t = ek.shape[1]
    grid_spec = pltpu.PrefetchScalarGridSpec(
        num_scalar_prefetch=1,
        grid=(t // tw,),
        in_specs=[pl.BlockSpec((TOP_K, tw), lambda i, ps: (0, i)),
                  pl.BlockSpec((TOP_K, tw), lambda i, ps: (0, i))],
        out_specs=pl.BlockSpec((tw // tm, TOP_K * tm), lambda i, ps: (i, 0)),
    )
    return pl.pallas_call(
        functools.partial(_dest_kernel, tm=tm),
        grid_spec=grid_spec,
        out_shape=jax.ShapeDtypeStruct((t // tm, TOP_K * tm), I32),
        compiler_params=_cparams("parallel"),
        name="dest",
    )(pstart, ek, rk)


def _layer(x, mem, rel_bias, w_in, b_in, conv_w, diff_lambda, subln_g, mlstm_norm_g, w_mem_kv,
           w_branch, w_out, ln1_g, ln1_b, w_router, router_bias, w_e_gate, w_e_up, w_e_down,
           w_s_gate, w_s_up, w_s_down, ln2_g, ln2_b, layer_idx, cfg):
    b, s, d = x.shape
    t = b * s
    x2 = x.reshape(t, d)

    g0 = (OFF_MLO + 8) * LANES
    w_main = jnp.concatenate([w_in[:, :g0], w_in[:, g0 + 2 * ML_HEADS:]], axis=1).astype(BF16)
    b_main = jnp.concatenate([b_in[:g0], b_in[g0 + 2 * ML_HEADS:]])[None, :]
    w_g = jnp.pad(w_in[:, g0:g0 + 2 * ML_HEADS], ((0, 0), (0, LANES - 2 * ML_HEADS))).astype(BF16)
    b_g = jnp.pad(b_in[g0:g0 + 2 * ML_HEADS], (0, LANES - 2 * ML_HEADS))[None, :]

    u2, gates2 = _proj_in(x2, w_main, b_main, w_g, b_g, cfg["proj_tm"], cfg["proj_tn"])
    u3 = u2.reshape(b, s, N_MAIN)
    gates3 = gates2.reshape(b, s, LANES)

    lam_init = 0.8 - 0.6 * math.exp(-0.3 * layer_idx)
    y_a = _diff_attention(u3, rel_bias, diff_lambda, subln_g[None, :], cfg["attn_tq"], lam_init)
    y_m = _mlstm(u3, gates3, conv_w, mlstm_norm_g[None, :], cfg["ml_chunk"])
    kv = _mm(mem.reshape(-1, d), w_mem_kv.astype(BF16), BF16, cfg["kv_tm"], "mem_kv")
    y_c = _mem_attention(u3, kv.reshape(b, -1, 2 * MA_HEADS * MA_HEAD_DIM), cfg["ma_tq"])

    x1, x1b, x1p = _merge(y_a.reshape(t, d), y_m.reshape(t, d), y_c.reshape(t, d), u2, x2,
                          w_branch.astype(BF16), w_out.astype(BF16), ln1_g[None, :],
                          ln1_b[None, :], cfg["merge_tm"])

    ek, rk, wt, cnt = _route(x1b, w_router.T.astype(BF16), router_bias.astype(F32)[:, None],
                             cfg["route_tm"])
    bm = cfg["expert_bm"]
    items, starts = _work_items(cnt[:, 0], t * TOP_K, bm)
    dest = _dest(starts, ek, rk, cfg["moe_tm"], cfg["dest_tw"]).reshape(-1)
    xs = _sc_scatter_rows(x1p, dest, cfg["moe_tm"])
    sh = _shared_ffn(x1b, w_s_gate.astype(BF16), w_s_up.astype(BF16), w_s_down.astype(BF16),
                     cfg["shared_tm"])
    ys = _experts(items, xs, w_e_gate, w_e_up, w_e_down, bm, after=sh)
    tm = cfg["moe_tm"]
    per = t // tm // cfg["combine_chunks"]
    out = None
    for c in range(cfg["combine_chunks"]):
        rows = dest[c * per * TOP_K * tm:(c + 1) * per * TOP_K * tm]
        yt = _sc_gather_rows(ys, rows).reshape(per, TOP_K, tm, d // 2)
        out = _ffn_out(yt, sh, x1, wt, ln2_g[None, :], ln2_b[None, :], tm, c * per, out)
    return out.reshape(b, s, d)


def _config(b, s):
    t = b * s
    return {
        "proj_tm": min(1024, t), "proj_tn": 1024,
        "attn_tq": 256, "ml_chunk": 256, "kv_tm": 512, "ma_tq": min(512, s),
        "merge_tm": 256, "route_tm": 512, "expert_bm": 512, "moe_tm": 256, "dest_tw": 2048,
        "shared_tm": 512, "combine_chunks": 4,
    }


def kernel(x, mem, rel_bias, w_in, b_in, conv_w, diff_lambda, subln_g, mlstm_norm_g, w_mem_kv,
           w_branch, w_out, ln1_g, ln1_b, w_router, router_bias, w_e_gate, w_e_up, w_e_down,
           w_s_gate, w_s_up, w_s_down, ln2_g, ln2_b):
    cfg = _config(x.shape[0], x.shape[1])
    for l in range(DEPTH):
        x = _layer(x, mem, rel_bias, w_in[l], b_in[l], conv_w[l], diff_lambda[l], subln_g[l],
                   mlstm_norm_g[l], w_mem_kv[l], w_branch[l], w_out[l], ln1_g[l], ln1_b[l],
                   w_router[l], router_bias[l], w_e_gate[l], w_e_up[l], w_e_down[l],
                   w_s_gate[l], w_s_up[l], w_s_down[l], ln2_g[l], ln2_b[l], l, cfg)
    return x
```

```python
import functools
import math

import numpy as np
import jax
import jax.numpy as jnp
from jax import lax
from jax.experimental import pallas as pl
from jax.experimental.pallas import tpu as pltpu
from jax.experimental.pallas import tpu_sc as plsc

F32 = jnp.float32
BF16 = jnp.bfloat16
U32 = jnp.uint32
I32 = jnp.int32

D_MODEL = 1024
DEPTH = 1
DA_HEAD_DIM = 64
DA_V_DIM = 128
DA_HEADS = 8
ML_HEADS = 4
ML_HEAD_DIM = 256
ML_CONV = 4
ML_GROUP = 2
MA_HEADS = 4
MA_HEAD_DIM = 256
N_BRANCH = 3
REL_BUCKETS = 32
REL_MAX_DIST = 128
N_EXPERTS = 256
TOP_K = 8
N_GROUP = 8
TOPK_GROUP = 4
D_EXPERT = 256
ROUTED_SCALE = 2.5
ALPHA = (2.0 * DEPTH) ** 0.25

LANES = 128
NEG = -1e30
LOG2E = math.log2(math.e)
FOLD_ACCS = 4
VMEM_LIMIT = 56 * 1024 * 1024

OFF_DAQ, OFF_DAK, OFF_DAV = 0, 8, 16
OFF_MLQ, OFF_MLK, OFF_MLV, OFF_MLO, OFF_MAQ, OFF_GATE = 24, 32, 40, 48, 56, 64
N_MAIN = 88 * LANES


def _cparams(*sem):
    return pltpu.CompilerParams(dimension_semantics=sem, vmem_limit_bytes=VMEM_LIMIT)


def _sigmoid(x):
    return 0.5 + 0.5 * jnp.tanh(0.5 * x)


def _layer_norm(z, g, b):
    mu = jnp.mean(z, axis=-1, keepdims=True)
    zc = z - mu
    var = jnp.mean(zc * zc, axis=-1, keepdims=True)
    return zc * lax.rsqrt(var + 1e-5) * g + b


def _proj_in_kernel(x_ref, w_ref, b_ref, wg_ref, bg_ref, u_ref, g_ref, xs_ref):
    @pl.when(pl.program_id(1) == 0)
    def _():
        xb = x_ref[...].astype(BF16)
        xs_ref[...] = xb
        g_ref[...] = jnp.dot(xb, wg_ref[...], preferred_element_type=F32) + bg_ref[...]

    acc = jnp.dot(xs_ref[...], w_ref[...], preferred_element_type=F32)
    u_ref[...] = (acc + b_ref[...]).astype(u_ref.dtype)


def _proj_in(x2, w_main, b_main, w_g, b_g, tm, tn):
    t, k = x2.shape
    n = w_main.shape[1]
    return pl.pallas_call(
        _proj_in_kernel,
        grid=(t // tm, n // tn),
        in_specs=[
            pl.BlockSpec((tm, k), lambda i, j: (i, 0)),
            pl.BlockSpec((k, tn), lambda i, j: (0, j)),
            pl.BlockSpec((1, tn), lambda i, j: (0, j)),
            pl.BlockSpec((k, LANES), lambda i, j: (0, 0)),
            pl.BlockSpec((1, LANES), lambda i, j: (0, 0)),
        ],
        out_specs=[
            pl.BlockSpec((tm, tn), lambda i, j: (i, j)),
            pl.BlockSpec((tm, LANES), lambda i, j: (i, 0)),
        ],
        out_shape=[jax.ShapeDtypeStruct((t, n), BF16), jax.ShapeDtypeStruct((t, LANES), F32)],
        scratch_shapes=[pltpu.VMEM((tm, k), BF16)],
        compiler_params=_cparams("parallel", "arbitrary"),
        name="proj_in",
    )(x2, w_main, b_main, w_g, b_g)


def _mm_kernel(x_ref, w_ref, o_ref):
    o_ref[...] = jnp.dot(x_ref[...].astype(BF16), w_ref[...],
                         preferred_element_type=F32).astype(o_ref.dtype)


def _mm(x2, w, out_dtype, tm, name):
    m, k = x2.shape
    n = w.shape[1]
    return pl.pallas_call(
        _mm_kernel,
        grid=(m // tm,),
        in_specs=[pl.BlockSpec((tm, k), lambda i: (i, 0)), pl.BlockSpec((k, n), lambda i: (0, 0))],
        out_specs=pl.BlockSpec((tm, n), lambda i: (i, 0)),
        out_shape=jax.ShapeDtypeStruct((m, n), out_dtype),
        compiler_params=_cparams("parallel"),
        name=name,
    )(x2, w)


def _fold8(x, op):
    n = x.shape[0] // 8
    accs = [x[8 * a:8 * a + 8, :] for a in range(min(FOLD_ACCS, n))]
    for a in range(FOLD_ACCS, n):
        accs[a % FOLD_ACCS] = op(accs[a % FOLD_ACCS], x[8 * a:8 * a + 8, :])
    while len(accs) > 1:
        accs = [op(accs[a], accs[a + 1]) for a in range(0, len(accs), 2)]
    return accs[0]


def _dattn_kernel(rb_ref, q_ref, k_ref, v_ref, bkt_ref, dl_ref, g_ref, o_ref,
                  st_ref, pt_ref, vt_ref, nb_ref, *, tq, seq, lam_init):
    h = pl.program_id(0)
    nq = seq // tq
    nt = (((1,), (1,)), ((), ()))

    @pl.when(pl.program_id(1) == 0)
    def _():
        far = rb_ref[REL_BUCKETS - 1, h]
        for t in range(2):
            bk = bkt_ref[t]
            tile = jnp.full((tq, tq), NEG, F32)
            for bb in range(REL_BUCKETS):
                tile = jnp.where(bk == bb, (rb_ref[bb, h] - far) * LOG2E, tile)
            nb_ref[t * tq:(t + 1) * tq, :] = tile

    for j in range(nq):
        vt_ref[:, j * tq:(j + 1) * tq] = v_ref[0, j * tq:(j + 1) * tq, :].astype(F32).T.astype(BF16)

    dl = dl_ref[...]
    lam = (jnp.exp(jnp.sum(dl[0:1] * dl[1:2], axis=1, keepdims=True))
           - jnp.exp(jnp.sum(dl[2:3] * dl[3:4], axis=1, keepdims=True)) + lam_init)
    lane = lax.broadcasted_iota(I32, (tq, LANES), 1)
    gain = g_ref[...] * (1.0 - lam_init)

    def logits_pass(i):
        qs = q_ref[0, i * tq:(i + 1) * tq, :] * jnp.asarray(DA_HEAD_DIM ** -0.5 * LOG2E, BF16)
        zero = jnp.zeros_like(qs)
        qm = (jnp.where(lane < DA_HEAD_DIM, qs, zero), jnp.where(lane >= DA_HEAD_DIM, qs, zero))
        mx = [None, None]

        def chunk(c):
            for half in range(2):
                s = lax.dot_general(k_ref[0, c * tq:(c + 1) * tq, :], qm[half], nt,
                                    preferred_element_type=F32)
                if c >= i - 1:
                    s = s + nb_ref[(c - i + 1) * tq:(c - i + 2) * tq, :]
                st_ref[2 * (i % 2) + half, c * tq:(c + 1) * tq, :] = s
                f = _fold8(s, jnp.maximum)
                mx[half] = f if mx[half] is None else jnp.maximum(mx[half], f)

        def result():
            return [jnp.max(mx[half], axis=0, keepdims=True) for half in range(2)]

        return [functools.partial(chunk, c) for c in range(i + 1)], result

    def exp_pass(i, m):
        l8 = [None, None]

        def chunk(c):
            for half in range(2):
                p = jnp.exp2(st_ref[2 * (i % 2) + half, c * tq:(c + 1) * tq, :] - m[half])
                pt_ref[2 * (i % 2) + half, c * tq:(c + 1) * tq, :] = p.astype(BF16)
                f = _fold8(p, jnp.add)
                l8[half] = f if l8[half] is None else l8[half] + f

        def result():
            return [jnp.sum(l8[half], axis=0, keepdims=True) for half in range(2)]

        return [functools.partial(chunk, c) for c in range(i + 1)], result

    def value_pass(i, l1, l2):
        kv = (i + 1) * tq
        acc = [jnp.dot(vt_ref[:, 0:kv], pt_ref[2 * (i % 2) + half, 0:kv, :],
                       preferred_element_type=F32) for half in range(2)]
        o = (acc[0] * (1.0 / l1) - acc[1] * (lam / l2)).T
        ms = jnp.mean(o * o, axis=1, keepdims=True)
        y = o * lax.rsqrt(ms + 1e-5) * gain
        o_ref[0, i * tq:(i + 1) * tq, :] = y.astype(o_ref.dtype)

    chunks, result = logits_pass(0)
    for run in chunks:
        run()
    m_next = result()
    for i in range(nq):
        exp_chunks, exp_result = exp_pass(i, m_next)
        next_chunks, next_result = logits_pass(i + 1) if i + 1 < nq else ([], None)
        for c in range(max(len(exp_chunks), len(next_chunks))):
            if c < len(next_chunks):
                next_chunks[c]()
            if c < len(exp_chunks):
                exp_chunks[c]()
        if next_result is not None:
            m_next = next_result()
        value_pass(i, *exp_result())


def _t5_bucket_np(dist):
    n = np.maximum(dist, 0)
    max_exact = REL_BUCKETS // 2
    large = max_exact + (np.log(np.maximum(n, 1).astype(np.float32) / max_exact)
                         / math.log(REL_MAX_DIST / max_exact)
                         * (REL_BUCKETS - max_exact)).astype(np.int32)
    large = np.minimum(large, REL_BUCKETS - 1)
    return np.where(n < max_exact, n, large)


def _near_buckets(tq):
    assert _t5_bucket_np(np.array([tq + 1]))[0] == REL_BUCKETS - 1
    c = np.arange(tq)[:, None]
    r = np.arange(tq)[None, :]
    left = _t5_bucket_np(r + tq - c)
    diag = np.where(r >= c, _t5_bucket_np(r - c), -1)
    return np.stack([left, diag], axis=0).astype(np.int32)


def _diff_attention(u3, rel_bias, diff_lambda, subln_g, tq, lam_init):
    b, s, _ = u3.shape
    kern = functools.partial(_dattn_kernel, tq=tq, seq=s, lam_init=lam_init)
    grid_spec = pltpu.PrefetchScalarGridSpec(
        num_scalar_prefetch=1,
        grid=(DA_HEADS, b),
        in_specs=[
            pl.BlockSpec((1, s, LANES), lambda h, bi, rb: (bi, 0, OFF_DAQ + h)),
            pl.BlockSpec((1, s, LANES), lambda h, bi, rb: (bi, 0, OFF_DAK + h)),
            pl.BlockSpec((1, s, LANES), lambda h, bi, rb: (bi, 0, OFF_DAV + h)),
            pl.BlockSpec((2, tq, tq), lambda h, bi, rb: (0, 0, 0)),
            pl.BlockSpec((4, DA_HEAD_DIM), lambda h, bi, rb: (0, 0)),
            pl.BlockSpec((1, DA_V_DIM), lambda h, bi, rb: (0, 0)),
        ],
        out_specs=pl.BlockSpec((1, s, LANES), lambda h, bi, rb: (bi, 0, h)),
        scratch_shapes=[pltpu.VMEM((4, s, tq), F32), pltpu.VMEM((4, s, tq), BF16),
                        pltpu.VMEM((DA_V_DIM, s), BF16),
                        pltpu.VMEM((2 * tq, tq), F32)],
    )
    return pl.pallas_call(
        kern,
        grid_spec=grid_spec,
        out_shape=jax.ShapeDtypeStruct((b, s, DA_HEADS * DA_V_DIM), BF16),
        compiler_params=_cparams("arbitrary", "arbitrary"),
        name="diff_attn",
    )(rel_bias.astype(F32), u3, u3, u3, jnp.asarray(_near_buckets(tq)), diff_lambda, subln_g)


def _mlstm_kernel(q_ref, k_ref, v_ref, og_ref, g_ref, cwq_ref, cwk_ref, ng_ref, y_ref,
                  ct_ref, gts_ref, *, chunk, seq):
    first_head = pl.program_id(1) * ML_GROUP
    nc = seq // chunk
    hd = ML_HEAD_DIM
    lane = lax.broadcasted_iota(jnp.int32, (chunk, LANES), 1)
    row = lax.broadcasted_iota(jnp.int32, (chunk, chunk), 0)
    col = lax.broadcasted_iota(jnp.int32, (chunk, chunk), 1)
    causal = row >= col
    tri = causal.astype(F32)
    ct_ref[...] = jnp.zeros(ct_ref.shape, F32)

    def conv_silu(ref, cw_ref, r0, c, cols):
        cw = cw_ref[:, cols]
        cur = ref[0, pl.ds(r0, chunk), cols].astype(F32)
        p0 = pl.multiple_of(jnp.maximum(r0 - 16, 0), 16)
        prev = ref[0, pl.ds(p0, 16), cols].astype(F32)
        prev = jnp.where(c > 0, prev, 0.0)
        x = jnp.concatenate([prev, cur], axis=0)
        out = cw[ML_CONV - 1:ML_CONV] * cur
        for tap in range(ML_CONV - 1):
            shifted = pltpu.roll(x, ML_CONV - 1 - tap, 0)[16:]
            out = out + cw[tap:tap + 1] * shifted
        return out * _sigmoid(out)

    def head_step(j, c, r0, g, bc, m, n):
        h = first_head + j
        cols = slice(j * hd, (j + 1) * hd)
        irow = gts_ref[pl.ds(h, 1), :]
        brow = gts_ref[pl.ds(ML_HEADS + h, 1), :]
        bcol = jnp.sum(jnp.where(lane == ML_HEADS + h, bc, 0.0), axis=1, keepdims=True)
        icol = jnp.sum(jnp.where(lane == h, g, 0.0), axis=1, keepdims=True)

        dmat = jnp.where(causal, bcol - brow + irow, NEG)
        inter = bcol + m
        m_row = jnp.maximum(inter, jnp.max(dmat, axis=1, keepdims=True))

        q = conv_silu(q_ref, cwq_ref, r0, c, cols)
        k = conv_silu(k_ref, cwk_ref, r0, c, cols) * (hd ** -0.5)
        qb = q.astype(BF16)
        kb = k.astype(BF16)
        vb = v_ref[0, pl.ds(r0, chunk), cols]
        sqk = lax.dot_general(qb, kb, (((1,), (1,)), ((), ())), preferred_element_type=F32)
        w = jnp.exp(dmat - m_row) * sqk
        inter_w = jnp.exp(inter - m_row)
        ct = ct_ref[j]
        num = (inter_w * jnp.dot(qb, ct.astype(BF16), preferred_element_type=F32)
               + jnp.dot(w.astype(BF16), vb, preferred_element_type=F32))
        den = inter_w * jnp.sum(q * n, axis=1, keepdims=True) + jnp.sum(w, axis=1, keepdims=True)
        hout = num / jnp.maximum(jnp.abs(den), jnp.exp(-m_row))
        mu = jnp.mean(hout, axis=1, keepdims=True)
        hc = hout - mu
        var = jnp.mean(hc * hc, axis=1, keepdims=True)
        hn = hc * lax.rsqrt(var + 1e-5) * ng_ref[:, cols]
        og = og_ref[0, pl.ds(r0, chunk), cols].astype(F32)
        y_ref[0, pl.ds(r0, chunk), cols] = (_sigmoid(og) * hn).astype(y_ref.dtype)

        total = brow[:, chunk - 1:chunk]
        grow = total - brow + irow
        m_new = jnp.maximum(total + m, jnp.max(grow, axis=1, keepdims=True))
        decay = jnp.exp(total + m - m_new)
        ws = jnp.exp(total - bcol + icol - m_new)
        wsv = (ws * vb.astype(F32)).astype(BF16)
        ct_ref[j] = decay * ct + jnp.dot(k.T.astype(BF16), wsv, preferred_element_type=F32)
        n_new = decay * n + jnp.sum(ws * k, axis=0, keepdims=True)
        return m_new, n_new

    def body(c, carry):
        r0 = pl.multiple_of(c * chunk, chunk)
        g = g_ref[0, pl.ds(r0, chunk), :]
        logf = jnp.minimum(g, 0.0) - jnp.log(1.0 + jnp.exp(-jnp.abs(g)))
        bc = jnp.dot(tri, logf, precision=lax.Precision.HIGHEST, preferred_element_type=F32)
        gts_ref[...] = jnp.where(lane < ML_HEADS, g, bc).T
        return tuple(head_step(j, c, r0, g, bc, *carry[j]) for j in range(ML_GROUP))

    init = tuple((jnp.zeros((1, 1), F32), jnp.zeros((1, hd), F32)) for _ in range(ML_GROUP))
    lax.fori_loop(0, nc, body, init)


def _mlstm(u3, gates3, conv_w, norm_g, chunk):
    b, s, _ = u3.shape
    gw = ML_GROUP * ML_HEAD_DIM
    nq = gw // LANES
    ngroups = ML_HEADS // ML_GROUP
    kern = functools.partial(_mlstm_kernel, chunk=chunk, seq=s)

    def ublock(off):
        return pl.BlockSpec((1, s, gw), lambda bi, h: (bi, 0, off // nq + h))

    return pl.pallas_call(
        kern,
        grid=(b, ngroups),
        in_specs=[
            ublock(OFF_MLQ), ublock(OFF_MLK), ublock(OFF_MLV), ublock(OFF_MLO),
            pl.BlockSpec((1, s, LANES), lambda bi, h: (bi, 0, 0)),
            pl.BlockSpec((ML_CONV, gw), lambda bi, h: (0, h)),
            pl.BlockSpec((ML_CONV, gw), lambda bi, h: (0, ngroups + h)),
            pl.BlockSpec((1, gw), lambda bi, h: (0, h)),
        ],
        out_specs=pl.BlockSpec((1, s, gw), lambda bi, h: (bi, 0, h)),
        out_shape=jax.ShapeDtypeStruct((b, s, ML_HEADS * ML_HEAD_DIM), BF16),
        scratch_shapes=[pltpu.VMEM((ML_GROUP, ML_HEAD_DIM, ML_HEAD_DIM), F32),
                        pltpu.VMEM((LANES, chunk), F32)],
        compiler_params=_cparams("parallel", "arbitrary"),
        name="mlstm",
    )(u3, u3, u3, u3, gates3, conv_w, conv_w, norm_g)


def _memattn_kernel(q_ref, k_ref, v_ref, o_ref, *, tq, seq):
    kb = k_ref[0]
    vb = v_ref[0]
    scale = jnp.asarray(MA_HEAD_DIM ** -0.5, BF16)
    for t in range(seq // tq):
        q = q_ref[0, t * tq:(t + 1) * tq, :] * scale
        s = lax.dot_general(q, kb, (((1,), (1,)), ((), ())), preferred_element_type=F32)
        p = jnp.exp(s - jnp.max(s, axis=1, keepdims=True))
        inv = 1.0 / jnp.sum(p, axis=1, keepdims=True)
        o = jnp.dot(p.astype(BF16), vb, preferred_element_type=F32) * inv
        o_ref[0, t * tq:(t + 1) * tq, :] = o.astype(o_ref.dtype)


def _mem_attention(u3, kv3, tq):
    b, s, _ = u3.shape
    mlen = kv3.shape[1]
    hd = MA_HEAD_DIM
    nq = hd // LANES
    kern = functools.partial(_memattn_kernel, tq=tq, seq=s)
    return pl.pallas_call(
        kern,
        grid=(b, MA_HEADS),
        in_specs=[
            pl.BlockSpec((1, s, hd), lambda bi, h: (bi, 0, OFF_MAQ // nq + h)),
            pl.BlockSpec((1, mlen, hd), lambda bi, h: (bi, 0, h)),
            pl.BlockSpec((1, mlen, hd), lambda bi, h: (bi, 0, MA_HEADS + h)),
        ],
        out_specs=pl.BlockSpec((1, s, hd), lambda bi, h: (bi, 0, h)),
        out_shape=jax.ShapeDtypeStruct((b, s, MA_HEADS * hd), BF16),
        compiler_params=_cparams("parallel", "parallel"),
        name="mem_attn",
    )(u3, kv3, kv3)


def _merge_kernel(ya_ref, ym_ref, yc_ref, g0_ref, g1_ref, g2_ref, x_ref, wb_ref, wo_ref,
                  lg_ref, lb_ref, x1_ref, x1b_ref, x1p_ref):
    acc = None
    for n, (y_ref, g_ref) in enumerate(((ya_ref, g0_ref), (ym_ref, g1_ref), (yc_ref, g2_ref))):
        pr = jnp.dot(y_ref[...], wb_ref[n], preferred_element_type=F32)
        t = jax.nn.sigmoid(g_ref[...].astype(F32)) * pr
        acc = t if acc is None else acc + t
    out = jnp.dot(acc.astype(BF16), wo_ref[...], preferred_element_type=F32)
    x1 = _layer_norm(ALPHA * x_ref[...] + out, lg_ref[...], lb_ref[...])
    x1_ref[...] = x1
    x1b = x1.astype(BF16)
    x1b_ref[...] = x1b
    x1p_ref[...] = _pack_halves(x1b)


def _pack_halves(vb):
    w = vb.shape[1] // 2
    hi = lax.bitcast_convert_type(vb[:, :w].astype(F32), U32)
    lo = lax.bitcast_convert_type(vb[:, w:].astype(F32), U32)
    return hi | (lo >> 16)


def _unpack_halves(u):
    hi = lax.bitcast_convert_type(u & jnp.uint32(0xFFFF0000), F32)
    lo = lax.bitcast_convert_type(u << 16, F32)
    return hi, lo


def _merge(ya, ym, yc, u2, x2, wb, wo, lg, lb, tm):
    t, d = x2.shape
    gb = OFF_GATE * LANES // d

    def rows(i):
        return (i, 0)

    return pl.pallas_call(
        _merge_kernel,
        grid=(t // tm,),
        in_specs=[
            pl.BlockSpec((tm, d), rows), pl.BlockSpec((tm, d), rows), pl.BlockSpec((tm, d), rows),
            pl.BlockSpec((tm, d), lambda i: (i, gb)),
            pl.BlockSpec((tm, d), lambda i: (i, gb + 1)),
            pl.BlockSpec((tm, d), lambda i: (i, gb + 2)),
            pl.BlockSpec((tm, d), rows),
            pl.BlockSpec((N_BRANCH, d, d), lambda i: (0, 0, 0)),
            pl.BlockSpec((d, d), lambda i: (0, 0)),
            pl.BlockSpec((1, d), lambda i: (0, 0)),
            pl.BlockSpec((1, d), lambda i: (0, 0)),
        ],
        out_specs=[pl.BlockSpec((tm, d), rows), pl.BlockSpec((tm, d), rows),
                   pl.BlockSpec((tm, d // 2), rows)],
        out_shape=[jax.ShapeDtypeStruct((t, d), F32), jax.ShapeDtypeStruct((t, d), BF16),
                   jax.ShapeDtypeStruct((t, d // 2), U32)],
        compiler_params=_cparams("parallel"),
        name="merge_ln1",
    )(ya, ym, yc, u2, u2, u2, x2, wb, wo, lg, lb)


def _expert_kernel(ib_ref, ie_ref, lo_ref, hi_ref, first_ref, nit_ref, new_ref, slot_ref, next_ref,
                   x_ref, wg_hbm, wu_hbm, wd_hbm, after_hbm, o_ref, wgf_ref, wuf_ref, wdf_ref,
                   wgb_ref, wub_ref, wdb_ref, blk_ref, wsem):
    i = pl.program_id(0)

    def weight_copies(e, s):
        return (pltpu.make_async_copy(wg_hbm.at[e], wgf_ref.at[s], wsem.at[s]),
                pltpu.make_async_copy(wu_hbm.at[e], wuf_ref.at[s], wsem.at[s]),
                pltpu.make_async_copy(wd_hbm.at[e], wdf_ref.at[s], wsem.at[s]))

    @pl.when(i == 0)
    def _():
        blk_ref[...] = jnp.zeros(blk_ref.shape, blk_ref.dtype)
        for cp in weight_copies(ie_ref[0], 0):
            cp.start()

    @pl.when(new_ref[i] == 1)
    def _():
        s = slot_ref[i]
        for cp in weight_copies(ie_ref[i], s):
            cp.wait()

        @pl.when(next_ref[i] >= 0)
        def _():
            for cp in weight_copies(next_ref[i], 1 - s):
                cp.start()

        wgb_ref[...] = wgf_ref[s].astype(BF16)
        wub_ref[...] = wuf_ref[s].astype(BF16)
        wdb_ref[...] = wdf_ref[s].astype(BF16)

    @pl.when(i < nit_ref[0])
    def _():
        xl, xr = _unpack_halves(x_ref[...])
        xl = xl.astype(BF16)
        xr = xr.astype(BF16)
        half = xl.shape[1]

        def up(w_ref):
            return (jnp.dot(xl, w_ref[:half, :], preferred_element_type=F32)
                    + jnp.dot(xr, w_ref[half:, :], preferred_element_type=F32))

        hg = up(wgb_ref)
        hu = up(wub_ref)
        act = (hg * jax.nn.sigmoid(hg) * hu).astype(BF16)
        y = _pack_halves(jnp.dot(act, wdb_ref[...], preferred_element_type=F32).astype(BF16))
        row = lax.broadcasted_iota(I32, y.shape, 0)
        mine = (row >= lo_ref[i]) & (row < hi_ref[i])

        keep = jnp.where(first_ref[i] == 1, jnp.uint32(0), blk_ref[...])
        out = jnp.where(mine, y, keep)
        blk_ref[...] = out
        o_ref[...] = out


def _work_items(counts, n_pairs, bm):
    assert n_pairs % bm == 0
    nblocks = n_pairs // bm
    ends = jnp.cumsum(counts)
    starts = ends - counts
    first_blk = starts // bm
    n_e = jnp.where(counts > 0, (ends - 1) // bm - first_blk + 1, 0)
    item_end = jnp.cumsum(n_e)
    item_start = item_end - n_e
    n_items = item_end[-1]
    i = jnp.arange(nblocks + N_EXPERTS)
    valid = i < n_items
    e = jnp.minimum(jnp.sum(item_end[None, :] <= jnp.minimum(i, n_items - 1)[:, None], axis=1),
                    N_EXPERTS - 1)
    blk = jnp.where(valid, first_blk[e] + i - item_start[e], nblocks - 1)
    lo = jnp.clip(starts[e] - blk * bm, 0, bm)
    hi = jnp.where(valid, jnp.clip(ends[e] - blk * bm, 0, bm), 0)
    first = jnp.concatenate([jnp.ones((1,), I32), (blk[1:] != blk[:-1]).astype(I32)])
    new = jnp.where(valid, jnp.concatenate([jnp.ones((1,), bool), e[1:] != e[:-1]]), False)
    slot = (jnp.cumsum(new) - 1) % 2
    ids = jnp.arange(N_EXPERTS)
    later = jnp.where((counts > 0)[None, :] & (ids[None, :] > ids[:, None]), ids[None, :], N_EXPERTS)
    next_e = jnp.min(later, axis=1)
    nxt = jnp.where(next_e[e] < N_EXPERTS, next_e[e], -1)
    items = tuple(a.astype(I32) for a in (blk, e, lo, hi, first, n_items[None], new, slot, nxt))
    return items, starts.astype(I32)


def _experts(items, xs, w_gate, w_up, w_down, bm, after):
    n, dh = xs.shape
    d = 2 * dh
    de = w_gate.shape[2]

    def rows(i, ib, *_):
        return (ib[i], 0)

    grid_spec = pltpu.PrefetchScalarGridSpec(
        num_scalar_prefetch=len(items),
        grid=(items[0].shape[0],),
        in_specs=[
            pl.BlockSpec((bm, dh), rows),
            pl.BlockSpec(memory_space=pl.ANY),
            pl.BlockSpec(memory_space=pl.ANY),
            pl.BlockSpec(memory_space=pl.ANY),
            pl.BlockSpec(memory_space=pl.ANY),
        ],
        out_specs=pl.BlockSpec((bm, dh), rows),
        scratch_shapes=[pltpu.VMEM((2, d, de), F32), pltpu.VMEM((2, d, de), F32),
                        pltpu.VMEM((2, de, d), F32),
                        pltpu.VMEM((d, de), BF16), pltpu.VMEM((d, de), BF16),
                        pltpu.VMEM((de, d), BF16), pltpu.VMEM((bm, dh), U32),
                        pltpu.SemaphoreType.DMA((2,))],
    )
    return pl.pallas_call(
        _expert_kernel,
        grid_spec=grid_spec,
        out_shape=jax.ShapeDtypeStruct((n, dh), U32),
        compiler_params=_cparams("arbitrary"),
        name="experts",
    )(*items, xs, w_gate, w_up, w_down, after)


SC_WINDOW = 128
SC_WORKERS = 32


def _sc_worker():
    return lax.axis_index("core") * (SC_WORKERS // 2) + lax.axis_index("subcore")


def _sc_scatter_rows(x, idx, tm):
    t, dh = x.shape
    n = idx.shape[0]
    assert n == t * TOP_K and tm % SC_WINDOW == 0
    windows = t // SC_WINDOW
    per = windows // SC_WORKERS
    assert windows % SC_WORKERS == 0
    wpt = tm // SC_WINDOW
    mesh = plsc.VectorSubcoreMesh(core_axis_name="core", subcore_axis_name="subcore")

    @pl.kernel(out_type=jax.ShapeDtypeStruct((n, dh), x.dtype), mesh=mesh,
               scratch_types=[pltpu.VMEM((SC_WINDOW,), I32), pltpu.VMEM((SC_WINDOW, dh), x.dtype)])
    def scatter(x_hbm, idx_hbm, out_hbm, idx_vmem, rows_vmem):
        worker = _sc_worker()

        @pl.loop(0, per)
        def _(j):
            w = worker * per + j
            tile = w // wpt
            off = (w - tile * wpt) * SC_WINDOW
            pltpu.sync_copy(x_hbm.at[pl.ds(w * SC_WINDOW, SC_WINDOW)], rows_vmem)

            @pl.loop(0, TOP_K)
            def _(k):
                base = (tile * TOP_K + k) * tm + off
                pltpu.sync_copy(idx_hbm.at[pl.ds(base, SC_WINDOW)], idx_vmem)
                pltpu.sync_copy(rows_vmem, out_hbm.at[idx_vmem])

    return scatter(x, idx)


def _sc_gather_rows(src, idx):
    n = idx.shape[0]
    dh = src.shape[1]
    per = n // SC_WORKERS
    assert per % SC_WINDOW == 0
    mesh = plsc.VectorSubcoreMesh(core_axis_name="core", subcore_axis_name="subcore")

    @pl.kernel(out_type=jax.ShapeDtypeStruct((n, dh), src.dtype), mesh=mesh,
               scratch_types=[pltpu.VMEM((SC_WINDOW,), I32), pltpu.VMEM((SC_WINDOW, dh), src.dtype)])
    def gather(src_hbm, idx_hbm, out_hbm, idx_vmem, rows_vmem):
        worker = _sc_worker()

        @pl.loop(0, per // SC_WINDOW)
        def _(j):
            base = worker * per + j * SC_WINDOW
            pltpu.sync_copy(idx_hbm.at[pl.ds(base, SC_WINDOW)], idx_vmem)
            pltpu.sync_copy(src_hbm.at[idx_vmem], rows_vmem)
            pltpu.sync_copy(rows_vmem, out_hbm.at[pl.ds(base, SC_WINDOW)])

    return gather(src, idx)


def _shared_ffn_kernel(xb_ref, wg_ref, wu_ref, wd_ref, o_ref):
    xb = xb_ref[...]
    hg = jnp.dot(xb, wg_ref[...], preferred_element_type=F32)
    hu = jnp.dot(xb, wu_ref[...], preferred_element_type=F32)
    act = (hg * jax.nn.sigmoid(hg) * hu).astype(BF16)
    o_ref[...] = jnp.dot(act, wd_ref[...], preferred_element_type=F32)


def _shared_ffn(x1b, wg, wu, wd, tm):
    t, d = x1b.shape
    ds = wg.shape[1]
    return pl.pallas_call(
        _shared_ffn_kernel,
        grid=(t // tm,),
        in_specs=[pl.BlockSpec((tm, d), lambda i: (i, 0)), pl.BlockSpec((d, ds), lambda i: (0, 0)),
                  pl.BlockSpec((d, ds), lambda i: (0, 0)), pl.BlockSpec((ds, d), lambda i: (0, 0))],
        out_specs=pl.BlockSpec((tm, d), lambda i: (i, 0)),
        out_shape=jax.ShapeDtypeStruct((t, d), F32),
        compiler_params=_cparams("parallel"),
        name="shared_ffn",
    )(x1b, wg, wu, wd)


def _ffn_out_kernel(yt_ref, sh_ref, x1_ref, wt_ref, lg_ref, lb_ref, *rest):
    o_ref = rest[-1]
    wt = wt_ref[...]
    rl = None
    rr = None
    for k in range(TOP_K):
        hi, lo = _unpack_halves(yt_ref[0, k])
        wk = wt[:, k:k + 1]
        rl = wk * hi if rl is None else rl + wk * hi
        rr = wk * lo if rr is None else rr + wk * lo
    z = ALPHA * x1_ref[...] + sh_ref[...] + jnp.concatenate([rl, rr], axis=1)
    o_ref[...] = _layer_norm(z, lg_ref[...], lb_ref[...])


def _ffn_out(yt, sh, x1, wt, lg, lb, tm, tile0, prev):
    t, d = x1.shape
    dh = yt.shape[-1]
    assert yt.shape[1:] == (TOP_K, tm, dh)

    def rows(i):
        return (i + tile0, 0)

    def whole(i):
        return (0, 0)

    in_specs = [
        pl.BlockSpec((1, TOP_K, tm, dh), lambda i: (i, 0, 0, 0)),
        pl.BlockSpec((tm, d), rows), pl.BlockSpec((tm, d), rows), pl.BlockSpec((tm, LANES), rows),
        pl.BlockSpec((1, d), whole), pl.BlockSpec((1, d), whole),
    ]
    args = [yt, sh, x1, wt, lg, lb]
    aliases = {}
    if prev is not None:
        in_specs.append(pl.BlockSpec(memory_space=pl.ANY))
        args.append(prev)
        aliases = {len(args) - 1: 0}
    return pl.pallas_call(
        _ffn_out_kernel,
        grid=(yt.shape[0],),
        in_specs=in_specs,
        out_specs=pl.BlockSpec((tm, d), rows),
        out_shape=jax.ShapeDtypeStruct((t, d), F32),
        input_output_aliases=aliases,
        compiler_params=_cparams("parallel"),
        name="ffn_out_ln2",
    )(*args)


def _route_kernel(xb_ref, wrt_ref, rb_ref, ek_ref, rk_ref, wt_ref, cnt_ref, upper_ref, run_ref, *,
                  tm):
    i = pl.program_id(0)
    gsz = N_EXPERTS // N_GROUP
    ninf = -jnp.inf

    @pl.when(i == 0)
    def _():
        r = lax.broadcasted_iota(I32, (tm, tm), 0)
        c = lax.broadcasted_iota(I32, (tm, tm), 1)
        upper_ref[...] = jnp.where(r < c, 1.0, 0.0).astype(BF16)
        run_ref[...] = jnp.zeros(run_ref.shape, F32)

    logits = lax.dot_general(wrt_ref[...], xb_ref[...], (((1,), (1,)), ((), ())),
                             preferred_element_type=F32)
    scores = jax.nn.sigmoid(logits)
    choice = scores + rb_ref[...]

    ridx = lax.broadcasted_iota(I32, (gsz, tm), 0)
    gscore = []
    for g in range(N_GROUP):
        blk = choice[g * gsz:(g + 1) * gsz, :]
        m1 = jnp.max(blk, axis=0, keepdims=True)
        i1 = jnp.min(jnp.where(blk == m1, ridx, gsz), axis=0, keepdims=True)
        m2 = jnp.max(jnp.where(ridx == i1, ninf, blk), axis=0, keepdims=True)
        gscore.append(m1 + m2)
    masked = []
    for g in range(N_GROUP):
        beaten = jnp.zeros((1, tm), I32)
        for g2 in range(N_GROUP):
            if g2 == g:
                continue
            wins = (gscore[g2] >= gscore[g]) if g2 < g else (gscore[g2] > gscore[g])
            beaten = beaten + jnp.where(wins, 1, 0)
        masked.append(jnp.where(beaten < TOPK_GROUP, choice[g * gsz:(g + 1) * gsz, :], ninf))
    v = jnp.concatenate(masked, axis=0)

    eidx = lax.broadcasted_iota(I32, (N_EXPERTS, tm), 0)
    sel = jnp.zeros((N_EXPERTS, tm), F32)
    e_rows = []
    s_rows = []
    for k in range(TOP_K):
        m = jnp.max(v, axis=0, keepdims=True)
        ik = jnp.min(jnp.where(v == m, eidx, N_EXPERTS), axis=0, keepdims=True)
        hit = eidx == ik
        e_rows.append(ik)
        s_rows.append(jnp.sum(jnp.where(hit, scores, 0.0), axis=0, keepdims=True))
        v = jnp.where(hit, ninf, v)
        sel = jnp.where(hit, 1.0, sel)

    prefix = jnp.dot(sel.astype(BF16), upper_ref[...], preferred_element_type=F32)
    pos = prefix + run_ref[...]
    for k in range(TOP_K):
        rk = jnp.sum(jnp.where(eidx == e_rows[k], pos, 0.0), axis=0, keepdims=True)
        ek_ref[k:k + 1, :] = e_rows[k]
        rk_ref[k:k + 1, :] = rk.astype(I32)
    run_ref[...] = run_ref[...] + jnp.sum(sel, axis=1, keepdims=True)
    cnt_ref[...] = jnp.broadcast_to(run_ref[...], cnt_ref.shape).astype(I32)

    ssum = s_rows[0]
    for k in range(1, TOP_K):
        ssum = ssum + s_rows[k]
    w_rows = [s / (ssum + 1e-20) * ROUTED_SCALE for s in s_rows]
    w_rows.append(jnp.zeros((LANES - TOP_K, tm), F32))
    wt_ref[...] = jnp.concatenate(w_rows, axis=0).T


def _route(x1b, wrt, rbias, tm):
    t, d = x1b.shape
    return pl.pallas_call(
        functools.partial(_route_kernel, tm=tm),
        grid=(t // tm,),
        in_specs=[pl.BlockSpec((tm, d), lambda i: (i, 0)),
                  pl.BlockSpec((N_EXPERTS, d), lambda i: (0, 0)),
                  pl.BlockSpec((N_EXPERTS, 1), lambda i: (0, 0))],
        out_specs=[pl.BlockSpec((TOP_K, tm), lambda i: (0, i)),
                   pl.BlockSpec((TOP_K, tm), lambda i: (0, i)),
                   pl.BlockSpec((tm, LANES), lambda i: (i, 0)),
                   pl.BlockSpec((N_EXPERTS, LANES), lambda i: (0, 0))],
        out_shape=[jax.ShapeDtypeStruct((TOP_K, t), I32), jax.ShapeDtypeStruct((TOP_K, t), I32),
                   jax.ShapeDtypeStruct((t, LANES), F32),
                   jax.ShapeDtypeStruct((N_EXPERTS, LANES), I32)],
        scratch_shapes=[pltpu.VMEM((tm, tm), BF16), pltpu.VMEM((N_EXPERTS, 1), F32)],
        compiler_params=_cparams("arbitrary"),
        name="route",
    )(x1b, wrt, rbias)


def _dest_kernel(ps_ref, ek_ref, rk_ref, d_ref, *, tm):
    e = ek_ref[...]

    def body(j, acc):
        return acc + jnp.where(e == j, ps_ref[j], 0)

    res = lax.fori_loop(0, N_EXPERTS, body, jnp.zeros(e.shape, I32)) + rk_ref[...]
    for a in range(e.shape[1] // tm):
        for k in range(TOP_K):
            d_ref[a:a + 1, k * tm:(k + 1) * tm] = res[k:k + 1, a * tm:(a + 1) * tm]


def _dest(pstart, ek, rk, tm, tw):
    t = ek.shape[1]
    grid_spec = pltpu.PrefetchScalarGridSpec(
        num_scalar_prefetch=1,
        grid=(t // tw,),
        in_specs=[pl.BlockSpec((TOP_K, tw), lambda i, ps: (0, i)),
                  pl.BlockSpec((TOP_K, tw), lambda i, ps: (0, i))],
        out_specs=pl.BlockSpec((tw // tm, TOP_K * tm), lambda i, ps: (i, 0)),
    )
    return pl.pallas_call(
        functools.partial(_dest_kernel, tm=tm),
        grid_spec=grid_spec,
        out_shape=jax.ShapeDtypeStruct((t // tm, TOP_K * tm), I32),
        compiler_params=_cparams("parallel"),
        name="dest",
    )(pstart, ek, rk)


def _layer(x, mem, rel_bias, w_in, b_in, conv_w, diff_lambda, subln_g, mlstm_norm_g, w_mem_kv,
           w_branch, w_out, ln1_g, ln1_b, w_router, router_bias, w_e_gate, w_e_up, w_e_down,
           w_s_gate, w_s_up, w_s_down, ln2_g, ln2_b, layer_idx, cfg):
    b, s, d = x.shape
    t = b * s
    x2 = x.reshape(t, d)

    g0 = (OFF_MLO + 8) * LANES
    w_main = jnp.concatenate([w_in[:, :g0], w_in[:, g0 + 2 * ML_HEADS:]], axis=1).astype(BF16)
    b_main = jnp.concatenate([b_in[:g0], b_in[g0 + 2 * ML_HEADS:]])[None, :]
    w_g = jnp.pad(w_in[:, g0:g0 + 2 * ML_HEADS], ((0, 0), (0, LANES - 2 * ML_HEADS))).astype(BF16)
    b_g = jnp.pad(b_in[g0:g0 + 2 * ML_HEADS], (0, LANES - 2 * ML_HEADS))[None, :]

    u2, gates2 = _proj_in(x2, w_main, b_main, w_g, b_g, cfg["proj_tm"], cfg["proj_tn"])
    u3 = u2.reshape(b, s, N_MAIN)
    gates3 = gates2.reshape(b, s, LANES)

    lam_init = 0.8 - 0.6 * math.exp(-0.3 * layer_idx)
    y_a = _diff_attention(u3, rel_bias, diff_lambda, subln_g[None, :], cfg["attn_tq"], lam_init)
    y_m = _mlstm(u3, gates3, conv_w, mlstm_norm_g[None, :], cfg["ml_chunk"])
    kv = _mm(mem.reshape(-1, d), w_mem_kv.astype(BF16), BF16, cfg["kv_tm"], "mem_kv")
    y_c = _mem_attention(u3, kv.reshape(b, -1, 2 * MA_HEADS * MA_HEAD_DIM), cfg["ma_tq"])

    x1, x1b, x1p = _merge(y_a.reshape(t, d), y_m.reshape(t, d), y_c.reshape(t, d), u2, x2,
                          w_branch.astype(BF16), w_out.astype(BF16), ln1_g[None, :],
                          ln1_b[None, :], cfg["merge_tm"])

    ek, rk, wt, cnt = _route(x1b, w_router.T.astype(BF16), router_bias.astype(F32)[:, None],
                             cfg["route_tm"])
    bm = cfg["expert_bm"]
    items, starts = _work_items(cnt[:, 0], t * TOP_K, bm)
    dest = _dest(starts, ek, rk, cfg["moe_tm"], cfg["dest_tw"]).reshape(-1)
    xs = _sc_scatter_rows(x1p, dest, cfg["moe_tm"])
    sh = _shared_ffn(x1b, w_s_gate.astype(BF16), w_s_up.astype(BF16), w_s_down.astype(BF16),
                     cfg["shared_tm"])
    ys = _experts(items, xs, w_e_gate, w_e_up, w_e_down, bm, after=sh)
    tm = cfg["moe_tm"]
    per = t // tm // cfg["combine_chunks"]
    out = None
    for c in range(cfg["combine_chunks"]):
        rows = dest[c * per * TOP_K * tm:(c + 1) * per * TOP_K * tm]
        yt = _sc_gather_rows(ys, rows).reshape(per, TOP_K, tm, d // 2)
        out = _ffn_out(yt, sh, x1, wt, ln2_g[None, :], ln2_b[None, :], tm, c * per, out)
    return out.reshape(b, s, d)


def _config(b, s):
    t = b * s
    return {
        "proj_tm": min(1024, t), "proj_tn": 1024,
        "attn_tq": 256, "ml_chunk": 256, "kv_tm": 512, "ma_tq": min(512, s),
        "merge_tm": 256, "route_tm": 512, "expert_bm": 512, "moe_tm": 256, "dest_tw": 2048,
        "shared_tm": 512, "combine_chunks": 4,
    }


def kernel(x, mem, rel_bias, w_in, b_in, conv_w, diff_lambda, subln_g, mlstm_norm_g, w_mem_kv,
           w_branch, w_out, ln1_g, ln1_b, w_router, router_bias, w_e_gate, w_e_up, w_e_down,
           w_s_gate, w_s_up, w_s_down, ln2_g, ln2_b):
    cfg = _config(x.shape[0], x.shape[1])
    for l in range(DEPTH):
        x = _layer(x, mem, rel_bias, w_in[l], b_in[l], conv_w[l], diff_lambda[l], subln_g[l],
                   mlstm_norm_g[l], w_mem_kv[l], w_branch[l], w_out[l], ln1_g[l], ln1_b[l],
                   w_router[l], router_bias[l], w_e_gate[l], w_e_up[l], w_e_down[l],
                   w_s_gate[l], w_s_up[l], w_s_down[l], ln2_g[l], ln2_b[l], l, cfg)
    return x
```

```python
import functools
import math

import numpy as np
import jax
import jax.numpy as jnp
from jax import lax
from jax.experimental import pallas as pl
from jax.experimental.pallas import tpu as pltpu
from jax.experimental.pallas import tpu_sc as plsc

F32 = jnp.float32
BF16 = jnp.bfloat16
U32 = jnp.uint32
I32 = jnp.int32

D_MODEL = 1024
DEPTH = 1
DA_HEAD_DIM = 64
DA_V_DIM = 128
DA_HEADS = 8
ML_HEADS = 4
ML_HEAD_DIM = 256
ML_CONV = 4
ML_GROUP = 2
MA_HEADS = 4
MA_HEAD_DIM = 256
N_BRANCH = 3
REL_BUCKETS = 32
REL_MAX_DIST = 128
N_EXPERTS = 256
TOP_K = 8
N_GROUP = 8
TOPK_GROUP = 4
D_EXPERT = 256
ROUTED_SCALE = 2.5
ALPHA = (2.0 * DEPTH) ** 0.25

LANES = 128
NEG = -1e30
LOG2E = math.log2(math.e)
FOLD_ACCS = 4
VMEM_LIMIT = 56 * 1024 * 1024

OFF_DAQ, OFF_DAK, OFF_DAV = 0, 8, 16
OFF_MLQ, OFF_MLK, OFF_MLV, OFF_MLO, OFF_MAQ, OFF_GATE = 24, 32, 40, 48, 56, 64
N_MAIN = 88 * LANES


def _cparams(*sem):
    return pltpu.CompilerParams(dimension_semantics=sem, vmem_limit_bytes=VMEM_LIMIT)


def _sigmoid(x):
    return 0.5 + 0.5 * jnp.tanh(0.5 * x)


def _layer_norm(z, g, b):
    mu = jnp.mean(z, axis=-1, keepdims=True)
    zc = z - mu
    var = jnp.mean(zc * zc, axis=-1, keepdims=True)
    return zc * lax.rsqrt(var + 1e-5) * g + b


def _proj_in_kernel(x_ref, w_ref, b_ref, wg_ref, bg_ref, u_ref, g_ref, xs_ref):
    @pl.when(pl.program_id(1) == 0)
    def _():
        xb = x_ref[...].astype(BF16)
        xs_ref[...] = xb
        g_ref[...] = jnp.dot(xb, wg_ref[...], preferred_element_type=F32) + bg_ref[...]

    acc = jnp.dot(xs_ref[...], w_ref[...], preferred_element_type=F32)
    u_ref[...] = (acc + b_ref[...]).astype(u_ref.dtype)


def _proj_in(x2, w_main, b_main, w_g, b_g, tm, tn):
    t, k = x2.shape
    n = w_main.shape[1]
    return pl.pallas_call(
        _proj_in_kernel,
        grid=(t // tm, n // tn),
        in_specs=[
            pl.BlockSpec((tm, k), lambda i, j: (i, 0)),
            pl.BlockSpec((k, tn), lambda i, j: (0, j)),
            pl.BlockSpec((1, tn), lambda i, j: (0, j)),
            pl.BlockSpec((k, LANES), lambda i, j: (0, 0)),
            pl.BlockSpec((1, LANES), lambda i, j: (0, 0)),
        ],
        out_specs=[
            pl.BlockSpec((tm, tn), lambda i, j: (i, j)),
            pl.BlockSpec((tm, LANES), lambda i, j: (i, 0)),
        ],
        out_shape=[jax.ShapeDtypeStruct((t, n), BF16), jax.ShapeDtypeStruct((t, LANES), F32)],
        scratch_shapes=[pltpu.VMEM((tm, k), BF16)],
        compiler_params=_cparams("parallel", "arbitrary"),
        name="proj_in",
    )(x2, w_main, b_main, w_g, b_g)


def _mm_kernel(x_ref, w_ref, o_ref):
    o_ref[...] = jnp.dot(x_ref[...].astype(BF16), w_ref[...],
                         preferred_element_type=F32).astype(o_ref.dtype)


def _mm(x2, w, out_dtype, tm, name):
    m, k = x2.shape
    n = w.shape[1]
    return pl.pallas_call(
        _mm_kernel,
        grid=(m // tm,),
        in_specs=[pl.BlockSpec((tm, k), lambda i: (i, 0)), pl.BlockSpec((k, n), lambda i: (0, 0))],
        out_specs=pl.BlockSpec((tm, n), lambda i: (i, 0)),
        out_shape=jax.ShapeDtypeStruct((m, n), out_dtype),
        compiler_params=_cparams("parallel"),
        name=name,
    )(x2, w)


def _fold8(x, op):
    n = x.shape[0] // 8
    accs = [x[8 * a:8 * a + 8, :] for a in range(min(FOLD_ACCS, n))]
    for a in range(FOLD_ACCS, n):
        accs[a % FOLD_ACCS] = op(accs[a % FOLD_ACCS], x[8 * a:8 * a + 8, :])
    while len(accs) > 1:
        accs = [op(accs[a], accs[a + 1]) for a in range(0, len(accs), 2)]
    return accs[0]


def _dattn_kernel(rb_ref, q_ref, k_ref, v_ref, bkt_ref, dl_ref, g_ref, o_ref,
                  st_ref, pt_ref, vt_ref, nb_ref, *, tq, seq, lam_init):
    h = pl.program_id(0)
    nq = seq // tq
    nt = (((1,), (1,)), ((), ()))

    @pl.when(pl.program_id(1) == 0)
    def _():
        far = rb_ref[REL_BUCKETS - 1, h]
        for t in range(2):
            bk = bkt_ref[t]
            tile = jnp.full((tq, tq), NEG, F32)
            for bb in range(REL_BUCKETS):
                tile = jnp.where(bk == bb, (rb_ref[bb, h] - far) * LOG2E, tile)
            nb_ref[t * tq:(t + 1) * tq, :] = tile

    for j in range(nq):
        vt_ref[:, j * tq:(j + 1) * tq] = v_ref[0, j * tq:(j + 1) * tq, :].astype(F32).T.astype(BF16)

    dl = dl_ref[...]
    lam = (jnp.exp(jnp.sum(dl[0:1] * dl[1:2], axis=1, keepdims=True))
           - jnp.exp(jnp.sum(dl[2:3] * dl[3:4], axis=1, keepdims=True)) + lam_init)
    lane = lax.broadcasted_iota(I32, (tq, LANES), 1)
    gain = g_ref[...] * (1.0 - lam_init)

    def logits_pass(i):
        qs = q_ref[0, i * tq:(i + 1) * tq, :] * jnp.asarray(DA_HEAD_DIM ** -0.5 * LOG2E, BF16)
        zero = jnp.zeros_like(qs)
        qm = (jnp.where(lane < DA_HEAD_DIM, qs, zero), jnp.where(lane >= DA_HEAD_DIM, qs, zero))
        mx = [None, None]

        def chunk(c):
            for half in range(2):
                s = lax.dot_general(k_ref[0, c * tq:(c + 1) * tq, :], qm[half], nt,
                                    preferred_element_type=F32)
                if c >= i - 1:
                    s = s + nb_ref[(c - i + 1) * tq:(c - i + 2) * tq, :]
                st_ref[2 * (i % 2) + half, c * tq:(c + 1) * tq, :] = s
                f = _fold8(s, jnp.maximum)
                mx[half] = f if mx[half] is None else jnp.maximum(mx[half], f)

        def result():
            return [jnp.max(mx[half], axis=0, keepdims=True) for half in range(2)]

        return [functools.partial(chunk, c) for c in range(i + 1)], result

    def exp_pass(i, m):
        l8 = [None, None]

        def chunk(c):
            for half in range(2):
                p = jnp.exp2(st_ref[2 * (i % 2) + half, c * tq:(c + 1) * tq, :] - m[half])
                pt_ref[2 * (i % 2) + half, c * tq:(c + 1) * tq, :] = p.astype(BF16)
                f = _fold8(p, jnp.add)
                l8[half] = f if l8[half] is None else l8[half] + f

        def result():
            return [jnp.sum(l8[half], axis=0, keepdims=True) for half in range(2)]

        return [functools.partial(chunk, c) for c in range(i + 1)], result

    def value_pass(i, l1, l2):
        kv = (i + 1) * tq
        acc = [jnp.dot(vt_ref[:, 0:kv], pt_ref[2 * (i % 2) + half, 0:kv, :],
                       preferred_element_type=F32) for half in range(2)]
        o = (acc[0] * (1.0 / l1) - acc[1] * (lam / l2)).T
        ms = jnp.mean(o * o, axis=1, keepdims=True)
        y = o * lax.rsqrt(ms + 1e-5) * gain
        o_ref[0, i * tq:(i + 1) * tq, :] = y.astype(o_ref.dtype)

    chunks, result = logits_pass(0)
    for run in chunks:
        run()
    m_next = result()
    for i in range(nq):
        exp_chunks, exp_result = exp_pass(i, m_next)
        next_chunks, next_result = logits_pass(i + 1) if i + 1 < nq else ([], None)
        for c in range(max(len(exp_chunks), len(next_chunks))):
            if c < len(next_chunks):
                next_chunks[c]()
            if c < len(exp_chunks):
                exp_chunks[c]()
        if next_result is not None:
            m_next = next_result()
        value_pass(i, *exp_result())


def _t5_bucket_np(dist):
    n = np.maximum(dist, 0)
    max_exact = REL_BUCKETS // 2
    large = max_exact + (np.log(np.maximum(n, 1).astype(np.float32) / max_exact)
                         / math.log(REL_MAX_DIST / max_exact)
                         * (REL_BUCKETS - max_exact)).astype(np.int32)
    large = np.minimum(large, REL_BUCKETS - 1)
    return np.where(n < max_exact, n, large)


def _near_buckets(tq):
    assert _t5_bucket_np(np.array([tq + 1]))[0] == REL_BUCKETS - 1
    c = np.arange(tq)[:, None]
    r = np.arange(tq)[None, :]
    left = _t5_bucket_np(r + tq - c)
    diag = np.where(r >= c, _t5_bucket_np(r - c), -1)
    return np.stack([left, diag], axis=0).astype(np.int32)


def _diff_attention(u3, rel_bias, diff_lambda, subln_g, tq, lam_init):
    b, s, _ = u3.shape
    kern = functools.partial(_dattn_kernel, tq=tq, seq=s, lam_init=lam_init)
    grid_spec = pltpu.PrefetchScalarGridSpec(
        num_scalar_prefetch=1,
        grid=(DA_HEADS, b),
        in_specs=[
            pl.BlockSpec((1, s, LANES), lambda h, bi, rb: (bi, 0, OFF_DAQ + h)),
            pl.BlockSpec((1, s, LANES), lambda h, bi, rb: (bi, 0, OFF_DAK + h)),
            pl.BlockSpec((1, s, LANES), lambda h, bi, rb: (bi, 0, OFF_DAV + h)),
            pl.BlockSpec((2, tq, tq), lambda h, bi, rb: (0, 0, 0)),
            pl.BlockSpec((4, DA_HEAD_DIM), lambda h, bi, rb: (0, 0)),
            pl.BlockSpec((1, DA_V_DIM), lambda h, bi, rb: (0, 0)),
        ],
        out_specs=pl.BlockSpec((1, s, LANES), lambda h, bi, rb: (bi, 0, h)),
        scratch_shapes=[pltpu.VMEM((4, s, tq), F32), pltpu.VMEM((4, s, tq), BF16),
                        pltpu.VMEM((DA_V_DIM, s), BF16),
                        pltpu.VMEM((2 * tq, tq), F32)],
    )
    return pl.pallas_call(
        kern,
        grid_spec=grid_spec,
        out_shape=jax.ShapeDtypeStruct((b, s, DA_HEADS * DA_V_DIM), BF16),
        compiler_params=_cparams("arbitrary", "arbitrary"),
        name="diff_attn",
    )(rel_bias.astype(F32), u3, u3, u3, jnp.asarray(_near_buckets(tq)), diff_lambda, subln_g)


def _mlstm_kernel(q_ref, k_ref, v_ref, og_ref, g_ref, cwq_ref, cwk_ref, ng_ref, y_ref,
                  ct_ref, gts_ref, *, chunk, seq):
    first_head = pl.program_id(1) * ML_GROUP
    nc = seq // chunk
    hd = ML_HEAD_DIM
    lane = lax.broadcasted_iota(jnp.int32, (chunk, LANES), 1)
    row = lax.broadcasted_iota(jnp.int32, (chunk, chunk), 0)
    col = lax.broadcasted_iota(jnp.int32, (chunk, chunk), 1)
    causal = row >= col
    tri = causal.astype(F32)
    ct_ref[...] = jnp.zeros(ct_ref.shape, F32)

    def conv_silu(ref, cw_ref, r0, c, cols):
        cw = cw_ref[:, cols]
        cur = ref[0, pl.ds(r0, chunk), cols].astype(F32)
        p0 = pl.multiple_of(jnp.maximum(r0 - 16, 0), 16)
        prev = ref[0, pl.ds(p0, 16), cols].astype(F32)
        prev = jnp.where(c > 0, prev, 0.0)
        x = jnp.concatenate([prev, cur], axis=0)
        out = cw[ML_CONV - 1:ML_CONV] * cur
        for tap in range(ML_CONV - 1):
            shifted = pltpu.roll(x, ML_CONV - 1 - tap, 0)[16:]
            out = out + cw[tap:tap + 1] * shifted
        return out * _sigmoid(out)

    def head_step(j, c, r0, g, bc, m, n):
        h = first_head + j
        cols = slice(j * hd, (j + 1) * hd)
        irow = gts_ref[pl.ds(h, 1), :]
        brow = gts_ref[pl.ds(ML_HEADS + h, 1), :]
        bcol = jnp.sum(jnp.where(lane == ML_HEADS + h, bc, 0.0), axis=1, keepdims=True)
        icol = jnp.sum(jnp.where(lane == h, g, 0.0), axis=1, keepdims=True)

        dmat = jnp.where(causal, bcol - brow + irow, NEG)
        inter = bcol + m
        m_row = jnp.maximum(inter, jnp.max(dmat, axis=1, keepdims=True))

        q = conv_silu(q_ref, cwq_ref, r0, c, cols)
        k = conv_silu(k_ref, cwk_ref, r0, c, cols) * (hd ** -0.5)
        qb = q.astype(BF16)
        kb = k.astype(BF16)
        vb = v_ref[0, pl.ds(r0, chunk), cols]
        sqk = lax.dot_general(qb, kb, (((1,), (1,)), ((), ())), preferred_element_type=F32)
        w = jnp.exp(dmat - m_row) * sqk
        inter_w = jnp.exp(inter - m_row)
        ct = ct_ref[j]
        num = (inter_w * jnp.dot(qb, ct.astype(BF16), preferred_element_type=F32)
               + jnp.dot(w.astype(BF16), vb, preferred_element_type=F32))
        den = inter_w * jnp.sum(q * n, axis=1, keepdims=True) + jnp.sum(w, axis=1, keepdims=True)
        hout = num / jnp.maximum(jnp.abs(den), jnp.exp(-m_row))
        mu = jnp.mean(hout, axis=1, keepdims=True)
        hc = hout - mu
        var = jnp.mean(hc * hc, axis=1, keepdims=True)
        hn = hc * lax.rsqrt(var + 1e-5) * ng_ref[:, cols]
        og = og_ref[0, pl.ds(r0, chunk), cols].astype(F32)
        y_ref[0, pl.ds(r0, chunk), cols] = (_sigmoid(og) * hn).astype(y_ref.dtype)

        total = brow[:, chunk - 1:chunk]
        grow = total - brow + irow
        m_new = jnp.maximum(total + m, jnp.max(grow, axis=1, keepdims=True))
        decay = jnp.exp(total + m - m_new)
        ws = jnp.exp(total - bcol + icol - m_new)
        wsv = (ws * vb.astype(F32)).astype(BF16)
        ct_ref[j] = decay * ct + jnp.dot(k.T.astype(BF16), wsv, preferred_element_type=F32)
        n_new = decay * n + jnp.sum(ws * k, axis=0, keepdims=True)
        return m_new, n_new

    def body(c, carry):
        r0 = pl.multiple_of(c * chunk, chunk)
        g = g_ref[0, pl.ds(r0, chunk), :]
        logf = jnp.minimum(g, 0.0) - jnp.log(1.0 + jnp.exp(-jnp.abs(g)))
        bc = jnp.dot(tri, logf, precision=lax.Precision.HIGHEST, preferred_element_type=F32)
        gts_ref[...] = jnp.where(lane < ML_HEADS, g, bc).T
        return tuple(head_step(j, c, r0, g, bc, *carry[j]) for j in range(ML_GROUP))

    init = tuple((jnp.zeros((1, 1), F32), jnp.zeros((1, hd), F32)) for _ in range(ML_GROUP))
    lax.fori_loop(0, nc, body, init)


def _mlstm(u3, gates3, conv_w, norm_g, chunk):
    b, s, _ = u3.shape
    gw = ML_GROUP * ML_HEAD_DIM
    nq = gw // LANES
    ngroups = ML_HEADS // ML_GROUP
    kern = functools.partial(_mlstm_kernel, chunk=chunk, seq=s)

    def ublock(off):
        return pl.BlockSpec((1, s, gw), lambda bi, h: (bi, 0, off // nq + h))

    return pl.pallas_call(
        kern,
        grid=(b, ngroups),
        in_specs=[
            ublock(OFF_MLQ), ublock(OFF_MLK), ublock(OFF_MLV), ublock(OFF_MLO),
            pl.BlockSpec((1, s, LANES), lambda bi, h: (bi, 0, 0)),
            pl.BlockSpec((ML_CONV, gw), lambda bi, h: (0, h)),
            pl.BlockSpec((ML_CONV, gw), lambda bi, h: (0, ngroups + h)),
            pl.BlockSpec((1, gw), lambda bi, h: (0, h)),
        ],
        out_specs=pl.BlockSpec((1, s, gw), lambda bi, h: (bi, 0, h)),
        out_shape=jax.ShapeDtypeStruct((b, s, ML_HEADS * ML_HEAD_DIM), BF16),
        scratch_shapes=[pltpu.VMEM((ML_GROUP, ML_HEAD_DIM, ML_HEAD_DIM), F32),
                        pltpu.VMEM((LANES, chunk), F32)],
        compiler_params=_cparams("parallel", "arbitrary"),
        name="mlstm",
    )(u3, u3, u3, u3, gates3, conv_w, conv_w, norm_g)


def _memattn_kernel(q_ref, k_ref, v_ref, o_ref, *, tq, seq):
    kb = k_ref[0]
    vb = v_ref[0]
    scale = jnp.asarray(MA_HEAD_DIM ** -0.5, BF16)
    for t in range(seq // tq):
        q = q_ref[0, t * tq:(t + 1) * tq, :] * scale
        s = lax.dot_general(q, kb, (((1,), (1,)), ((), ())), preferred_element_type=F32)
        p = jnp.exp(s - jnp.max(s, axis=1, keepdims=True))
        inv = 1.0 / jnp.sum(p, axis=1, keepdims=True)
        o = jnp.dot(p.astype(BF16), vb, preferred_element_type=F32) * inv
        o_ref[0, t * tq:(t + 1) * tq, :] = o.astype(o_ref.dtype)


def _mem_attention(u3, kv3, tq):
    b, s, _ = u3.shape
    mlen = kv3.shape[1]
    hd = MA_HEAD_DIM
    nq = hd // LANES
    kern = functools.partial(_memattn_kernel, tq=tq, seq=s)
    return pl.pallas_call(
        kern,
        grid=(b, MA_HEADS),
        in_specs=[
            pl.BlockSpec((1, s, hd), lambda bi, h: (bi, 0, OFF_MAQ // nq + h)),
            pl.BlockSpec((1, mlen, hd), lambda bi, h: (bi, 0, h)),
            pl.BlockSpec((1, mlen, hd), lambda bi, h: (bi, 0, MA_HEADS + h)),
        ],
        out_specs=pl.BlockSpec((1, s, hd), lambda bi, h: (bi, 0, h)),
        out_shape=jax.ShapeDtypeStruct((b, s, MA_HEADS * hd), BF16),
        compiler_params=_cparams("parallel", "parallel"),
        name="mem_attn",
    )(u3, kv3, kv3)


def _merge_kernel(ya_ref, ym_ref, yc_ref, g0_ref, g1_ref, g2_ref, x_ref, wb_ref, wo_ref,
                  lg_ref, lb_ref, x1_ref, x1b_ref, x1p_ref):
    acc = None
    for n, (y_ref, g_ref) in enumerate(((ya_ref, g0_ref), (ym_ref, g1_ref), (yc_ref, g2_ref))):
        pr = jnp.dot(y_ref[...], wb_ref[n], preferred_element_type=F32)
        t = jax.nn.sigmoid(g_ref[...].astype(F32)) * pr
        acc = t if acc is None else acc + t
    out = jnp.dot(acc.astype(BF16), wo_ref[...], preferred_element_type=F32)
    x1 = _layer_norm(ALPHA * x_ref[...] + out, lg_ref[...], lb_ref[...])
    x1_ref[...] = x1
    x1b = x1.astype(BF16)
    x1b_ref[...] = x1b
    x1p_ref[...] = _pack_halves(x1b)


def _pack_halves(vb):
    w = vb.shape[1] // 2
    hi = lax.bitcast_convert_type(vb[:, :w].astype(F32), U32)
    lo = lax.bitcast_convert_type(vb[:, w:].astype(F32), U32)
    return hi | (lo >> 16)


def _unpack_halves(u):
    hi = lax.bitcast_convert_type(u & jnp.uint32(0xFFFF0000), F32)
    lo = lax.bitcast_convert_type(u << 16, F32)
    return hi, lo


def _merge(ya, ym, yc, u2, x2, wb, wo, lg, lb, tm):
    t, d = x2.shape
    gb = OFF_GATE * LANES // d

    def rows(i):
        return (i, 0)

    return pl.pallas_call(
        _merge_kernel,
        grid=(t // tm,),
        in_specs=[
            pl.BlockSpec((tm, d), rows), pl.BlockSpec((tm, d), rows), pl.BlockSpec((tm, d), rows),
            pl.BlockSpec((tm, d), lambda i: (i, gb)),
            pl.BlockSpec((tm, d), lambda i: (i, gb + 1)),
            pl.BlockSpec((tm, d), lambda i: (i, gb + 2)),
            pl.BlockSpec((tm, d), rows),
            pl.BlockSpec((N_BRANCH, d, d), lambda i: (0, 0, 0)),
            pl.BlockSpec((d, d), lambda i: (0, 0)),
            pl.BlockSpec((1, d), lambda i: (0, 0)),
            pl.BlockSpec((1, d), lambda i: (0, 0)),
        ],
        out_specs=[pl.BlockSpec((tm, d), rows), pl.BlockSpec((tm, d), rows),
                   pl.BlockSpec((tm, d // 2), rows)],
        out_shape=[jax.ShapeDtypeStruct((t, d), F32), jax.ShapeDtypeStruct((t, d), BF16),
                   jax.ShapeDtypeStruct((t, d // 2), U32)],
        compiler_params=_cparams("parallel"),
        name="merge_ln1",
    )(ya, ym, yc, u2, u2, u2, x2, wb, wo, lg, lb)


def _expert_kernel(ib_ref, ie_ref, lo_ref, hi_ref, first_ref, nit_ref, new_ref, slot_ref, next_ref,
                   x_ref, wg_hbm, wu_hbm, wd_hbm, after_hbm, o_ref, wgf_ref, wuf_ref, wdf_ref,
                   wgb_ref, wub_ref, wdb_ref, blk_ref, wsem):
    i = pl.program_id(0)

    def weight_copies(e, s):
        return (pltpu.make_async_copy(wg_hbm.at[e], wgf_ref.at[s], wsem.at[s]),
                pltpu.make_async_copy(wu_hbm.at[e], wuf_ref.at[s], wsem.at[s]),
                pltpu.make_async_copy(wd_hbm.at[e], wdf_ref.at[s], wsem.at[s]))

    @pl.when(i == 0)
    def _():
        blk_ref[...] = jnp.zeros(blk_ref.shape, blk_ref.dtype)
        for cp in weight_copies(ie_ref[0], 0):
            cp.start()

    @pl.when(new_ref[i] == 1)
    def _():
        s = slot_ref[i]
        for cp in weight_copies(ie_ref[i], s):
            cp.wait()

        @pl.when(next_ref[i] >= 0)
        def _():
            for cp in weight_copies(next_ref[i], 1 - s):
                cp.start()

        wgb_ref[...] = wgf_ref[s].astype(BF16)
        wub_ref[...] = wuf_ref[s].astype(BF16)
        wdb_ref[...] = wdf_ref[s].astype(BF16)

    @pl.when(i < nit_ref[0])
    def _():
        xl, xr = _unpack_halves(x_ref[...])
        xl = xl.astype(BF16)
        xr = xr.astype(BF16)
        half = xl.shape[1]

        def up(w_ref):
            return (jnp.dot(xl, w_ref[:half, :], preferred_element_type=F32)
                    + jnp.dot(xr, w_ref[half:, :], preferred_element_type=F32))

        hg = up(wgb_ref)
        hu = up(wub_ref)
        act = (hg * jax.nn.sigmoid(hg) * hu).astype(BF16)
        y = _pack_halves(jnp.dot(act, wdb_ref[...], preferred_element_type=F32).astype(BF16))
        row = lax.broadcasted_iota(I32, y.shape, 0)
        mine = (row >= lo_ref[i]) & (row < hi_ref[i])

        keep = jnp.where(first_ref[i] == 1, jnp.uint32(0), blk_ref[...])
        out = jnp.where(mine, y, keep)
        blk_ref[...] = out
        o_ref[...] = out


def _work_items(counts, n_pairs, bm):
    assert n_pairs % bm == 0
    nblocks = n_pairs // bm
    ends = jnp.cumsum(counts)
    starts = ends - counts
    first_blk = starts // bm
    n_e = jnp.where(counts > 0, (ends - 1) // bm - first_blk + 1, 0)
    item_end = jnp.cumsum(n_e)
    item_start = item_end - n_e
    n_items = item_end[-1]
    i = jnp.arange(nblocks + N_EXPERTS)
    valid = i < n_items
    e = jnp.minimum(jnp.sum(item_end[None, :] <= jnp.minimum(i, n_items - 1)[:, None], axis=1),
                    N_EXPERTS - 1)
    blk = jnp.where(valid, first_blk[e] + i - item_start[e], nblocks - 1)
    lo = jnp.clip(starts[e] - blk * bm, 0, bm)
    hi = jnp.where(valid, jnp.clip(ends[e] - blk * bm, 0, bm), 0)
    first = jnp.concatenate([jnp.ones((1,), I32), (blk[1:] != blk[:-1]).astype(I32)])
    new = jnp.where(valid, jnp.concatenate([jnp.ones((1,), bool), e[1:] != e[:-1]]), False)
    slot = (jnp.cumsum(new) - 1) % 2
    ids = jnp.arange(N_EXPERTS)
    later = jnp.where((counts > 0)[None, :] & (ids[None, :] > ids[:, None]), ids[None, :], N_EXPERTS)
    next_e = jnp.min(later, axis=1)
    nxt = jnp.where(next_e[e] < N_EXPERTS, next_e[e], -1)
    items = tuple(a.astype(I32) for a in (blk, e, lo, hi, first, n_items[None], new, slot, nxt))
    return items, starts.astype(I32)


def _experts(items, xs, w_gate, w_up, w_down, bm, after):
    n, dh = xs.shape
    d = 2 * dh
    de = w_gate.shape[2]

    def rows(i, ib, *_):
        return (ib[i], 0)

    grid_spec = pltpu.PrefetchScalarGridSpec(
        num_scalar_prefetch=len(items),
        grid=(items[0].shape[0],),
        in_specs=[
            pl.BlockSpec((bm, dh), rows),
            pl.BlockSpec(memory_space=pl.ANY),
            pl.BlockSpec(memory_space=pl.ANY),
            pl.BlockSpec(memory_space=pl.ANY),
            pl.BlockSpec(memory_space=pl.ANY),
        ],
        out_specs=pl.BlockSpec((bm, dh), rows),
        scratch_shapes=[pltpu.VMEM((2, d, de), F32), pltpu.VMEM((2, d, de), F32),
                        pltpu.VMEM((2, de, d), F32),
                        pltpu.VMEM((d, de), BF16), pltpu.VMEM((d, de), BF16),
                        pltpu.VMEM((de, d), BF16), pltpu.VMEM((bm, dh), U32),
                        pltpu.SemaphoreType.DMA((2,))],
    )
    return pl.pallas_call(
        _expert_kernel,
        grid_spec=grid_spec,
        out_shape=jax.ShapeDtypeStruct((n, dh), U32),
        compiler_params=_cparams("arbitrary"),
        name="experts",
    )(*items, xs, w_gate, w_up, w_down, after)


SC_WINDOW = 128
SC_WORKERS = 32


def _sc_worker():
    return lax.axis_index("core") * (SC_WORKERS // 2) + lax.axis_index("subcore")


def _sc_scatter_rows(x, idx, tm):
    t, dh = x.shape
    n = idx.shape[0]
    assert n == t * TOP_K and tm % SC_WINDOW == 0
    windows = t // SC_WINDOW
    per = windows // SC_WORKERS
    assert windows % SC_WORKERS == 0
    wpt = tm // SC_WINDOW
    mesh = plsc.VectorSubcoreMesh(core_axis_name="core", subcore_axis_name="subcore")

    @pl.kernel(out_type=jax.ShapeDtypeStruct((n, dh), x.dtype), mesh=mesh,
               scratch_types=[pltpu.VMEM((SC_WINDOW,), I32), pltpu.VMEM((SC_WINDOW, dh), x.dtype)])
    def scatter(x_hbm, idx_hbm, out_hbm, idx_vmem, rows_vmem):
        worker = _sc_worker()

        @pl.loop(0, per)
        def _(j):
            w = worker * per + j
            tile = w // wpt
            off = (w - tile * wpt) * SC_WINDOW
            pltpu.sync_copy(x_hbm.at[pl.ds(w * SC_WINDOW, SC_WINDOW)], rows_vmem)

            @pl.loop(0, TOP_K)
            def _(k):
                base = (tile * TOP_K + k) * tm + off
                pltpu.sync_copy(idx_hbm.at[pl.ds(base, SC_WINDOW)], idx_vmem)
                pltpu.sync_copy(rows_vmem, out_hbm.at[idx_vmem])

    return scatter(x, idx)


def _sc_gather_rows(src, idx):
    n = idx.shape[0]
    dh = src.shape[1]
    per = n // SC_WORKERS
    assert per % SC_WINDOW == 0
    mesh = plsc.VectorSubcoreMesh(core_axis_name="core", subcore_axis_name="subcore")

    @pl.kernel(out_type=jax.ShapeDtypeStruct((n, dh), src.dtype), mesh=mesh,
               scratch_types=[pltpu.VMEM((SC_WINDOW,), I32), pltpu.VMEM((SC_WINDOW, dh), src.dtype)])
    def gather(src_hbm, idx_hbm, out_hbm, idx_vmem, rows_vmem):
        worker = _sc_worker()

        @pl.loop(0, per // SC_WINDOW)
        def _(j):
            base = worker * per + j * SC_WINDOW
            pltpu.sync_copy(idx_hbm.at[pl.ds(base, SC_WINDOW)], idx_vmem)
            pltpu.sync_copy(src_hbm.at[idx_vmem], rows_vmem)
            pltpu.sync_copy(rows_vmem, out_hbm.at[pl.ds(base, SC_WINDOW)])

    return gather(src, idx)


def _shared_ffn_kernel(xb_ref, wg_ref, wu_ref, wd_ref, o_ref):
    xb = xb_ref[...]
    hg = jnp.dot(xb, wg_ref[...], preferred_element_type=F32)
    hu = jnp.dot(xb, wu_ref[...], preferred_element_type=F32)
    act = (hg * jax.nn.sigmoid(hg) * hu).astype(BF16)
    o_ref[...] = jnp.dot(act, wd_ref[...], preferred_element_type=F32)


def _shared_ffn(x1b, wg, wu, wd, tm):
    t, d = x1b.shape
    ds = wg.shape[1]
    return pl.pallas_call(
        _shared_ffn_kernel,
        grid=(t // tm,),
        in_specs=[pl.BlockSpec((tm, d), lambda i: (i, 0)), pl.BlockSpec((d, ds), lambda i: (0, 0)),
                  pl.BlockSpec((d, ds), lambda i: (0, 0)), pl.BlockSpec((ds, d), lambda i: (0, 0))],
        out_specs=pl.BlockSpec((tm, d), lambda i: (i, 0)),
        out_shape=jax.ShapeDtypeStruct((t, d), F32),
        compiler_params=_cparams("parallel"),
        name="shared_ffn",
    )(x1b, wg, wu, wd)


def _ffn_out_kernel(yt_ref, sh_ref, x1_ref, wt_ref, lg_ref, lb_ref, *rest):
    o_ref = rest[-1]
    wt = wt_ref[...]
    rl = None
    rr = None
    for k in range(TOP_K):
        hi, lo = _unpack_halves(yt_ref[0, k])
        wk = wt[:, k:k + 1]
        rl = wk * hi if rl is None else rl + wk * hi
        rr = wk * lo if rr is None else rr + wk * lo
    z = ALPHA * x1_ref[...] + sh_ref[...] + jnp.concatenate([rl, rr], axis=1)
    o_ref[...] = _layer_norm(z, lg_ref[...], lb_ref[...])


def _ffn_out(yt, sh, x1, wt, lg, lb, tm, tile0, prev):
    t, d = x1.shape
    dh = yt.shape[-1]
    assert yt.shape[1:] == (TOP_K, tm, dh)

    def rows(i):
        return (i + tile0, 0)

    def whole(i):
        return (0, 0)

    in_specs = [
        pl.BlockSpec((1, TOP_K, tm, dh), lambda i: (i, 0, 0, 0)),
        pl.BlockSpec((tm, d), rows), pl.BlockSpec((tm, d), rows), pl.BlockSpec((tm, LANES), rows),
        pl.BlockSpec((1, d), whole), pl.BlockSpec((1, d), whole),
    ]
    args = [yt, sh, x1, wt, lg, lb]
    aliases = {}
    if prev is not None:
        in_specs.append(pl.BlockSpec(memory_space=pl.ANY))
        args.append(prev)
        aliases = {len(args) - 1: 0}
    return pl.pallas_call(
        _ffn_out_kernel,
        grid=(yt.shape[0],),
        in_specs=in_specs,
        out_specs=pl.BlockSpec((tm, d), rows),
        out_shape=jax.ShapeDtypeStruct((t, d), F32),
        input_output_aliases=aliases,
        compiler_params=_cparams("parallel"),
        name="ffn_out_ln2",
    )(*args)


def _route_kernel(xb_ref, wrt_ref, rb_ref, ek_ref, rk_ref, wt_ref, cnt_ref, upper_ref, run_ref, *,
                  tm):
    i = pl.program_id(0)
    gsz = N_EXPERTS // N_GROUP
    ninf = -jnp.inf

    @pl.when(i == 0)
    def _():
        r = lax.broadcasted_iota(I32, (tm, tm), 0)
        c = lax.broadcasted_iota(I32, (tm, tm), 1)
        upper_ref[...] = jnp.where(r < c, 1.0, 0.0).astype(BF16)
        run_ref[...] = jnp.zeros(run_ref.shape, F32)

    logits = lax.dot_general(wrt_ref[...], xb_ref[...], (((1,), (1,)), ((), ())),
                             preferred_element_type=F32)
    scores = jax.nn.sigmoid(logits)
    choice = scores + rb_ref[...]

    ridx = lax.broadcasted_iota(I32, (gsz, tm), 0)
    gscore = []
    for g in range(N_GROUP):
        blk = choice[g * gsz:(g + 1) * gsz, :]
        m1 = jnp.max(blk, axis=0, keepdims=True)
        i1 = jnp.min(jnp.where(blk == m1, ridx, gsz), axis=0, keepdims=True)
        m2 = jnp.max(jnp.where(ridx == i1, ninf, blk), axis=0, keepdims=True)
        gscore.append(m1 + m2)
    masked = []
    for g in range(N_GROUP):
        beaten = jnp.zeros((1, tm), I32)
        for g2 in range(N_GROUP):
            if g2 == g:
                continue
            wins = (gscore[g2] >= gscore[g]) if g2 < g else (gscore[g2] > gscore[g])
            beaten = beaten + jnp.where(wins, 1, 0)
        masked.append(jnp.where(beaten < TOPK_GROUP, choice[g * gsz:(g + 1) * gsz, :], ninf))
    v = jnp.concatenate(masked, axis=0)

    eidx = lax.broadcasted_iota(I32, (N_EXPERTS, tm), 0)
    sel = jnp.zeros((N_EXPERTS, tm), F32)
    e_rows = []
    s_rows = []
    for k in range(TOP_K):
        m = jnp.max(v, axis=0, keepdims=True)
        ik = jnp.min(jnp.where(v == m, eidx, N_EXPERTS), axis=0, keepdims=True)
        hit = eidx == ik
        e_rows.append(ik)
        s_rows.append(jnp.sum(jnp.where(hit, scores, 0.0), axis=0, keepdims=True))
        v = jnp.where(hit, ninf, v)
        sel = jnp.where(hit, 1.0, sel)

    prefix = jnp.dot(sel.astype(BF16), upper_ref[...], preferred_element_type=F32)
    pos = prefix + run_ref[...]
    for k in range(TOP_K):
        rk = jnp.sum(jnp.where(eidx == e_rows[k], pos, 0.0), axis=0, keepdims=True)
        ek_ref[k:k + 1, :] = e_rows[k]
        rk_ref[k:k + 1, :] = rk.astype(I32)
    run_ref[...] = run_ref[...] + jnp.sum(sel, axis=1, keepdims=True)
    cnt_ref[...] = jnp.broadcast_to(run_ref[...], cnt_ref.shape).astype(I32)

    ssum = s_rows[0]
    for k in range(1, TOP_K):
        ssum = ssum + s_rows[k]
    w_rows = [s / (ssum + 1e-20) * ROUTED_SCALE for s in s_rows]
    w_rows.append(jnp.zeros((LANES - TOP_K, tm), F32))
    wt_ref[...] = jnp.concatenate(w_rows, axis=0).T


def _route(x1b, wrt, rbias, tm):
    t, d = x1b.shape
    return pl.pallas_call(
        functools.partial(_route_kernel, tm=tm),
        grid=(t // tm,),
        in_specs=[pl.BlockSpec((tm, d), lambda i: (i, 0)),
                  pl.BlockSpec((N_EXPERTS, d), lambda i: (0, 0)),
                  pl.BlockSpec((N_EXPERTS, 1), lambda i: (0, 0))],
        out_specs=[pl.BlockSpec((TOP_K, tm), lambda i: (0, i)),
                   pl.BlockSpec((TOP_K, tm), lambda i: (0, i)),
                   pl.BlockSpec((tm, LANES), lambda i: (i, 0)),
                   pl.BlockSpec((N_EXPERTS, LANES), lambda i: (0, 0))],
        out_shape=[jax.ShapeDtypeStruct((TOP_K, t), I32), jax.ShapeDtypeStruct((TOP_K, t), I32),
                   jax.ShapeDtypeStruct((t, LANES), F32),
                   jax.ShapeDtypeStruct((N_EXPERTS, LANES), I32)],
        scratch_shapes=[pltpu.VMEM((tm, tm), BF16), pltpu.VMEM((N_EXPERTS, 1), F32)],
        compiler_params=_cparams("arbitrary"),
        name="route",
    )(x1b, wrt, rbias)


def _dest_kernel(ps_ref, ek_ref, rk_ref, d_ref, *, tm):
    e = ek_ref[...]

    def body(j, acc):
        return acc + jnp.where(e == j, ps_ref[j], 0)

    res = lax.fori_loop(0, N_EXPERTS, body, jnp.zeros(e.shape, I32)) + rk_ref[...]
    for a in range(e.shape[1] // tm):
        for k in range(TOP_K):
            d_ref[a:a + 1, k * tm:(k + 1) * tm] = res[k:k + 1, a * tm:(a + 1) * tm]


def _dest(pstart, ek, rk, tm, tw):
    t = ek.shape[1]
    grid_spec = pltpu.PrefetchScalarGridSpec(
        num_scalar_prefetch=1,
        grid=(t // tw,),
        in_specs=[pl.BlockSpec((TOP_K, tw), lambda i, ps: (0, i)),
                  pl.BlockSpec((TOP_K, tw), lambda i, ps: (0, i))],
        out_specs=pl.BlockSpec((tw // tm, TOP_K * tm), lambda i, ps: (i, 0)),
    )
    return pl.pallas_call(
        functools.partial(_dest_kernel, tm=tm),
        grid_spec=grid_spec,
        out_shape=jax.ShapeDtypeStruct((t // tm, TOP_K * tm), I32),
        compiler_params=_cparams("parallel"),
        name="dest",
    )(pstart, ek, rk)


def _layer(x, mem, rel_bias, w_in, b_in, conv_w, diff_lambda, subln_g, mlstm_norm_g, w_mem_kv,
           w_branch, w_out, ln1_g, ln1_b, w_router, router_bias, w_e_gate, w_e_up, w_e_down,
           w_s_gate, w_s_up, w_s_down, ln2_g, ln2_b, layer_idx, cfg):
    b, s, d = x.shape
    t = b * s
    x2 = x.reshape(t, d)

    g0 = (OFF_MLO + 8) * LANES
    w_main = jnp.concatenate([w_in[:, :g0], w_in[:, g0 + 2 * ML_HEADS:]], axis=1).astype(BF16)
    b_main = jnp.concatenate([b_in[:g0], b_in[g0 + 2 * ML_HEADS:]])[None, :]
    w_g = jnp.pad(w_in[:, g0:g0 + 2 * ML_HEADS], ((0, 0), (0, LANES - 2 * ML_HEADS))).astype(BF16)
    b_g = jnp.pad(b_in[g0:g0 + 2 * ML_HEADS], (0, LANES - 2 * ML_HEADS))[None, :]

    u2, gates2 = _proj_in(x2, w_main, b_main, w_g, b_g, cfg["proj_tm"], cfg["proj_tn"])
    u3 = u2.reshape(b, s, N_MAIN)
    gates3 = gates2.reshape(b, s, LANES)

    lam_init = 0.8 - 0.6 * math.exp(-0.3 * layer_idx)
    y_a = _diff_attention(u3, rel_bias, diff_lambda, subln_g[None, :], cfg["attn_tq"], lam_init)
    y_m = _mlstm(u3, gates3, conv_w, mlstm_norm_g[None, :], cfg["ml_chunk"])
    kv = _mm(mem.reshape(-1, d), w_mem_kv.astype(BF16), BF16, cfg["kv_tm"], "mem_kv")
    y_c = _mem_attention(u3, kv.reshape(b, -1, 2 * MA_HEADS * MA_HEAD_DIM), cfg["ma_tq"])

    x1, x1b, x1p = _merge(y_a.reshape(t, d), y_m.reshape(t, d), y_c.reshape(t, d), u2, x2,
                          w_branch.astype(BF16), w_out.astype(BF16), ln1_g[None, :],
                          ln1_b[None, :], cfg["merge_tm"])

    ek, rk, wt, cnt = _route(x1b, w_router.T.astype(BF16), router_bias.astype(F32)[:, None],
                             cfg["route_tm"])
    bm = cfg["expert_bm"]
    items, starts = _work_items(cnt[:, 0], t * TOP_K, bm)
    dest = _dest(starts, ek, rk, cfg["moe_tm"], cfg["dest_tw"]).reshape(-1)
    xs = _sc_scatter_rows(x1p, dest, cfg["moe_tm"])
    sh = _shared_ffn(x1b, w_s_gate.astype(BF16), w_s_up.astype(BF16), w_s_down.astype(BF16),
                     cfg["shared_tm"])
    ys = _experts(items, xs, w_e_gate, w_e_up, w_e_down, bm, after=sh)
    tm = cfg["moe_tm"]
    per = t // tm // cfg["combine_chunks"]
    out = None
    for c in range(cfg["combine_chunks"]):
        rows = dest[c * per * TOP_K * tm:(c + 1) * per * TOP_K * tm]
        yt = _sc_gather_rows(ys, rows).reshape(per, TOP_K, tm, d // 2)
        out = _ffn_out(yt, sh, x1, wt, ln2_g[None, :], ln2_b[None, :], tm, c * per, out)
    return out.reshape(b, s, d)


def _config(b, s):
    t = b * s
    return {
        "proj_tm": min(1024, t), "proj_tn": 1024,
        "attn_tq": 256, "ml_chunk": 256, "kv_tm": 512, "ma_tq": min(512, s),
        "merge_tm": 256, "route_tm": 512, "expert_bm": 512, "moe_tm": 256, "dest_tw": 2048,
        "shared_tm": 512, "combine_chunks": 8,
    }


def kernel(x, mem, rel_bias, w_in, b_in, conv_w, diff_lambda, subln_g, mlstm_norm_g, w_mem_kv,
           w_branch, w_out, ln1_g, ln1_b, w_router, router_bias, w_e_gate, w_e_up, w_e_down,
           w_s_gate, w_s_up, w_s_down, ln2_g, ln2_b):
    cfg = _config(x.shape[0], x.shape[1])
    for l in range(DEPTH):
        x = _layer(x, mem, rel_bias, w_in[l], b_in[l], conv_w[l], diff_lambda[l], subln_g[l],
                   mlstm_norm_g[l], w_mem_kv[l], w_branch[l], w_out[l], ln1_g[l], ln1_b[l],
                   w_router[l], router_bias[l], w_e_gate[l], w_e_up[l], w_e_down[l],
                   w_s_gate[l], w_s_up[l], w_s_down[l], ln2_g[l], ln2_b[l], l, cfg)
    return x
```

```python
import functools
import math

import numpy as np
import jax
import jax.numpy as jnp
from jax import lax
from jax.experimental import pallas as pl
from jax.experimental.pallas import tpu as pltpu
from jax.experimental.pallas import tpu_sc as plsc

F32 = jnp.float32
BF16 = jnp.bfloat16
U32 = jnp.uint32
I32 = jnp.int32

D_MODEL = 1024
DEPTH = 1
DA_HEAD_DIM = 64
DA_V_DIM = 128
DA_HEADS = 8
ML_HEADS = 4
ML_HEAD_DIM = 256
ML_CONV = 4
ML_GROUP = 2
MA_HEADS = 4
MA_HEAD_DIM = 256
N_BRANCH = 3
REL_BUCKETS = 32
REL_MAX_DIST = 128
N_EXPERTS = 256
TOP_K = 8
N_GROUP = 8
TOPK_GROUP = 4
D_EXPERT = 256
ROUTED_SCALE = 2.5
ALPHA = (2.0 * DEPTH) ** 0.25

LANES = 128
NEG = -1e30
LOG2E = math.log2(math.e)
FOLD_ACCS = 4
VMEM_LIMIT = 56 * 1024 * 1024

OFF_DAQ, OFF_DAK, OFF_DAV = 0, 8, 16
OFF_MLQ, OFF_MLK, OFF_MLV, OFF_MLO, OFF_MAQ, OFF_GATE = 24, 32, 40, 48, 56, 64
N_MAIN = 88 * LANES


def _cparams(*sem):
    return pltpu.CompilerParams(dimension_semantics=sem, vmem_limit_bytes=VMEM_LIMIT)


def _sigmoid(x):
    return 0.5 + 0.5 * jnp.tanh(0.5 * x)


def _layer_norm(z, g, b):
    mu = jnp.mean(z, axis=-1, keepdims=True)
    zc = z - mu
    var = jnp.mean(zc * zc, axis=-1, keepdims=True)
    return zc * lax.rsqrt(var + 1e-5) * g + b


def _proj_in_kernel(x_ref, w_ref, b_ref, wg_ref, bg_ref, u_ref, g_ref, xs_ref):
    @pl.when(pl.program_id(1) == 0)
    def _():
        xb = x_ref[...].astype(BF16)
        xs_ref[...] = xb
        g_ref[...] = jnp.dot(xb, wg_ref[...], preferred_element_type=F32) + bg_ref[...]

    acc = jnp.dot(xs_ref[...], w_ref[...], preferred_element_type=F32)
    u_ref[...] = (acc + b_ref[...]).astype(u_ref.dtype)


def _proj_in(x2, w_main, b_main, w_g, b_g, tm, tn):
    t, k = x2.shape
    n = w_main.shape[1]
    return pl.pallas_call(
        _proj_in_kernel,
        grid=(t // tm, n // tn),
        in_specs=[
            pl.BlockSpec((tm, k), lambda i, j: (i, 0)),
            pl.BlockSpec((k, tn), lambda i, j: (0, j)),
            pl.BlockSpec((1, tn), lambda i, j: (0, j)),
            pl.BlockSpec((k, LANES), lambda i, j: (0, 0)),
            pl.BlockSpec((1, LANES), lambda i, j: (0, 0)),
        ],
        out_specs=[
            pl.BlockSpec((tm, tn), lambda i, j: (i, j)),
            pl.BlockSpec((tm, LANES), lambda i, j: (i, 0)),
        ],
        out_shape=[jax.ShapeDtypeStruct((t, n), BF16), jax.ShapeDtypeStruct((t, LANES), F32)],
        scratch_shapes=[pltpu.VMEM((tm, k), BF16)],
        compiler_params=_cparams("parallel", "arbitrary"),
        name="proj_in",
    )(x2, w_main, b_main, w_g, b_g)


def _mm_kernel(x_ref, w_ref, o_ref):
    o_ref[...] = jnp.dot(x_ref[...].astype(BF16), w_ref[...],
                         preferred_element_type=F32).astype(o_ref.dtype)


def _mm(x2, w, out_dtype, tm, name):
    m, k = x2.shape
    n = w.shape[1]
    return pl.pallas_call(
        _mm_kernel,
        grid=(m // tm,),
        in_specs=[pl.BlockSpec((tm, k), lambda i: (i, 0)), pl.BlockSpec((k, n), lambda i: (0, 0))],
        out_specs=pl.BlockSpec((tm, n), lambda i: (i, 0)),
        out_shape=jax.ShapeDtypeStruct((m, n), out_dtype),
        compiler_params=_cparams("parallel"),
        name=name,
    )(x2, w)


def _fold8(x, op):
    n = x.shape[0] // 8
    accs = [x[8 * a:8 * a + 8, :] for a in range(min(FOLD_ACCS, n))]
    for a in range(FOLD_ACCS, n):
        accs[a % FOLD_ACCS] = op(accs[a % FOLD_ACCS], x[8 * a:8 * a + 8, :])
    while len(accs) > 1:
        accs = [op(accs[a], accs[a + 1]) for a in range(0, len(accs), 2)]
    return accs[0]


def _dattn_kernel(rb_ref, q_ref, k_ref, v_ref, bkt_ref, dl_ref, g_ref, o_ref,
                  st_ref, pt_ref, vt_ref, nb_ref, *, tq, seq, lam_init):
    h = pl.program_id(0)
    nq = seq // tq
    nt = (((1,), (1,)), ((), ()))

    @pl.when(pl.program_id(1) == 0)
    def _():
        far = rb_ref[REL_BUCKETS - 1, h]
        for t in range(2):
            bk = bkt_ref[t]
            tile = jnp.full((tq, tq), NEG, F32)
            for bb in range(REL_BUCKETS):
                tile = jnp.where(bk == bb, (rb_ref[bb, h] - far) * LOG2E, tile)
            nb_ref[t * tq:(t + 1) * tq, :] = tile

    for j in range(nq):
        vt_ref[:, j * tq:(j + 1) * tq] = v_ref[0, j * tq:(j + 1) * tq, :].astype(F32).T.astype(BF16)

    dl = dl_ref[...]
    lam = (jnp.exp(jnp.sum(dl[0:1] * dl[1:2], axis=1, keepdims=True))
           - jnp.exp(jnp.sum(dl[2:3] * dl[3:4], axis=1, keepdims=True)) + lam_init)
    lane = lax.broadcasted_iota(I32, (tq, LANES), 1)
    gain = g_ref[...] * (1.0 - lam_init)

    def logits_pass(i):
        qs = q_ref[0, i * tq:(i + 1) * tq, :] * jnp.asarray(DA_HEAD_DIM ** -0.5 * LOG2E, BF16)
        zero = jnp.zeros_like(qs)
        qm = (jnp.where(lane < DA_HEAD_DIM, qs, zero), jnp.where(lane >= DA_HEAD_DIM, qs, zero))
        mx = [None, None]

        def chunk(c):
            for half in range(2):
                s = lax.dot_general(k_ref[0, c * tq:(c + 1) * tq, :], qm[half], nt,
                                    preferred_element_type=F32)
                if c >= i - 1:
                    s = s + nb_ref[(c - i + 1) * tq:(c - i + 2) * tq, :]
                st_ref[2 * (i % 2) + half, c * tq:(c + 1) * tq, :] = s
                f = _fold8(s, jnp.maximum)
                mx[half] = f if mx[half] is None else jnp.maximum(mx[half], f)

        def result():
            return [jnp.max(mx[half], axis=0, keepdims=True) for half in range(2)]

        return [functools.partial(chunk, c) for c in range(i + 1)], result

    def exp_pass(i, m):
        l8 = [None, None]

        def chunk(c):
            for half in range(2):
                p = jnp.exp2(st_ref[2 * (i % 2) + half, c * tq:(c + 1) * tq, :] - m[half])
                pt_ref[2 * (i % 2) + half, c * tq:(c + 1) * tq, :] = p.astype(BF16)
                f = _fold8(p, jnp.add)
                l8[half] = f if l8[half] is None else l8[half] + f

        def result():
            return [jnp.sum(l8[half], axis=0, keepdims=True) for half in range(2)]

        return [functools.partial(chunk, c) for c in range(i + 1)], result

    def value_pass(i, l1, l2):
        kv = (i + 1) * tq
        acc = [jnp.dot(vt_ref[:, 0:kv], pt_ref[2 * (i % 2) + half, 0:kv, :],
                       preferred_element_type=F32) for half in range(2)]
        o = (acc[0] * (1.0 / l1) - acc[1] * (lam / l2)).T
        ms = jnp.mean(o * o, axis=1, keepdims=True)
        y = o * lax.rsqrt(ms + 1e-5) * gain
        o_ref[0, i * tq:(i + 1) * tq, :] = y.astype(o_ref.dtype)

    chunks, result = logits_pass(0)
    for run in chunks:
        run()
    m_next = result()
    for i in range(nq):
        exp_chunks, exp_result = exp_pass(i, m_next)
        next_chunks, next_result = logits_pass(i + 1) if i + 1 < nq else ([], None)
        for c in range(max(len(exp_chunks), len(next_chunks))):
            if c < len(next_chunks):
                next_chunks[c]()
            if c < len(exp_chunks):
                exp_chunks[c]()
        if next_result is not None:
            m_next = next_result()
        value_pass(i, *exp_result())


def _t5_bucket_np(dist):
    n = np.maximum(dist, 0)
    max_exact = REL_BUCKETS // 2
    large = max_exact + (np.log(np.maximum(n, 1).astype(np.float32) / max_exact)
                         / math.log(REL_MAX_DIST / max_exact)
                         * (REL_BUCKETS - max_exact)).astype(np.int32)
    large = np.minimum(large, REL_BUCKETS - 1)
    return np.where(n < max_exact, n, large)


def _near_buckets(tq):
    assert _t5_bucket_np(np.array([tq + 1]))[0] == REL_BUCKETS - 1
    c = np.arange(tq)[:, None]
    r = np.arange(tq)[None, :]
    left = _t5_bucket_np(r + tq - c)
    diag = np.where(r >= c, _t5_bucket_np(r - c), -1)
    return np.stack([left, diag], axis=0).astype(np.int32)


def _diff_attention(u3, rel_bias, diff_lambda, subln_g, tq, lam_init):
    b, s, _ = u3.shape
    kern = functools.partial(_dattn_kernel, tq=tq, seq=s, lam_init=lam_init)
    grid_spec = pltpu.PrefetchScalarGridSpec(
        num_scalar_prefetch=1,
        grid=(DA_HEADS, b),
        in_specs=[
            pl.BlockSpec((1, s, LANES), lambda h, bi, rb: (bi, 0, OFF_DAQ + h)),
            pl.BlockSpec((1, s, LANES), lambda h, bi, rb: (bi, 0, OFF_DAK + h)),
            pl.BlockSpec((1, s, LANES), lambda h, bi, rb: (bi, 0, OFF_DAV + h)),
            pl.BlockSpec((2, tq, tq), lambda h, bi, rb: (0, 0, 0)),
            pl.BlockSpec((4, DA_HEAD_DIM), lambda h, bi, rb: (0, 0)),
            pl.BlockSpec((1, DA_V_DIM), lambda h, bi, rb: (0, 0)),
        ],
        out_specs=pl.BlockSpec((1, s, LANES), lambda h, bi, rb: (bi, 0, h)),
        scratch_shapes=[pltpu.VMEM((4, s, tq), F32), pltpu.VMEM((4, s, tq), BF16),
                        pltpu.VMEM((DA_V_DIM, s), BF16),
                        pltpu.VMEM((2 * tq, tq), F32)],
    )
    return pl.pallas_call(
        kern,
        grid_spec=grid_spec,
        out_shape=jax.ShapeDtypeStruct((b, s, DA_HEADS * DA_V_DIM), BF16),
        compiler_params=_cparams("arbitrary", "arbitrary"),
        name="diff_attn",
    )(rel_bias.astype(F32), u3, u3, u3, jnp.asarray(_near_buckets(tq)), diff_lambda, subln_g)


def _mlstm_kernel(q_ref, k_ref, v_ref, og_ref, g_ref, cwq_ref, cwk_ref, ng_ref, y_ref,
                  ct_ref, gts_ref, *, chunk, seq):
    first_head = pl.program_id(1) * ML_GROUP
    nc = seq // chunk
    hd = ML_HEAD_DIM
    lane = lax.broadcasted_iota(jnp.int32, (chunk, LANES), 1)
    row = lax.broadcasted_iota(jnp.int32, (chunk, chunk), 0)
    col = lax.broadcasted_iota(jnp.int32, (chunk, chunk), 1)
    causal = row >= col
    tri = causal.astype(F32)
    ct_ref[...] = jnp.zeros(ct_ref.shape, F32)

    def conv_silu(ref, cw_ref, r0, c, cols):
        cw = cw_ref[:, cols]
        cur = ref[0, pl.ds(r0, chunk), cols].astype(F32)
        p0 = pl.multiple_of(jnp.maximum(r0 - 16, 0), 16)
        prev = ref[0, pl.ds(p0, 16), cols].astype(F32)
        prev = jnp.where(c > 0, prev, 0.0)
        x = jnp.concatenate([prev, cur], axis=0)
        out = cw[ML_CONV - 1:ML_CONV] * cur
        for tap in range(ML_CONV - 1):
            shifted = pltpu.roll(x, ML_CONV - 1 - tap, 0)[16:]
            out = out + cw[tap:tap + 1] * shifted
        return out * _sigmoid(out)

    def head_step(j, c, r0, g, bc, m, n):
        h = first_head + j
        cols = slice(j * hd, (j + 1) * hd)
        irow = gts_ref[pl.ds(h, 1), :]
        brow = gts_ref[pl.ds(ML_HEADS + h, 1), :]
        bcol = jnp.sum(jnp.where(lane == ML_HEADS + h, bc, 0.0), axis=1, keepdims=True)
        icol = jnp.sum(jnp.where(lane == h, g, 0.0), axis=1, keepdims=True)

        dmat = jnp.where(causal, bcol - brow + irow, NEG)
        inter = bcol + m
        m_row = jnp.maximum(inter, jnp.max(dmat, axis=1, keepdims=True))

        q = conv_silu(q_ref, cwq_ref, r0, c, cols)
        k = conv_silu(k_ref, cwk_ref, r0, c, cols) * (hd ** -0.5)
        qb = q.astype(BF16)
        kb = k.astype(BF16)
        vb = v_ref[0, pl.ds(r0, chunk), cols]
        sqk = lax.dot_general(qb, kb, (((1,), (1,)), ((), ())), preferred_element_type=F32)
        w = jnp.exp(dmat - m_row) * sqk
        inter_w = jnp.exp(inter - m_row)
        ct = ct_ref[j]
        num = (inter_w * jnp.dot(qb, ct.astype(BF16), preferred_element_type=F32)
               + jnp.dot(w.astype(BF16), vb, preferred_element_type=F32))
        den = inter_w * jnp.sum(q * n, axis=1, keepdims=True) + jnp.sum(w, axis=1, keepdims=True)
        hout = num / jnp.maximum(jnp.abs(den), jnp.exp(-m_row))
        mu = jnp.mean(hout, axis=1, keepdims=True)
        hc = hout - mu
        var = jnp.mean(hc * hc, axis=1, keepdims=True)
        hn = hc * lax.rsqrt(var + 1e-5) * ng_ref[:, cols]
        og = og_ref[0, pl.ds(r0, chunk), cols].astype(F32)
        y_ref[0, pl.ds(r0, chunk), cols] = (_sigmoid(og) * hn).astype(y_ref.dtype)

        total = brow[:, chunk - 1:chunk]
        grow = total - brow + irow
        m_new = jnp.maximum(total + m, jnp.max(grow, axis=1, keepdims=True))
        decay = jnp.exp(total + m - m_new)
        ws = jnp.exp(total - bcol + icol - m_new)
        wsv = (ws * vb.astype(F32)).astype(BF16)
        ct_ref[j] = decay * ct + jnp.dot(k.T.astype(BF16), wsv, preferred_element_type=F32)
        n_new = decay * n + jnp.sum(ws * k, axis=0, keepdims=True)
        return m_new, n_new

    def body(c, carry):
        r0 = pl.multiple_of(c * chunk, chunk)
        g = g_ref[0, pl.ds(r0, chunk), :]
        logf = jnp.minimum(g, 0.0) - jnp.log(1.0 + jnp.exp(-jnp.abs(g)))
        bc = jnp.dot(tri, logf, precision=lax.Precision.HIGHEST, preferred_element_type=F32)
        gts_ref[...] = jnp.where(lane < ML_HEADS, g, bc).T
        return tuple(head_step(j, c, r0, g, bc, *carry[j]) for j in range(ML_GROUP))

    init = tuple((jnp.zeros((1, 1), F32), jnp.zeros((1, hd), F32)) for _ in range(ML_GROUP))
    lax.fori_loop(0, nc, body, init)


def _mlstm(u3, gates3, conv_w, norm_g, chunk):
    b, s, _ = u3.shape
    gw = ML_GROUP * ML_HEAD_DIM
    nq = gw // LANES
    ngroups = ML_HEADS // ML_GROUP
    kern = functools.partial(_mlstm_kernel, chunk=chunk, seq=s)

    def ublock(off):
        return pl.BlockSpec((1, s, gw), lambda bi, h: (bi, 0, off // nq + h))

    return pl.pallas_call(
        kern,
        grid=(b, ngroups),
        in_specs=[
            ublock(OFF_MLQ), ublock(OFF_MLK), ublock(OFF_MLV), ublock(OFF_MLO),
            pl.BlockSpec((1, s, LANES), lambda bi, h: (bi, 0, 0)),
            pl.BlockSpec((ML_CONV, gw), lambda bi, h: (0, h)),
            pl.BlockSpec((ML_CONV, gw), lambda bi, h: (0, ngroups + h)),
            pl.BlockSpec((1, gw), lambda bi, h: (0, h)),
        ],
        out_specs=pl.BlockSpec((1, s, gw), lambda bi, h: (bi, 0, h)),
        out_shape=jax.ShapeDtypeStruct((b, s, ML_HEADS * ML_HEAD_DIM), BF16),
        scratch_shapes=[pltpu.VMEM((ML_GROUP, ML_HEAD_DIM, ML_HEAD_DIM), F32),
                        pltpu.VMEM((LANES, chunk), F32)],
        compiler_params=_cparams("parallel", "arbitrary"),
        name="mlstm",
    )(u3, u3, u3, u3, gates3, conv_w, conv_w, norm_g)


def _memattn_kernel(q_ref, k_ref, v_ref, o_ref, *, tq, seq):
    kb = k_ref[0]
    vb = v_ref[0]
    scale = jnp.asarray(MA_HEAD_DIM ** -0.5, BF16)
    for t in range(seq // tq):
        q = q_ref[0, t * tq:(t + 1) * tq, :] * scale
        s = lax.dot_general(q, kb, (((1,), (1,)), ((), ())), preferred_element_type=F32)
        p = jnp.exp(s - jnp.max(s, axis=1, keepdims=True))
        inv = 1.0 / jnp.sum(p, axis=1, keepdims=True)
        o = jnp.dot(p.astype(BF16), vb, preferred_element_type=F32) * inv
        o_ref[0, t * tq:(t + 1) * tq, :] = o.astype(o_ref.dtype)


def _mem_attention(u3, kv3, tq):
    b, s, _ = u3.shape
    mlen = kv3.shape[1]
    hd = MA_HEAD_DIM
    nq = hd // LANES
    kern = functools.partial(_memattn_kernel, tq=tq, seq=s)
    return pl.pallas_call(
        kern,
        grid=(b, MA_HEADS),
        in_specs=[
            pl.BlockSpec((1, s, hd), lambda bi, h: (bi, 0, OFF_MAQ // nq + h)),
            pl.BlockSpec((1, mlen, hd), lambda bi, h: (bi, 0, h)),
            pl.BlockSpec((1, mlen, hd), lambda bi, h: (bi, 0, MA_HEADS + h)),
        ],
        out_specs=pl.BlockSpec((1, s, hd), lambda bi, h: (bi, 0, h)),
        out_shape=jax.ShapeDtypeStruct((b, s, MA_HEADS * hd), BF16),
        compiler_params=_cparams("parallel", "parallel"),
        name="mem_attn",
    )(u3, kv3, kv3)


def _merge_kernel(ya_ref, ym_ref, yc_ref, g0_ref, g1_ref, g2_ref, x_ref, wb_ref, wo_ref,
                  lg_ref, lb_ref, x1_ref, x1b_ref, x1p_ref):
    acc = None
    for n, (y_ref, g_ref) in enumerate(((ya_ref, g0_ref), (ym_ref, g1_ref), (yc_ref, g2_ref))):
        pr = jnp.dot(y_ref[...], wb_ref[n], preferred_element_type=F32)
        t = jax.nn.sigmoid(g_ref[...].astype(F32)) * pr
        acc = t if acc is None else acc + t
    out = jnp.dot(acc.astype(BF16), wo_ref[...], preferred_element_type=F32)
    x1 = _layer_norm(ALPHA * x_ref[...] + out, lg_ref[...], lb_ref[...])
    x1_ref[...] = x1
    x1b = x1.astype(BF16)
    x1b_ref[...] = x1b
    x1p_ref[...] = _pack_halves(x1b)


def _pack_halves(vb):
    w = vb.shape[1] // 2
    hi = lax.bitcast_convert_type(vb[:, :w].astype(F32), U32)
    lo = lax.bitcast_convert_type(vb[:, w:].astype(F32), U32)
    return hi | (lo >> 16)


def _unpack_halves(u):
    hi = lax.bitcast_convert_type(u & jnp.uint32(0xFFFF0000), F32)
    lo = lax.bitcast_convert_type(u << 16, F32)
    return hi, lo


def _merge(ya, ym, yc, u2, x2, wb, wo, lg, lb, tm):
    t, d = x2.shape
    gb = OFF_GATE * LANES // d

    def rows(i):
        return (i, 0)

    return pl.pallas_call(
        _merge_kernel,
        grid=(t // tm,),
        in_specs=[
            pl.BlockSpec((tm, d), rows), pl.BlockSpec((tm, d), rows), pl.BlockSpec((tm, d), rows),
            pl.BlockSpec((tm, d), lambda i: (i, gb)),
            pl.BlockSpec((tm, d), lambda i: (i, gb + 1)),
            pl.BlockSpec((tm, d), lambda i: (i, gb + 2)),
            pl.BlockSpec((tm, d), rows),
            pl.BlockSpec((N_BRANCH, d, d), lambda i: (0, 0, 0)),
            pl.BlockSpec((d, d), lambda i: (0, 0)),
            pl.BlockSpec((1, d), lambda i: (0, 0)),
            pl.BlockSpec((1, d), lambda i: (0, 0)),
        ],
        out_specs=[pl.BlockSpec((tm, d), rows), pl.BlockSpec((tm, d), rows),
                   pl.BlockSpec((tm, d // 2), rows)],
        out_shape=[jax.ShapeDtypeStruct((t, d), F32), jax.ShapeDtypeStruct((t, d), BF16),
                   jax.ShapeDtypeStruct((t, d // 2), U32)],
        compiler_params=_cparams("parallel"),
        name="merge_ln1",
    )(ya, ym, yc, u2, u2, u2, x2, wb, wo, lg, lb)


def _expert_kernel(ib_ref, ie_ref, lo_ref, hi_ref, first_ref, nit_ref, new_ref, slot_ref, next_ref,
                   x_ref, wg_hbm, wu_hbm, wd_hbm, after_hbm, o_ref, wgf_ref, wuf_ref, wdf_ref,
                   wgb_ref, wub_ref, wdb_ref, blk_ref, wsem):
    i = pl.program_id(0)

    def weight_copies(e, s):
        return (pltpu.make_async_copy(wg_hbm.at[e], wgf_ref.at[s], wsem.at[s]),
                pltpu.make_async_copy(wu_hbm.at[e], wuf_ref.at[s], wsem.at[s]),
                pltpu.make_async_copy(wd_hbm.at[e], wdf_ref.at[s], wsem.at[s]))

    @pl.when(i == 0)
    def _():
        blk_ref[...] = jnp.zeros(blk_ref.shape, blk_ref.dtype)
        for cp in weight_copies(ie_ref[0], 0):
            cp.start()

    @pl.when(new_ref[i] == 1)
    def _():
        s = slot_ref[i]
        for cp in weight_copies(ie_ref[i], s):
            cp.wait()

        @pl.when(next_ref[i] >= 0)
        def _():
            for cp in weight_copies(next_ref[i], 1 - s):
                cp.start()

        wgb_ref[...] = wgf_ref[s].astype(BF16)
        wub_ref[...] = wuf_ref[s].astype(BF16)
        wdb_ref[...] = wdf_ref[s].astype(BF16)

    @pl.when(i < nit_ref[0])
    def _():
        xl, xr = _unpack_halves(x_ref[...])
        xl = xl.astype(BF16)
        xr = xr.astype(BF16)
        half = xl.shape[1]

        def up(w_ref):
            return (jnp.dot(xl, w_ref[:half, :], preferred_element_type=F32)
                    + jnp.dot(xr, w_ref[half:, :], preferred_element_type=F32))

        hg = up(wgb_ref)
        hu = up(wub_ref)
        act = (hg * jax.nn.sigmoid(hg) * hu).astype(BF16)
        y = _pack_halves(jnp.dot(act, wdb_ref[...], preferred_element_type=F32).astype(BF16))
        row = lax.broadcasted_iota(I32, y.shape, 0)
        mine = (row >= lo_ref[i]) & (row < hi_ref[i])

        keep = jnp.where(first_ref[i] == 1, jnp.uint32(0), blk_ref[...])
        out = jnp.where(mine, y, keep)
        blk_ref[...] = out
        o_ref[...] = out


def _work_items(counts, n_pairs, bm):
    assert n_pairs % bm == 0
    nblocks = n_pairs // bm
    ends = jnp.cumsum(counts)
    starts = ends - counts
    first_blk = starts // bm
    n_e = jnp.where(counts > 0, (ends - 1) // bm - first_blk + 1, 0)
    item_end = jnp.cumsum(n_e)
    item_start = item_end - n_e
    n_items = item_end[-1]
    i = jnp.arange(nblocks + N_EXPERTS)
    valid = i < n_items
    e = jnp.minimum(jnp.sum(item_end[None, :] <= jnp.minimum(i, n_items - 1)[:, None], axis=1),
                    N_EXPERTS - 1)
    blk = jnp.where(valid, first_blk[e] + i - item_start[e], nblocks - 1)
    lo = jnp.clip(starts[e] - blk * bm, 0, bm)
    hi = jnp.where(valid, jnp.clip(ends[e] - blk * bm, 0, bm), 0)
    first = jnp.concatenate([jnp.ones((1,), I32), (blk[1:] != blk[:-1]).astype(I32)])
    new = jnp.where(valid, jnp.concatenate([jnp.ones((1,), bool), e[1:] != e[:-1]]), False)
    slot = (jnp.cumsum(new) - 1) % 2
    ids = jnp.arange(N_EXPERTS)
    later = jnp.where((counts > 0)[None, :] & (ids[None, :] > ids[:, None]), ids[None, :], N_EXPERTS)
    next_e = jnp.min(later, axis=1)
    nxt = jnp.where(next_e[e] < N_EXPERTS, next_e[e], -1)
    items = tuple(a.astype(I32) for a in (blk, e, lo, hi, first, n_items[None], new, slot, nxt))
    return items, starts.astype(I32)


def _experts(items, xs, w_gate, w_up, w_down, bm, after):
    n, dh = xs.shape
    d = 2 * dh
    de = w_gate.shape[2]

    def rows(i, ib, *_):
        return (ib[i], 0)

    grid_spec = pltpu.PrefetchScalarGridSpec(
        num_scalar_prefetch=len(items),
        grid=(items[0].shape[0],),
        in_specs=[
            pl.BlockSpec((bm, dh), rows),
            pl.BlockSpec(memory_space=pl.ANY),
            pl.BlockSpec(memory_space=pl.ANY),
            pl.BlockSpec(memory_space=pl.ANY),
            pl.BlockSpec(memory_space=pl.ANY),
        ],
        out_specs=pl.BlockSpec((bm, dh), rows),
        scratch_shapes=[pltpu.VMEM((2, d, de), F32), pltpu.VMEM((2, d, de), F32),
                        pltpu.VMEM((2, de, d), F32),
                        pltpu.VMEM((d, de), BF16), pltpu.VMEM((d, de), BF16),
                        pltpu.VMEM((de, d), BF16), pltpu.VMEM((bm, dh), U32),
                        pltpu.SemaphoreType.DMA((2,))],
    )
    return pl.pallas_call(
        _expert_kernel,
        grid_spec=grid_spec,
        out_shape=jax.ShapeDtypeStruct((n, dh), U32),
        compiler_params=_cparams("arbitrary"),
        name="experts",
    )(*items, xs, w_gate, w_up, w_down, after)


SC_WINDOW = 128
SC_WORKERS = 32


def _sc_worker():
    return lax.axis_index("core") * (SC_WORKERS // 2) + lax.axis_index("subcore")


def _sc_scatter_rows(x, idx, tm):
    t, dh = x.shape
    n = idx.shape[0]
    assert n == t * TOP_K and tm % SC_WINDOW == 0
    windows = t // SC_WINDOW
    per = windows // SC_WORKERS
    assert windows % SC_WORKERS == 0
    wpt = tm // SC_WINDOW
    mesh = plsc.VectorSubcoreMesh(core_axis_name="core", subcore_axis_name="subcore")

    @pl.kernel(out_type=jax.ShapeDtypeStruct((n, dh), x.dtype), mesh=mesh,
               scratch_types=[pltpu.VMEM((SC_WINDOW,), I32), pltpu.VMEM((SC_WINDOW, dh), x.dtype)])
    def scatter(x_hbm, idx_hbm, out_hbm, idx_vmem, rows_vmem):
        worker = _sc_worker()

        @pl.loop(0, per)
        def _(j):
            w = worker * per + j
            tile = w // wpt
            off = (w - tile * wpt) * SC_WINDOW
            pltpu.sync_copy(x_hbm.at[pl.ds(w * SC_WINDOW, SC_WINDOW)], rows_vmem)

            @pl.loop(0, TOP_K)
            def _(k):
                base = (tile * TOP_K + k) * tm + off
                pltpu.sync_copy(idx_hbm.at[pl.ds(base, SC_WINDOW)], idx_vmem)
                pltpu.sync_copy(rows_vmem, out_hbm.at[idx_vmem])

    return scatter(x, idx)


def _sc_gather_rows(src, idx):
    n = idx.shape[0]
    dh = src.shape[1]
    per = n // SC_WORKERS
    assert per % SC_WINDOW == 0
    mesh = plsc.VectorSubcoreMesh(core_axis_name="core", subcore_axis_name="subcore")

    @pl.kernel(out_type=jax.ShapeDtypeStruct((n, dh), src.dtype), mesh=mesh,
               scratch_types=[pltpu.VMEM((SC_WINDOW,), I32), pltpu.VMEM((SC_WINDOW, dh), src.dtype)])
    def gather(src_hbm, idx_hbm, out_hbm, idx_vmem, rows_vmem):
        worker = _sc_worker()

        @pl.loop(0, per // SC_WINDOW)
        def _(j):
            base = worker * per + j * SC_WINDOW
            pltpu.sync_copy(idx_hbm.at[pl.ds(base, SC_WINDOW)], idx_vmem)
            pltpu.sync_copy(src_hbm.at[idx_vmem], rows_vmem)
            pltpu.sync_copy(rows_vmem, out_hbm.at[pl.ds(base, SC_WINDOW)])

    return gather(src, idx)


def _shared_ffn_kernel(xb_ref, wg_ref, wu_ref, wd_ref, o_ref):
    xb = xb_ref[...]
    hg = jnp.dot(xb, wg_ref[...], preferred_element_type=F32)
    hu = jnp.dot(xb, wu_ref[...], preferred_element_type=F32)
    act = (hg * jax.nn.sigmoid(hg) * hu).astype(BF16)
    o_ref[...] = jnp.dot(act, wd_ref[...], preferred_element_type=F32)


def _shared_ffn(x1b, wg, wu, wd, tm):
    t, d = x1b.shape
    ds = wg.shape[1]
    return pl.pallas_call(
        _shared_ffn_kernel,
        grid=(t // tm,),
        in_specs=[pl.BlockSpec((tm, d), lambda i: (i, 0)), pl.BlockSpec((d, ds), lambda i: (0, 0)),
                  pl.BlockSpec((d, ds), lambda i: (0, 0)), pl.BlockSpec((ds, d), lambda i: (0, 0))],
        out_specs=pl.BlockSpec((tm, d), lambda i: (i, 0)),
        out_shape=jax.ShapeDtypeStruct((t, d), F32),
        compiler_params=_cparams("parallel"),
        name="shared_ffn",
    )(x1b, wg, wu, wd)


def _ffn_out_kernel(yt_ref, sh_ref, x1_ref, wt_ref, lg_ref, lb_ref, *rest):
    o_ref = rest[-1]
    wt = wt_ref[...]
    rl = None
    rr = None
    for k in range(TOP_K):
        hi, lo = _unpack_halves(yt_ref[0, k])
        wk = wt[:, k:k + 1]
        rl = wk * hi if rl is None else rl + wk * hi
        rr = wk * lo if rr is None else rr + wk * lo
    z = ALPHA * x1_ref[...] + sh_ref[...] + jnp.concatenate([rl, rr], axis=1)
    o_ref[...] = _layer_norm(z, lg_ref[...], lb_ref[...])


def _ffn_out(yt, sh, x1, wt, lg, lb, tm, tile0, prev):
    t, d = x1.shape
    dh = yt.shape[-1]
    assert yt.shape[1:] == (TOP_K, tm, dh)

    def rows(i):
        return (i + tile0, 0)

    def whole(i):
        return (0, 0)

    in_specs = [
        pl.BlockSpec((1, TOP_K, tm, dh), lambda i: (i, 0, 0, 0)),
        pl.BlockSpec((tm, d), rows), pl.BlockSpec((tm, d), rows), pl.BlockSpec((tm, LANES), rows),
        pl.BlockSpec((1, d), whole), pl.BlockSpec((1, d), whole),
    ]
    args = [yt, sh, x1, wt, lg, lb]
    aliases = {}
    if prev is not None:
        in_specs.append(pl.BlockSpec(memory_space=pl.ANY))
        args.append(prev)
        aliases = {len(args) - 1: 0}
    return pl.pallas_call(
        _ffn_out_kernel,
        grid=(yt.shape[0],),
        in_specs=in_specs,
        out_specs=pl.BlockSpec((tm, d), rows),
        out_shape=jax.ShapeDtypeStruct((t, d), F32),
        input_output_aliases=aliases,
        compiler_params=_cparams("parallel"),
        name="ffn_out_ln2",
    )(*args)


def _route_kernel(xb_ref, wrt_ref, rb_ref, ek_ref, rk_ref, wt_ref, cnt_ref, upper_ref, run_ref, *,
                  tm):
    i = pl.program_id(0)
    gsz = N_EXPERTS // N_GROUP
    ninf = -jnp.inf

    @pl.when(i == 0)
    def _():
        r = lax.broadcasted_iota(I32, (tm, tm), 0)
        c = lax.broadcasted_iota(I32, (tm, tm), 1)
        upper_ref[...] = jnp.where(r < c, 1.0, 0.0).astype(BF16)
        run_ref[...] = jnp.zeros(run_ref.shape, F32)

    logits = lax.dot_general(wrt_ref[...], xb_ref[...], (((1,), (1,)), ((), ())),
                             preferred_element_type=F32)
    scores = jax.nn.sigmoid(logits)
    choice = scores + rb_ref[...]

    ridx = lax.broadcasted_iota(I32, (gsz, tm), 0)
    gscore = []
    for g in range(N_GROUP):
        blk = choice[g * gsz:(g + 1) * gsz, :]
        m1 = jnp.max(blk, axis=0, keepdims=True)
        i1 = jnp.min(jnp.where(blk == m1, ridx, gsz), axis=0, keepdims=True)
        m2 = jnp.max(jnp.where(ridx == i1, ninf, blk), axis=0, keepdims=True)
        gscore.append(m1 + m2)
    masked = []
    for g in range(N_GROUP):
        beaten = jnp.zeros((1, tm), I32)
        for g2 in range(N_GROUP):
            if g2 == g:
                continue
            wins = (gscore[g2] >= gscore[g]) if g2 < g else (gscore[g2] > gscore[g])
            beaten = beaten + jnp.where(wins, 1, 0)
        masked.append(jnp.where(beaten < TOPK_GROUP, choice[g * gsz:(g + 1) * gsz, :], ninf))
    v = jnp.concatenate(masked, axis=0)

    eidx = lax.broadcasted_iota(I32, (N_EXPERTS, tm), 0)
    sel = jnp.zeros((N_EXPERTS, tm), F32)
    e_rows = []
    s_rows = []
    for k in range(TOP_K):
        m = jnp.max(v, axis=0, keepdims=True)
        ik = jnp.min(jnp.where(v == m, eidx, N_EXPERTS), axis=0, keepdims=True)
        hit = eidx == ik
        e_rows.append(ik)
        s_rows.append(jnp.sum(jnp.where(hit, scores, 0.0), axis=0, keepdims=True))
        v = jnp.where(hit, ninf, v)
        sel = jnp.where(hit, 1.0, sel)

    prefix = jnp.dot(sel.astype(BF16), upper_ref[...], preferred_element_type=F32)
    pos = prefix + run_ref[...]
    for k in range(TOP_K):
        rk = jnp.sum(jnp.where(eidx == e_rows[k], pos, 0.0), axis=0, keepdims=True)
        ek_ref[k:k + 1, :] = e_rows[k]
        rk_ref[k:k + 1, :] = rk.astype(I32)
    run_ref[...] = run_ref[...] + jnp.sum(sel, axis=1, keepdims=True)
    cnt_ref[...] = jnp.broadcast_to(run_ref[...], cnt_ref.shape).astype(I32)

    ssum = s_rows[0]
    for k in range(1, TOP_K):
        ssum = ssum + s_rows[k]
    w_rows = [s / (ssum + 1e-20) * ROUTED_SCALE for s in s_rows]
    w_rows.append(jnp.zeros((LANES - TOP_K, tm), F32))
    wt_ref[...] = jnp.concatenate(w_rows, axis=0).T


def _route(x1b, wrt, rbias, tm):
    t, d = x1b.shape
    return pl.pallas_call(
        functools.partial(_route_kernel, tm=tm),
        grid=(t // tm,),
        in_specs=[pl.BlockSpec((tm, d), lambda i: (i, 0)),
                  pl.BlockSpec((N_EXPERTS, d), lambda i: (0, 0)),
                  pl.BlockSpec((N_EXPERTS, 1), lambda i: (0, 0))],
        out_specs=[pl.BlockSpec((TOP_K, tm), lambda i: (0, i)),
                   pl.BlockSpec((TOP_K, tm), lambda i: (0, i)),
                   pl.BlockSpec((tm, LANES), lambda i: (i, 0)),
                   pl.BlockSpec((N_EXPERTS, LANES), lambda i: (0, 0))],
        out_shape=[jax.ShapeDtypeStruct((TOP_K, t), I32), jax.ShapeDtypeStruct((TOP_K, t), I32),
                   jax.ShapeDtypeStruct((t, LANES), F32),
                   jax.ShapeDtypeStruct((N_EXPERTS, LANES), I32)],
        scratch_shapes=[pltpu.VMEM((tm, tm), BF16), pltpu.VMEM((N_EXPERTS, 1), F32)],
        compiler_params=_cparams("arbitrary"),
        name="route",
    )(x1b, wrt, rbias)


def _dest_kernel(ps_ref, ek_ref, rk_ref, d_ref, *, tm):
    e = ek_ref[...]

    def body(j, acc):
        return acc + jnp.where(e == j, ps_ref[j], 0)

    res = lax.fori_loop(0, N_EXPERTS, body, jnp.zeros(e.shape, I32)) + rk_ref[...]
    for a in range(e.shape[1] // tm):
        for k in range(TOP_K):
            d_ref[a:a + 1, k * tm:(k + 1) * tm] = res[k:k + 1, a * tm:(a + 1) * tm]


def _dest(pstart, ek, rk, tm, tw):
    t = ek.shape[1]
    grid_spec = pltpu.PrefetchScalarGridSpec(
        num_scalar_prefetch=1,
        grid=(t // tw,),
        in_specs=[pl.BlockSpec((TOP_K, tw), lambda i, ps: (0, i)),
                  pl.BlockSpec((TOP_K, tw), lambda i, ps: (0, i))],
        out_specs=pl.BlockSpec((tw // tm, TOP_K * tm), lambda i, ps: (i, 0)),
    )
    return pl.pallas_call(
        functools.partial(_dest_kernel, tm=tm),
        grid_spec=grid_spec,
        out_shape=jax.ShapeDtypeStruct((t // tm, TOP_K * tm), I32),
        compiler_params=_cparams("parallel"),
        name="dest",
    )(pstart, ek, rk)


def _layer(x, mem, rel_bias, w_in, b_in, conv_w, diff_lambda, subln_g, mlstm_norm_g, w_mem_kv,
           w_branch, w_out, ln1_g, ln1_b, w_router, router_bias, w_e_gate, w_e_up, w_e_down,
           w_s_gate, w_s_up, w_s_down, ln2_g, ln2_b, layer_idx, cfg):
    b, s, d = x.shape
    t = b * s
    x2 = x.reshape(t, d)

    g0 = (OFF_MLO + 8) * LANES
    w_main = jnp.concatenate([w_in[:, :g0], w_in[:, g0 + 2 * ML_HEADS:]], axis=1).astype(BF16)
    b_main = jnp.concatenate([b_in[:g0], b_in[g0 + 2 * ML_HEADS:]])[None, :]
    w_g = jnp.pad(w_in[:, g0:g0 + 2 * ML_HEADS], ((0, 0), (0, LANES - 2 * ML_HEADS))).astype(BF16)
    b_g = jnp.pad(b_in[g0:g0 + 2 * ML_HEADS], (0, LANES - 2 * ML_HEADS))[None, :]

    u2, gates2 = _proj_in(x2, w_main, b_main, w_g, b_g, cfg["proj_tm"], cfg["proj_tn"])
    u3 = u2.reshape(b, s, N_MAIN)
    gates3 = gates2.reshape(b, s, LANES)

    lam_init = 0.8 - 0.6 * math.exp(-0.3 * layer_idx)
    y_a = _diff_attention(u3, rel_bias, diff_lambda, subln_g[None, :], cfg["attn_tq"], lam_init)
    y_m = _mlstm(u3, gates3, conv_w, mlstm_norm_g[None, :], cfg["ml_chunk"])
    kv = _mm(mem.reshape(-1, d), w_mem_kv.astype(BF16), BF16, cfg["kv_tm"], "mem_kv")
    y_c = _mem_attention(u3, kv.reshape(b, -1, 2 * MA_HEADS * MA_HEAD_DIM), cfg["ma_tq"])

    x1, x1b, x1p = _merge(y_a.reshape(t, d), y_m.reshape(t, d), y_c.reshape(t, d), u2, x2,
                          w_branch.astype(BF16), w_out.astype(BF16), ln1_g[None, :],
                          ln1_b[None, :], cfg["merge_tm"])

    ek, rk, wt, cnt = _route(x1b, w_router.T.astype(BF16), router_bias.astype(F32)[:, None],
                             cfg["route_tm"])
    bm = cfg["expert_bm"]
    items, starts = _work_items(cnt[:, 0], t * TOP_K, bm)
    dest = _dest(starts, ek, rk, cfg["moe_tm"], cfg["dest_tw"]).reshape(-1)
    xs = _sc_scatter_rows(x1p, dest, cfg["moe_tm"])
    sh = _shared_ffn(x1b, w_s_gate.astype(BF16), w_s_up.astype(BF16), w_s_down.astype(BF16),
                     cfg["shared_tm"])
    ys = _experts(items, xs, w_e_gate, w_e_up, w_e_down, bm, after=sh)
    tm = cfg["moe_tm"]
    per = t // tm // cfg["combine_chunks"]
    out = None
    for c in range(cfg["combine_chunks"]):
        rows = dest[c * per * TOP_K * tm:(c + 1) * per * TOP_K * tm]
        yt = _sc_gather_rows(ys, rows).reshape(per, TOP_K, tm, d // 2)
        out = _ffn_out(yt, sh, x1, wt, ln2_g[None, :], ln2_b[None, :], tm, c * per, out)
    return out.reshape(b, s, d)


def _config(b, s):
    t = b * s
    return {
        "proj_tm": min(2048, t), "proj_tn": 1024,
        "attn_tq": 256, "ml_chunk": 256, "kv_tm": 512, "ma_tq": min(512, s),
        "merge_tm": 256, "route_tm": 512, "expert_bm": 512, "moe_tm": 256, "dest_tw": 2048,
        "shared_tm": 512, "combine_chunks": 4,
    }


def kernel(x, mem, rel_bias, w_in, b_in, conv_w, diff_lambda, subln_g, mlstm_norm_g, w_mem_kv,
           w_branch, w_out, ln1_g, ln1_b, w_router, router_bias, w_e_gate, w_e_up, w_e_down,
           w_s_gate, w_s_up, w_s_down, ln2_g, ln2_b):
    cfg = _config(x.shape[0], x.shape[1])
    for l in range(DEPTH):
        x = _layer(x, mem, rel_bias, w_in[l], b_in[l], conv_w[l], diff_lambda[l], subln_g[l],
                   mlstm_norm_g[l], w_mem_kv[l], w_branch[l], w_out[l], ln1_g[l], ln1_b[l],
                   w_router[l], router_bias[l], w_e_gate[l], w_e_up[l], w_e_down[l],
                   w_s_gate[l], w_s_up[l], w_s_down[l], ln2_g[l], ln2_b[l], l, cfg)
    return x
```

```python
import functools
import math

import numpy as np
import jax
import jax.numpy as jnp
from jax import lax
from jax.experimental import pallas as pl
from jax.experimental.pallas import tpu as pltpu
from jax.experimental.pallas import tpu_sc as plsc

F32 = jnp.float32
BF16 = jnp.bfloat16
U32 = jnp.uint32
I32 = jnp.int32

D_MODEL = 1024
DEPTH = 1
DA_HEAD_DIM = 64
DA_V_DIM = 128
DA_HEADS = 8
ML_HEADS = 4
ML_HEAD_DIM = 256
ML_CONV = 4
ML_GROUP = 2
MA_HEADS = 4
MA_HEAD_DIM = 256
N_BRANCH = 3
REL_BUCKETS = 32
REL_MAX_DIST = 128
N_EXPERTS = 256
TOP_K = 8
N_GROUP = 8
TOPK_GROUP = 4
D_EXPERT = 256
ROUTED_SCALE = 2.5
ALPHA = (2.0 * DEPTH) ** 0.25

LANES = 128
NEG = -1e30
LOG2E = math.log2(math.e)
FOLD_ACCS = 4
VMEM_LIMIT = 56 * 1024 * 1024

OFF_DAQ, OFF_DAK, OFF_DAV = 0, 8, 16
OFF_MLQ, OFF_MLK, OFF_MLV, OFF_MLO, OFF_MAQ, OFF_GATE = 24, 32, 40, 48, 56, 64
N_MAIN = 88 * LANES


def _cparams(*sem):
    return pltpu.CompilerParams(dimension_semantics=sem, vmem_limit_bytes=VMEM_LIMIT)


def _sigmoid(x):
    return 0.5 + 0.5 * jnp.tanh(0.5 * x)


def _layer_norm(z, g, b):
    mu = jnp.mean(z, axis=-1, keepdims=True)
    zc = z - mu
    var = jnp.mean(zc * zc, axis=-1, keepdims=True)
    return zc * lax.rsqrt(var + 1e-5) * g + b


def _proj_in_kernel(x_ref, w_ref, b_ref, wg_ref, bg_ref, u_ref, g_ref, xs_ref):
    @pl.when(pl.program_id(1) == 0)
    def _():
        xb = x_ref[...].astype(BF16)
        xs_ref[...] = xb
        g_ref[...] = jnp.dot(xb, wg_ref[...], preferred_element_type=F32) + bg_ref[...]

    acc = jnp.dot(xs_ref[...], w_ref[...], preferred_element_type=F32)
    u_ref[...] = (acc + b_ref[...]).astype(u_ref.dtype)


def _proj_in(x2, w_main, b_main, w_g, b_g, tm, tn):
    t, k = x2.shape
    n = w_main.shape[1]
    return pl.pallas_call(
        _proj_in_kernel,
        grid=(t // tm, n // tn),
        in_specs=[
            pl.BlockSpec((tm, k), lambda i, j: (i, 0)),
            pl.BlockSpec((k, tn), lambda i, j: (0, j)),
            pl.BlockSpec((1, tn), lambda i, j: (0, j)),
            pl.BlockSpec((k, LANES), lambda i, j: (0, 0)),
            pl.BlockSpec((1, LANES), lambda i, j: (0, 0)),
        ],
        out_specs=[
            pl.BlockSpec((tm, tn), lambda i, j: (i, j)),
            pl.BlockSpec((tm, LANES), lambda i, j: (i, 0)),
        ],
        out_shape=[jax.ShapeDtypeStruct((t, n), BF16), jax.ShapeDtypeStruct((t, LANES), F32)],
        scratch_shapes=[pltpu.VMEM((tm, k), BF16)],
        compiler_params=_cparams("parallel", "arbitrary"),
        name="proj_in",
    )(x2, w_main, b_main, w_g, b_g)


def _mm_kernel(x_ref, w_ref, o_ref):
    o_ref[...] = jnp.dot(x_ref[...].astype(BF16), w_ref[...],
                         preferred_element_type=F32).astype(o_ref.dtype)


def _mm(x2, w, out_dtype, tm, name):
    m, k = x2.shape
    n = w.shape[1]
    return pl.pallas_call(
        _mm_kernel,
        grid=(m // tm,),
        in_specs=[pl.BlockSpec((tm, k), lambda i: (i, 0)), pl.BlockSpec((k, n), lambda i: (0, 0))],
        out_specs=pl.BlockSpec((tm, n), lambda i: (i, 0)),
        out_shape=jax.ShapeDtypeStruct((m, n), out_dtype),
        compiler_params=_cparams("parallel"),
        name=name,
    )(x2, w)


def _fold8(x, op):
    n = x.shape[0] // 8
    accs = [x[8 * a:8 * a + 8, :] for a in range(min(FOLD_ACCS, n))]
    for a in range(FOLD_ACCS, n):
        accs[a % FOLD_ACCS] = op(accs[a % FOLD_ACCS], x[8 * a:8 * a + 8, :])
    while len(accs) > 1:
        accs = [op(accs[a], accs[a + 1]) for a in range(0, len(accs), 2)]
    return accs[0]


def _dattn_kernel(rb_ref, q_ref, k_ref, v_ref, bkt_ref, dl_ref, g_ref, o_ref,
                  st_ref, pt_ref, vt_ref, nb_ref, *, tq, seq, lam_init):
    h = pl.program_id(0)
    nq = seq // tq
    nt = (((1,), (1,)), ((), ()))

    @pl.when(pl.program_id(1) == 0)
    def _():
        far = rb_ref[REL_BUCKETS - 1, h]
        for t in range(2):
            bk = bkt_ref[t]
            tile = jnp.full((tq, tq), NEG, F32)
            for bb in range(REL_BUCKETS):
                tile = jnp.where(bk == bb, (rb_ref[bb, h] - far) * LOG2E, tile)
            nb_ref[t * tq:(t + 1) * tq, :] = tile

    for j in range(nq):
        vt_ref[:, j * tq:(j + 1) * tq] = v_ref[0, j * tq:(j + 1) * tq, :].astype(F32).T.astype(BF16)

    dl = dl_ref[...]
    lam = (jnp.exp(jnp.sum(dl[0:1] * dl[1:2], axis=1, keepdims=True))
           - jnp.exp(jnp.sum(dl[2:3] * dl[3:4], axis=1, keepdims=True)) + lam_init)
    lane = lax.broadcasted_iota(I32, (tq, LANES), 1)
    gain = g_ref[...] * (1.0 - lam_init)

    def logits_pass(i):
        qs = q_ref[0, i * tq:(i + 1) * tq, :] * jnp.asarray(DA_HEAD_DIM ** -0.5 * LOG2E, BF16)
        zero = jnp.zeros_like(qs)
        qm = (jnp.where(lane < DA_HEAD_DIM, qs, zero), jnp.where(lane >= DA_HEAD_DIM, qs, zero))
        mx = [None, None]

        def chunk(c):
            for half in range(2):
                s = lax.dot_general(k_ref[0, c * tq:(c + 1) * tq, :], qm[half], nt,
                                    preferred_element_type=F32)
                if c >= i - 1:
                    s = s + nb_ref[(c - i + 1) * tq:(c - i + 2) * tq, :]
                st_ref[2 * (i % 2) + half, c * tq:(c + 1) * tq, :] = s
                f = _fold8(s, jnp.maximum)
                mx[half] = f if mx[half] is None else jnp.maximum(mx[half], f)

        def result():
            return [jnp.max(mx[half], axis=0, keepdims=True) for half in range(2)]

        return [functools.partial(chunk, c) for c in range(i + 1)], result

    def exp_pass(i, m):
        l8 = [None, None]

        def chunk(c):
            for half in range(2):
                p = jnp.exp2(st_ref[2 * (i % 2) + half, c * tq:(c + 1) * tq, :] - m[half])
                pt_ref[2 * (i % 2) + half, c * tq:(c + 1) * tq, :] = p.astype(BF16)
                f = _fold8(p, jnp.add)
                l8[half] = f if l8[half] is None else l8[half] + f

        def result():
            return [jnp.sum(l8[half], axis=0, keepdims=True) for half in range(2)]

        return [functools.partial(chunk, c) for c in range(i + 1)], result

    def value_pass(i, l1, l2):
        kv = (i + 1) * tq
        acc = [jnp.dot(vt_ref[:, 0:kv], pt_ref[2 * (i % 2) + half, 0:kv, :],
                       preferred_element_type=F32) for half in range(2)]
        o = (acc[0] * (1.0 / l1) - acc[1] * (lam / l2)).T
        ms = jnp.mean(o * o, axis=1, keepdims=True)
        y = o * lax.rsqrt(ms + 1e-5) * gain
        o_ref[0, i * tq:(i + 1) * tq, :] = y.astype(o_ref.dtype)

    chunks, result = logits_pass(0)
    for run in chunks:
        run()
    m_next = result()
    for i in range(nq):
        exp_chunks, exp_result = exp_pass(i, m_next)
        next_chunks, next_result = logits_pass(i + 1) if i + 1 < nq else ([], None)
        for c in range(max(len(exp_chunks), len(next_chunks))):
            if c < len(next_chunks):
                next_chunks[c]()
            if c < len(exp_chunks):
                exp_chunks[c]()
        if next_result is not None:
            m_next = next_result()
        value_pass(i, *exp_result())


def _t5_bucket_np(dist):
    n = np.maximum(dist, 0)
    max_exact = REL_BUCKETS // 2
    large = max_exact + (np.log(np.maximum(n, 1).astype(np.float32) / max_exact)
                         / math.log(REL_MAX_DIST / max_exact)
                         * (REL_BUCKETS - max_exact)).astype(np.int32)
    large = np.minimum(large, REL_BUCKETS - 1)
    return np.where(n < max_exact, n, large)


def _near_buckets(tq):
    assert _t5_bucket_np(np.array([tq + 1]))[0] == REL_BUCKETS - 1
    c = np.arange(tq)[:, None]
    r = np.arange(tq)[None, :]
    left = _t5_bucket_np(r + tq - c)
    diag = np.where(r >= c, _t5_bucket_np(r - c), -1)
    return np.stack([left, diag], axis=0).astype(np.int32)


def _diff_attention(u3, rel_bias, diff_lambda, subln_g, tq, lam_init):
    b, s, _ = u3.shape
    kern = functools.partial(_dattn_kernel, tq=tq, seq=s, lam_init=lam_init)
    grid_spec = pltpu.PrefetchScalarGridSpec(
        num_scalar_prefetch=1,
        grid=(DA_HEADS, b),
        in_specs=[
            pl.BlockSpec((1, s, LANES), lambda h, bi, rb: (bi, 0, OFF_DAQ + h)),
            pl.BlockSpec((1, s, LANES), lambda h, bi, rb: (bi, 0, OFF_DAK + h)),
            pl.BlockSpec((1, s, LANES), lambda h, bi, rb: (bi, 0, OFF_DAV + h)),
            pl.BlockSpec((2, tq, tq), lambda h, bi, rb: (0, 0, 0)),
            pl.BlockSpec((4, DA_HEAD_DIM), lambda h, bi, rb: (0, 0)),
            pl.BlockSpec((1, DA_V_DIM), lambda h, bi, rb: (0, 0)),
        ],
        out_specs=pl.BlockSpec((1, s, LANES), lambda h, bi, rb: (bi, 0, h)),
        scratch_shapes=[pltpu.VMEM((4, s, tq), F32), pltpu.VMEM((4, s, tq), BF16),
                        pltpu.VMEM((DA_V_DIM, s), BF16),
                        pltpu.VMEM((2 * tq, tq), F32)],
    )
    return pl.pallas_call(
        kern,
        grid_spec=grid_spec,
        out_shape=jax.ShapeDtypeStruct((b, s, DA_HEADS * DA_V_DIM), BF16),
        compiler_params=_cparams("arbitrary", "arbitrary"),
        name="diff_attn",
    )(rel_bias.astype(F32), u3, u3, u3, jnp.asarray(_near_buckets(tq)), diff_lambda, subln_g)


def _mlstm_kernel(q_ref, k_ref, v_ref, og_ref, g_ref, cwq_ref, cwk_ref, ng_ref, y_ref,
                  ct_ref, gts_ref, *, chunk, seq):
    first_head = pl.program_id(1) * ML_GROUP
    nc = seq // chunk
    hd = ML_HEAD_DIM
    lane = lax.broadcasted_iota(jnp.int32, (chunk, LANES), 1)
    row = lax.broadcasted_iota(jnp.int32, (chunk, chunk), 0)
    col = lax.broadcasted_iota(jnp.int32, (chunk, chunk), 1)
    causal = row >= col
    tri = causal.astype(F32)
    ct_ref[...] = jnp.zeros(ct_ref.shape, F32)

    def conv_silu(ref, cw_ref, r0, c, cols):
        cw = cw_ref[:, cols]
        cur = ref[0, pl.ds(r0, chunk), cols].astype(F32)
        p0 = pl.multiple_of(jnp.maximum(r0 - 16, 0), 16)
        prev = ref[0, pl.ds(p0, 16), cols].astype(F32)
        prev = jnp.where(c > 0, prev, 0.0)
        x = jnp.concatenate([prev, cur], axis=0)
        out = cw[ML_CONV - 1:ML_CONV] * cur
        for tap in range(ML_CONV - 1):
            shifted = pltpu.roll(x, ML_CONV - 1 - tap, 0)[16:]
            out = out + cw[tap:tap + 1] * shifted
        return out * _sigmoid(out)

    def head_step(j, c, r0, g, bc, m, n):
        h = first_head + j
        cols = slice(j * hd, (j + 1) * hd)
        irow = gts_ref[pl.ds(h, 1), :]
        brow = gts_ref[pl.ds(ML_HEADS + h, 1), :]
        bcol = jnp.sum(jnp.where(lane == ML_HEADS + h, bc, 0.0), axis=1, keepdims=True)
        icol = jnp.sum(jnp.where(lane == h, g, 0.0), axis=1, keepdims=True)

        dmat = jnp.where(causal, bcol - brow + irow, NEG)
        inter = bcol + m
        m_row = jnp.maximum(inter, jnp.max(dmat, axis=1, keepdims=True))

        q = conv_silu(q_ref, cwq_ref, r0, c, cols)
        k = conv_silu(k_ref, cwk_ref, r0, c, cols) * (hd ** -0.5)
        qb = q.astype(BF16)
        kb = k.astype(BF16)
        vb = v_ref[0, pl.ds(r0, chunk), cols]
        sqk = lax.dot_general(qb, kb, (((1,), (1,)), ((), ())), preferred_element_type=F32)
        w = jnp.exp(dmat - m_row) * sqk
        inter_w = jnp.exp(inter - m_row)
        ct = ct_ref[j]
        num = (inter_w * jnp.dot(qb, ct.astype(BF16), preferred_element_type=F32)
               + jnp.dot(w.astype(BF16), vb, preferred_element_type=F32))
        den = inter_w * jnp.sum(q * n, axis=1, keepdims=True) + jnp.sum(w, axis=1, keepdims=True)
        hout = num / jnp.maximum(jnp.abs(den), jnp.exp(-m_row))
        mu = jnp.mean(hout, axis=1, keepdims=True)
        hc = hout - mu
        var = jnp.mean(hc * hc, axis=1, keepdims=True)
        hn = hc * lax.rsqrt(var + 1e-5) * ng_ref[:, cols]
        og = og_ref[0, pl.ds(r0, chunk), cols].astype(F32)
        y_ref[0, pl.ds(r0, chunk), cols] = (_sigmoid(og) * hn).astype(y_ref.dtype)

        total = brow[:, chunk - 1:chunk]
        grow = total - brow + irow
        m_new = jnp.maximum(total + m, jnp.max(grow, axis=1, keepdims=True))
        decay = jnp.exp(total + m - m_new)
        ws = jnp.exp(total - bcol + icol - m_new)
        wsv = (ws * vb.astype(F32)).astype(BF16)
        ct_ref[j] = decay * ct + jnp.dot(k.T.astype(BF16), wsv, preferred_element_type=F32)
        n_new = decay * n + jnp.sum(ws * k, axis=0, keepdims=True)
        return m_new, n_new

    def body(c, carry):
        r0 = pl.multiple_of(c * chunk, chunk)
        g = g_ref[0, pl.ds(r0, chunk), :]
        logf = jnp.minimum(g, 0.0) - jnp.log(1.0 + jnp.exp(-jnp.abs(g)))
        bc = jnp.dot(tri, logf, precision=lax.Precision.HIGHEST, preferred_element_type=F32)
        gts_ref[...] = jnp.where(lane < ML_HEADS, g, bc).T
        return tuple(head_step(j, c, r0, g, bc, *carry[j]) for j in range(ML_GROUP))

    init = tuple((jnp.zeros((1, 1), F32), jnp.zeros((1, hd), F32)) for _ in range(ML_GROUP))
    lax.fori_loop(0, nc, body, init)


def _mlstm(u3, gates3, conv_w, norm_g, chunk):
    b, s, _ = u3.shape
    gw = ML_GROUP * ML_HEAD_DIM
    nq = gw // LANES
    ngroups = ML_HEADS // ML_GROUP
    kern = functools.partial(_mlstm_kernel, chunk=chunk, seq=s)

    def ublock(off):
        return pl.BlockSpec((1, s, gw), lambda bi, h: (bi, 0, off // nq + h))

    return pl.pallas_call(
        kern,
        grid=(b, ngroups),
        in_specs=[
            ublock(OFF_MLQ), ublock(OFF_MLK), ublock(OFF_MLV), ublock(OFF_MLO),
            pl.BlockSpec((1, s, LANES), lambda bi, h: (bi, 0, 0)),
            pl.BlockSpec((ML_CONV, gw), lambda bi, h: (0, h)),
            pl.BlockSpec((ML_CONV, gw), lambda bi, h: (0, ngroups + h)),
            pl.BlockSpec((1, gw), lambda bi, h: (0, h)),
        ],
        out_specs=pl.BlockSpec((1, s, gw), lambda bi, h: (bi, 0, h)),
        out_shape=jax.ShapeDtypeStruct((b, s, ML_HEADS * ML_HEAD_DIM), BF16),
        scratch_shapes=[pltpu.VMEM((ML_GROUP, ML_HEAD_DIM, ML_HEAD_DIM), F32),
                        pltpu.VMEM((LANES, chunk), F32)],
        compiler_params=_cparams("parallel", "arbitrary"),
        name="mlstm",
    )(u3, u3, u3, u3, gates3, conv_w, conv_w, norm_g)


def _memattn_kernel(q_ref, k_ref, v_ref, o_ref, *, tq, seq):
    kb = k_ref[0]
    vb = v_ref[0]
    scale = jnp.asarray(MA_HEAD_DIM ** -0.5, BF16)
    for t in range(seq // tq):
        q = q_ref[0, t * tq:(t + 1) * tq, :] * scale
        s = lax.dot_general(q, kb, (((1,), (1,)), ((), ())), preferred_element_type=F32)
        p = jnp.exp(s - jnp.max(s, axis=1, keepdims=True))
        inv = 1.0 / jnp.sum(p, axis=1, keepdims=True)
        o = jnp.dot(p.astype(BF16), vb, preferred_element_type=F32) * inv
        o_ref[0, t * tq:(t + 1) * tq, :] = o.astype(o_ref.dtype)


def _mem_attention(u3, kv3, tq):
    b, s, _ = u3.shape
    mlen = kv3.shape[1]
    hd = MA_HEAD_DIM
    nq = hd // LANES
    kern = functools.partial(_memattn_kernel, tq=tq, seq=s)
    return pl.pallas_call(
        kern,
        grid=(b, MA_HEADS),
        in_specs=[
            pl.BlockSpec((1, s, hd), lambda bi, h: (bi, 0, OFF_MAQ // nq + h)),
            pl.BlockSpec((1, mlen, hd), lambda bi, h: (bi, 0, h)),
            pl.BlockSpec((1, mlen, hd), lambda bi, h: (bi, 0, MA_HEADS + h)),
        ],
        out_specs=pl.BlockSpec((1, s, hd), lambda bi, h: (bi, 0, h)),
        out_shape=jax.ShapeDtypeStruct((b, s, MA_HEADS * hd), BF16),
        compiler_params=_cparams("parallel", "parallel"),
        name="mem_attn",
    )(u3, kv3, kv3)


def _merge_kernel(ya_ref, ym_ref, yc_ref, g0_ref, g1_ref, g2_ref, x_ref, wb_ref, wo_ref,
                  lg_ref, lb_ref, x1_ref, x1b_ref, x1p_ref):
    acc = None
    for n, (y_ref, g_ref) in enumerate(((ya_ref, g0_ref), (ym_ref, g1_ref), (yc_ref, g2_ref))):
        pr = jnp.dot(y_ref[...], wb_ref[n], preferred_element_type=F32)
        t = jax.nn.sigmoid(g_ref[...].astype(F32)) * pr
        acc = t if acc is None else acc + t
    out = jnp.dot(acc.astype(BF16), wo_ref[...], preferred_element_type=F32)
    x1 = _layer_norm(ALPHA * x_ref[...] + out, lg_ref[...], lb_ref[...])
    x1_ref[...] = x1
    x1b = x1.astype(BF16)
    x1b_ref[...] = x1b
    x1p_ref[...] = _pack_halves(x1b)


def _pack_halves(vb):
    w = vb.shape[1] // 2
    hi = lax.bitcast_convert_type(vb[:, :w].astype(F32), U32)
    lo = lax.bitcast_convert_type(vb[:, w:].astype(F32), U32)
    return hi | (lo >> 16)


def _unpack_halves(u):
    hi = lax.bitcast_convert_type(u & jnp.uint32(0xFFFF0000), F32)
    lo = lax.bitcast_convert_type(u << 16, F32)
    return hi, lo


def _merge(ya, ym, yc, u2, x2, wb, wo, lg, lb, tm):
    t, d = x2.shape
    gb = OFF_GATE * LANES // d

    def rows(i):
        return (i, 0)

    return pl.pallas_call(
        _merge_kernel,
        grid=(t // tm,),
        in_specs=[
            pl.BlockSpec((tm, d), rows), pl.BlockSpec((tm, d), rows), pl.BlockSpec((tm, d), rows),
            pl.BlockSpec((tm, d), lambda i: (i, gb)),
            pl.BlockSpec((tm, d), lambda i: (i, gb + 1)),
            pl.BlockSpec((tm, d), lambda i: (i, gb + 2)),
            pl.BlockSpec((tm, d), rows),
            pl.BlockSpec((N_BRANCH, d, d), lambda i: (0, 0, 0)),
            pl.BlockSpec((d, d), lambda i: (0, 0)),
            pl.BlockSpec((1, d), lambda i: (0, 0)),
            pl.BlockSpec((1, d), lambda i: (0, 0)),
        ],
        out_specs=[pl.BlockSpec((tm, d), rows), pl.BlockSpec((tm, d), rows),
                   pl.BlockSpec((tm, d // 2), rows)],
        out_shape=[jax.ShapeDtypeStruct((t, d), F32), jax.ShapeDtypeStruct((t, d), BF16),
                   jax.ShapeDtypeStruct((t, d // 2), U32)],
        compiler_params=_cparams("parallel"),
        name="merge_ln1",
    )(ya, ym, yc, u2, u2, u2, x2, wb, wo, lg, lb)


def _expert_kernel(ib_ref, ie_ref, lo_ref, hi_ref, first_ref, nit_ref, new_ref, slot_ref, next_ref,
                   x_ref, wg_hbm, wu_hbm, wd_hbm, after_hbm, o_ref, wgf_ref, wuf_ref, wdf_ref,
                   wgb_ref, wub_ref, wdb_ref, blk_ref, wsem):
    i = pl.program_id(0)

    def weight_copies(e, s):
        return (pltpu.make_async_copy(wg_hbm.at[e], wgf_ref.at[s], wsem.at[s]),
                pltpu.make_async_copy(wu_hbm.at[e], wuf_ref.at[s], wsem.at[s]),
                pltpu.make_async_copy(wd_hbm.at[e], wdf_ref.at[s], wsem.at[s]))

    @pl.when(i == 0)
    def _():
        blk_ref[...] = jnp.zeros(blk_ref.shape, blk_ref.dtype)
        for cp in weight_copies(ie_ref[0], 0):
            cp.start()

    @pl.when(new_ref[i] == 1)
    def _():
        s = slot_ref[i]
        for cp in weight_copies(ie_ref[i], s):
            cp.wait()

        @pl.when(next_ref[i] >= 0)
        def _():
            for cp in weight_copies(next_ref[i], 1 - s):
                cp.start()

        wgb_ref[...] = wgf_ref[s].astype(BF16)
        wub_ref[...] = wuf_ref[s].astype(BF16)
        wdb_ref[...] = wdf_ref[s].astype(BF16)

    @pl.when(i < nit_ref[0])
    def _():
        xl, xr = _unpack_halves(x_ref[...])
        xl = xl.astype(BF16)
        xr = xr.astype(BF16)
        half = xl.shape[1]

        def up(w_ref):
            return (jnp.dot(xl, w_ref[:half, :], preferred_element_type=F32)
                    + jnp.dot(xr, w_ref[half:, :], preferred_element_type=F32))

        hg = up(wgb_ref)
        hu = up(wub_ref)
        act = (hg * jax.nn.sigmoid(hg) * hu).astype(BF16)
        y = _pack_halves(jnp.dot(act, wdb_ref[...], preferred_element_type=F32).astype(BF16))
        row = lax.broadcasted_iota(I32, y.shape, 0)
        mine = (row >= lo_ref[i]) & (row < hi_ref[i])

        keep = jnp.where(first_ref[i] == 1, jnp.uint32(0), blk_ref[...])
        out = jnp.where(mine, y, keep)
        blk_ref[...] = out
        o_ref[...] = out


def _work_items(counts, n_pairs, bm):
    assert n_pairs % bm == 0
    nblocks = n_pairs // bm
    ends = jnp.cumsum(counts)
    starts = ends - counts
    first_blk = starts // bm
    n_e = jnp.where(counts > 0, (ends - 1) // bm - first_blk + 1, 0)
    item_end = jnp.cumsum(n_e)
    item_start = item_end - n_e
    n_items = item_end[-1]
    i = jnp.arange(nblocks + N_EXPERTS)
    valid = i < n_items
    e = jnp.minimum(jnp.sum(item_end[None, :] <= jnp.minimum(i, n_items - 1)[:, None], axis=1),
                    N_EXPERTS - 1)
    blk = jnp.where(valid, first_blk[e] + i - item_start[e], nblocks - 1)
    lo = jnp.clip(starts[e] - blk * bm, 0, bm)
    hi = jnp.where(valid, jnp.clip(ends[e] - blk * bm, 0, bm), 0)
    first = jnp.concatenate([jnp.ones((1,), I32), (blk[1:] != blk[:-1]).astype(I32)])
    new = jnp.where(valid, jnp.concatenate([jnp.ones((1,), bool), e[1:] != e[:-1]]), False)
    slot = (jnp.cumsum(new) - 1) % 2
    ids = jnp.arange(N_EXPERTS)
    later = jnp.where((counts > 0)[None, :] & (ids[None, :] > ids[:, None]), ids[None, :], N_EXPERTS)
    next_e = jnp.min(later, axis=1)
    nxt = jnp.where(next_e[e] < N_EXPERTS, next_e[e], -1)
    items = tuple(a.astype(I32) for a in (blk, e, lo, hi, first, n_items[None], new, slot, nxt))
    return items, starts.astype(I32)


def _experts(items, xs, w_gate, w_up, w_down, bm, after):
    n, dh = xs.shape
    d = 2 * dh
    de = w_gate.shape[2]

    def rows(i, ib, *_):
        return (ib[i], 0)

    grid_spec = pltpu.PrefetchScalarGridSpec(
        num_scalar_prefetch=len(items),
        grid=(items[0].shape[0],),
        in_specs=[
            pl.BlockSpec((bm, dh), rows),
            pl.BlockSpec(memory_space=pl.ANY),
            pl.BlockSpec(memory_space=pl.ANY),
            pl.BlockSpec(memory_space=pl.ANY),
            pl.BlockSpec(memory_space=pl.ANY),
        ],
        out_specs=pl.BlockSpec((bm, dh), rows),
        scratch_shapes=[pltpu.VMEM((2, d, de), F32), pltpu.VMEM((2, d, de), F32),
                        pltpu.VMEM((2, de, d), F32),
                        pltpu.VMEM((d, de), BF16), pltpu.VMEM((d, de), BF16),
                        pltpu.VMEM((de, d), BF16), pltpu.VMEM((bm, dh), U32),
                        pltpu.SemaphoreType.DMA((2,))],
    )
    return pl.pallas_call(
        _expert_kernel,
        grid_spec=grid_spec,
        out_shape=jax.ShapeDtypeStruct((n, dh), U32),
        compiler_params=_cparams("arbitrary"),
        name="experts",
    )(*items, xs, w_gate, w_up, w_down, after)


SC_WINDOW = 128
SC_WORKERS = 32


def _sc_worker():
    return lax.axis_index("core") * (SC_WORKERS // 2) + lax.axis_index("subcore")


def _sc_scatter_rows(x, idx, tm):
    t, dh = x.shape
    n = idx.shape[0]
    assert n == t * TOP_K and tm % SC_WINDOW == 0
    windows = t // SC_WINDOW
    per = windows // SC_WORKERS
    assert windows % SC_WORKERS == 0
    wpt = tm // SC_WINDOW
    mesh = plsc.VectorSubcoreMesh(core_axis_name="core", subcore_axis_name="subcore")

    @pl.kernel(out_type=jax.ShapeDtypeStruct((n, dh), x.dtype), mesh=mesh,
               scratch_types=[pltpu.VMEM((SC_WINDOW,), I32), pltpu.VMEM((SC_WINDOW, dh), x.dtype)])
    def scatter(x_hbm, idx_hbm, out_hbm, idx_vmem, rows_vmem):
        worker = _sc_worker()

        @pl.loop(0, per)
        def _(j):
            w = worker * per + j
            tile = w // wpt
            off = (w - tile * wpt) * SC_WINDOW
            pltpu.sync_copy(x_hbm.at[pl.ds(w * SC_WINDOW, SC_WINDOW)], rows_vmem)

            @pl.loop(0, TOP_K)
            def _(k):
                base = (tile * TOP_K + k) * tm + off
                pltpu.sync_copy(idx_hbm.at[pl.ds(base, SC_WINDOW)], idx_vmem)
                pltpu.sync_copy(rows_vmem, out_hbm.at[idx_vmem])

    return scatter(x, idx)


def _sc_gather_rows(src, idx):
    n = idx.shape[0]
    dh = src.shape[1]
    per = n // SC_WORKERS
    assert per % SC_WINDOW == 0
    mesh = plsc.VectorSubcoreMesh(core_axis_name="core", subcore_axis_name="subcore")

    @pl.kernel(out_type=jax.ShapeDtypeStruct((n, dh), src.dtype), mesh=mesh,
               scratch_types=[pltpu.VMEM((SC_WINDOW,), I32), pltpu.VMEM((SC_WINDOW, dh), src.dtype)])
    def gather(src_hbm, idx_hbm, out_hbm, idx_vmem, rows_vmem):
        worker = _sc_worker()

        @pl.loop(0, per // SC_WINDOW)
        def _(j):
            base = worker * per + j * SC_WINDOW
            pltpu.sync_copy(idx_hbm.at[pl.ds(base, SC_WINDOW)], idx_vmem)
            pltpu.sync_copy(src_hbm.at[idx_vmem], rows_vmem)
            pltpu.sync_copy(rows_vmem, out_hbm.at[pl.ds(base, SC_WINDOW)])

    return gather(src, idx)


def _shared_ffn_kernel(xb_ref, wg_ref, wu_ref, wd_ref, o_ref):
    xb = xb_ref[...]
    hg = jnp.dot(xb, wg_ref[...], preferred_element_type=F32)
    hu = jnp.dot(xb, wu_ref[...], preferred_element_type=F32)
    act = (hg * jax.nn.sigmoid(hg) * hu).astype(BF16)
    o_ref[...] = jnp.dot(act, wd_ref[...], preferred_element_type=F32)


def _shared_ffn(x1b, wg, wu, wd, tm):
    t, d = x1b.shape
    ds = wg.shape[1]
    return pl.pallas_call(
        _shared_ffn_kernel,
        grid=(t // tm,),
        in_specs=[pl.BlockSpec((tm, d), lambda i: (i, 0)), pl.BlockSpec((d, ds), lambda i: (0, 0)),
                  pl.BlockSpec((d, ds), lambda i: (0, 0)), pl.BlockSpec((ds, d), lambda i: (0, 0))],
        out_specs=pl.BlockSpec((tm, d), lambda i: (i, 0)),
        out_shape=jax.ShapeDtypeStruct((t, d), F32),
        compiler_params=_cparams("parallel"),
        name="shared_ffn",
    )(x1b, wg, wu, wd)


def _ffn_out_kernel(yt_ref, sh_ref, x1_ref, wt_ref, lg_ref, lb_ref, *rest):
    o_ref = rest[-1]
    wt = wt_ref[...]
    rl = None
    rr = None
    for k in range(TOP_K):
        hi, lo = _unpack_halves(yt_ref[0, k])
        wk = wt[:, k:k + 1]
        rl = wk * hi if rl is None else rl + wk * hi
        rr = wk * lo if rr is None else rr + wk * lo
    z = ALPHA * x1_ref[...] + sh_ref[...] + jnp.concatenate([rl, rr], axis=1)
    o_ref[...] = _layer_norm(z, lg_ref[...], lb_ref[...])


def _ffn_out(yt, sh, x1, wt, lg, lb, tm, tile0, prev):
    t, d = x1.shape
    dh = yt.shape[-1]
    assert yt.shape[1:] == (TOP_K, tm, dh)

    def rows(i):
        return (i + tile0, 0)

    def whole(i):
        return (0, 0)

    in_specs = [
        pl.BlockSpec((1, TOP_K, tm, dh), lambda i: (i, 0, 0, 0)),
        pl.BlockSpec((tm, d), rows), pl.BlockSpec((tm, d), rows), pl.BlockSpec((tm, LANES), rows),
        pl.BlockSpec((1, d), whole), pl.BlockSpec((1, d), whole),
    ]
    args = [yt, sh, x1, wt, lg, lb]
    aliases = {}
    if prev is not None:
        in_specs.append(pl.BlockSpec(memory_space=pl.ANY))
        args.append(prev)
        aliases = {len(args) - 1: 0}
    return pl.pallas_call(
        _ffn_out_kernel,
        grid=(yt.shape[0],),
        in_specs=in_specs,
        out_specs=pl.BlockSpec((tm, d), rows),
        out_shape=jax.ShapeDtypeStruct((t, d), F32),
        input_output_aliases=aliases,
        compiler_params=_cparams("parallel"),
        name="ffn_out_ln2",
    )(*args)


def _route_kernel(xb_ref, wrt_ref, rb_ref, ek_ref, rk_ref, wt_ref, cnt_ref, upper_ref, run_ref, *,
                  tm):
    i = pl.program_id(0)
    gsz = N_EXPERTS // N_GROUP
    ninf = -jnp.inf

    @pl.when(i == 0)
    def _():
        r = lax.broadcasted_iota(I32, (tm, tm), 0)
        c = lax.broadcasted_iota(I32, (tm, tm), 1)
        upper_ref[...] = jnp.where(r < c, 1.0, 0.0).astype(BF16)
        run_ref[...] = jnp.zeros(run_ref.shape, F32)

    logits = lax.dot_general(wrt_ref[...], xb_ref[...], (((1,), (1,)), ((), ())),
                             preferred_element_type=F32)
    scores = jax.nn.sigmoid(logits)
    choice = scores + rb_ref[...]

    ridx = lax.broadcasted_iota(I32, (gsz, tm), 0)
    gscore = []
    for g in range(N_GROUP):
        blk = choice[g * gsz:(g + 1) * gsz, :]
        m1 = jnp.max(blk, axis=0, keepdims=True)
        i1 = jnp.min(jnp.where(blk == m1, ridx, gsz), axis=0, keepdims=True)
        m2 = jnp.max(jnp.where(ridx == i1, ninf, blk), axis=0, keepdims=True)
        gscore.append(m1 + m2)
    masked = []
    for g in range(N_GROUP):
        beaten = jnp.zeros((1, tm), I32)
        for g2 in range(N_GROUP):
            if g2 == g:
                continue
            wins = (gscore[g2] >= gscore[g]) if g2 < g else (gscore[g2] > gscore[g])
            beaten = beaten + jnp.where(wins, 1, 0)
        masked.append(jnp.where(beaten < TOPK_GROUP, choice[g * gsz:(g + 1) * gsz, :], ninf))
    v = jnp.concatenate(masked, axis=0)

    eidx = lax.broadcasted_iota(I32, (N_EXPERTS, tm), 0)
    sel = jnp.zeros((N_EXPERTS, tm), F32)
    e_rows = []
    s_rows = []
    for k in range(TOP_K):
        m = jnp.max(v, axis=0, keepdims=True)
        ik = jnp.min(jnp.where(v == m, eidx, N_EXPERTS), axis=0, keepdims=True)
        hit = eidx == ik
        e_rows.append(ik)
        s_rows.append(jnp.sum(jnp.where(hit, scores, 0.0), axis=0, keepdims=True))
        v = jnp.where(hit, ninf, v)
        sel = jnp.where(hit, 1.0, sel)

    prefix = jnp.dot(sel.astype(BF16), upper_ref[...], preferred_element_type=F32)
    pos = prefix + run_ref[...]
    for k in range(TOP_K):
        rk = jnp.sum(jnp.where(eidx == e_rows[k], pos, 0.0), axis=0, keepdims=True)
        ek_ref[k:k + 1, :] = e_rows[k]
        rk_ref[k:k + 1, :] = rk.astype(I32)
    run_ref[...] = run_ref[...] + jnp.sum(sel, axis=1, keepdims=True)
    cnt_ref[...] = jnp.broadcast_to(run_ref[...], cnt_ref.shape).astype(I32)

    ssum = s_rows[0]
    for k in range(1, TOP_K):
        ssum = ssum + s_rows[k]
    w_rows = [s / (ssum + 1e-20) * ROUTED_SCALE for s in s_rows]
    w_rows.append(jnp.zeros((LANES - TOP_K, tm), F32))
    wt_ref[...] = jnp.concatenate(w_rows, axis=0).T


def _route(x1b, wrt, rbias, tm):
    t, d = x1b.shape
    return pl.pallas_call(
        functools.partial(_route_kernel, tm=tm),
        grid=(t // tm,),
        in_specs=[pl.BlockSpec((tm, d), lambda i: (i, 0)),
                  pl.BlockSpec((N_EXPERTS, d), lambda i: (0, 0)),
                  pl.BlockSpec((N_EXPERTS, 1), lambda i: (0, 0))],
        out_specs=[pl.BlockSpec((TOP_K, tm), lambda i: (0, i)),
                   pl.BlockSpec((TOP_K, tm), lambda i: (0, i)),
                   pl.BlockSpec((tm, LANES), lambda i: (i, 0)),
                   pl.BlockSpec((N_EXPERTS, LANES), lambda i: (0, 0))],
        out_shape=[jax.ShapeDtypeStruct((TOP_K, t), I32), jax.ShapeDtypeStruct((TOP_K, t), I32),
                   jax.ShapeDtypeStruct((t, LANES), F32),
                   jax.ShapeDtypeStruct((N_EXPERTS, LANES), I32)],
        scratch_shapes=[pltpu.VMEM((tm, tm), BF16), pltpu.VMEM((N_EXPERTS, 1), F32)],
        compiler_params=_cparams("arbitrary"),
        name="route",
    )(x1b, wrt, rbias)


def _dest_kernel(ps_ref, ek_ref, rk_ref, d_ref, *, tm):
    e = ek_ref[...]

    def body(j, acc):
        return acc + jnp.where(e == j, ps_ref[j], 0)

    res = lax.fori_loop(0, N_EXPERTS, body, jnp.zeros(e.shape, I32)) + rk_ref[...]
    for a in range(e.shape[1] // tm):
        for k in range(TOP_K):
            d_ref[a:a + 1, k * tm:(k + 1) * tm] = res[k:k + 1, a * tm:(a + 1) * tm]


def _dest(pstart, ek, rk, tm, tw):
    t = ek.shape[1]
    grid_spec = pltpu.PrefetchScalarGridSpec(
        num_scalar_prefetch=1,
        grid=(t // tw,),
        in_specs=[pl.BlockSpec((TOP_K, tw), lambda i, ps: (0, i)),
                  pl.BlockSpec((TOP_K, tw), lambda i, ps: (0, i))],
        out_specs=pl.BlockSpec((tw // tm, TOP_K * tm), lambda i, ps: (i, 0)),
    )
    return pl.pallas_call(
        functools.partial(_dest_kernel, tm=tm),
        grid_spec=grid_spec,
        out_shape=jax.ShapeDtypeStruct((t // tm, TOP_K * tm), I32),
        compiler_params=_cparams("parallel"),
        name="dest",
    )(pstart, ek, rk)


def _layer(x, mem, rel_bias, w_in, b_in, conv_w, diff_lambda, subln_g, mlstm_norm_g, w_mem_kv,
           w_branch, w_out, ln1_g, ln1_b, w_router, router_bias, w_e_gate, w_e_up, w_e_down,
           w_s_gate, w_s_up, w_s_down, ln2_g, ln2_b, layer_idx, cfg):
    b, s, d = x.shape
    t = b * s
    x2 = x.reshape(t, d)

    g0 = (OFF_MLO + 8) * LANES
    w_main = jnp.concatenate([w_in[:, :g0], w_in[:, g0 + 2 * ML_HEADS:]], axis=1).astype(BF16)
    b_main = jnp.concatenate([b_in[:g0], b_in[g0 + 2 * ML_HEADS:]])[None, :]
    w_g = jnp.pad(w_in[:, g0:g0 + 2 * ML_HEADS], ((0, 0), (0, LANES - 2 * ML_HEADS))).astype(BF16)
    b_g = jnp.pad(b_in[g0:g0 + 2 * ML_HEADS], (0, LANES - 2 * ML_HEADS))[None, :]

    u2, gates2 = _proj_in(x2, w_main, b_main, w_g, b_g, cfg["proj_tm"], cfg["proj_tn"])
    u3 = u2.reshape(b, s, N_MAIN)
    gates3 = gates2.reshape(b, s, LANES)

    lam_init = 0.8 - 0.6 * math.exp(-0.3 * layer_idx)
    y_a = _diff_attention(u3, rel_bias, diff_lambda, subln_g[None, :], cfg["attn_tq"], lam_init)
    y_m = _mlstm(u3, gates3, conv_w, mlstm_norm_g[None, :], cfg["ml_chunk"])
    kv = _mm(mem.reshape(-1, d), w_mem_kv.astype(BF16), BF16, cfg["kv_tm"], "mem_kv")
    y_c = _mem_attention(u3, kv.reshape(b, -1, 2 * MA_HEADS * MA_HEAD_DIM), cfg["ma_tq"])

    x1, x1b, x1p = _merge(y_a.reshape(t, d), y_m.reshape(t, d), y_c.reshape(t, d), u2, x2,
                          w_branch.astype(BF16), w_out.astype(BF16), ln1_g[None, :],
                          ln1_b[None, :], cfg["merge_tm"])

    ek, rk, wt, cnt = _route(x1b, w_router.T.astype(BF16), router_bias.astype(F32)[:, None],
                             cfg["route_tm"])
    bm = cfg["expert_bm"]
    items, starts = _work_items(cnt[:, 0], t * TOP_K, bm)
    dest = _dest(starts, ek, rk, cfg["moe_tm"], cfg["dest_tw"]).reshape(-1)
    xs = _sc_scatter_rows(x1p, dest, cfg["moe_tm"])
    sh = _shared_ffn(x1b, w_s_gate.astype(BF16), w_s_up.astype(BF16), w_s_down.astype(BF16),
                     cfg["shared_tm"])
    ys = _experts(items, xs, w_e_gate, w_e_up, w_e_down, bm, after=sh)
    tm = cfg["moe_tm"]
    per = t // tm // cfg["combine_chunks"]
    out = None
    for c in range(cfg["combine_chunks"]):
        rows = dest[c * per * TOP_K * tm:(c + 1) * per * TOP_K * tm]
        yt = _sc_gather_rows(ys, rows).reshape(per, TOP_K, tm, d // 2)
        out = _ffn_out(yt, sh, x1, wt, ln2_g[None, :], ln2_b[None, :], tm, c * per, out)
    return out.reshape(b, s, d)


def _config(b, s):
    t = b * s
    return {
        "proj_tm": min(2048, t), "proj_tn": 1024,
        "attn_tq": 256, "ml_chunk": 256, "kv_tm": 512, "ma_tq": min(512, s),
        "merge_tm": 512, "route_tm": 512, "expert_bm": 512, "moe_tm": 256, "dest_tw": 2048,
        "shared_tm": 512, "combine_chunks": 4,
    }


def kernel(x, mem, rel_bias, w_in, b_in, conv_w, diff_lambda, subln_g, mlstm_norm_g, w_mem_kv,
           w_branch, w_out, ln1_g, ln1_b, w_router, router_bias, w_e_gate, w_e_up, w_e_down,
           w_s_gate, w_s_up, w_s_down, ln2_g, ln2_b):
    cfg = _config(x.shape[0], x.shape[1])
    for l in range(DEPTH):
        x = _layer(x, mem, rel_bias, w_in[l], b_in[l], conv_w[l], diff_lambda[l], subln_g[l],
                   mlstm_norm_g[l], w_mem_kv[l], w_branch[l], w_out[l], ln1_g[l], ln1_b[l],
                   w_router[l], router_bias[l], w_e_gate[l], w_e_up[l], w_e_down[l],
                   w_s_gate[l], w_s_up[l], w_s_down[l], ln2_g[l], ln2_b[l], l, cfg)
    return x
```
